```python
import math
import jax, jax.numpy as jnp
from jax import lax
import numpy as np

D_MODEL = 2048
BATCH = 1
SEQ = 16384
DEPTH = 1
DEC_BATCH = 32
DEC_SEQ = 32
PAST_LEN = 2048

CHUNK = 64
Q_BLOCK = 128
HEAD_DIM = 128
MIX_WIDTH = D_MODEL
DIFF_WIDTH = MIX_WIDTH // 2
FOX_WIDTH = MIX_WIDTH - DIFF_WIDTH
DIFF_HEADS = DIFF_WIDTH // (2 * HEAD_DIM)
FOX_HEADS = FOX_WIDTH // HEAD_DIM
ROT_DIM = HEAD_DIM // 4
ROPE_THETA = 500000.0
N_EXPERTS = 32
TOP_K = 4
D_FF = D_MODEL
SWIGLU_LIMIT = 7.0
SWIGLU_ALPHA = 1.702
MOE_BLOCK = 256
RMS_EPS = 1e-6
NEG_INF = -1e30
FORGET_BIAS_INIT = 3.0
IN_COLS = 3 * DIFF_WIDTH + 3 * FOX_WIDTH + FOX_HEADS
IN_SPLIT_POINTS = (DIFF_WIDTH, 2 * DIFF_WIDTH, 3 * DIFF_WIDTH,
                   3 * DIFF_WIDTH + FOX_WIDTH, 3 * DIFF_WIDTH + 2 * FOX_WIDTH,
                   3 * DIFF_WIDTH + 3 * FOX_WIDTH)

kernel_name = 'hybrid_diff_fox_moe_stream_step'


def rms_norm(x, g):
    xf = x.astype(jnp.float32)
    y = xf * lax.rsqrt(jnp.mean(xf * xf, axis=-1, keepdims=True) + RMS_EPS)
    return (y * g.astype(jnp.float32)).astype(x.dtype)


def partial_rope(x, pos):
    half = ROT_DIM // 2
    inv_freq = jnp.power(ROPE_THETA, -jnp.arange(half, dtype=jnp.float32) * (2.0 / ROT_DIM))
    ang = pos.astype(jnp.float32)[:, None] * inv_freq[None, :]
    shape = (1, x.shape[1]) + (1,) * (x.ndim - 3) + (half,)
    cos = jnp.cos(ang).reshape(shape).astype(x.dtype)
    sin = jnp.sin(ang).reshape(shape).astype(x.dtype)
    x1 = x[..., :half]
    x2 = x[..., half:ROT_DIM]
    return jnp.concatenate([x1 * cos - x2 * sin, x2 * cos + x1 * sin, x[..., ROT_DIM:]], axis=-1)


def diff_attention(q, q_pos, k, v, k_pos, lam):
    s = jnp.einsum('bqhmd,bkhmd->bhmqk', q, k).astype(jnp.float32) * (HEAD_DIM ** -0.5)
    allowed = (k_pos[None, :] // CHUNK) <= (q_pos[0][:, None] // CHUNK)
    s = jnp.where(allowed, s, NEG_INF)
    p = jax.nn.softmax(s, axis=-1)
    a = (p[:, :, 0] - lam * p[:, :, 1]).astype(v.dtype)
    return jnp.einsum('bhqk,bkhe->bqhe', a, v)


def forgetting_attention(q, c_q, q_pos, k, v, c_k, k_pos):
    s = jnp.einsum('bqhd,bkhd->bhqk', q, k).astype(jnp.float32) * (HEAD_DIM ** -0.5)
    decay = jnp.transpose(c_q, (0, 2, 1))[:, :, :, None] - c_k[:, :, None, :]
    allowed = k_pos[None, :] <= q_pos[0][:, None]
    s = jnp.where(allowed, s + decay, NEG_INF)
    p = jax.nn.softmax(s, axis=-1).astype(v.dtype)
    return jnp.einsum('bhqk,bkhd->bqhd', p, v)


def sweep_query_blocks(attn_fn, q_args):
    t = q_args[0].shape[1]

    def one_block(i):
        start = i * Q_BLOCK
        return attn_fn(*[lax.dynamic_slice_in_dim(a, start, Q_BLOCK, axis=1) for a in q_args])

    out = lax.map(one_block, jnp.arange(t // Q_BLOCK))
    out = jnp.moveaxis(out, 0, 1)
    return out.reshape((out.shape[0], t) + out.shape[3:])


def mixer_sublayer(x, pos, past, sweep, lam_init, norm_g, w_in, b_forget,
                   lq1, lk1, lq2, lk2, diff_g, fox_g, w_out):
    b, t, _ = x.shape
    xn = rms_norm(x, norm_g)
    proj = xn @ w_in
    dq, dk, dv, fq, fk, fv, fl = jnp.split(proj, IN_SPLIT_POINTS, axis=-1)
    dq = partial_rope(dq.reshape(b, t, DIFF_HEADS, 2, HEAD_DIM), pos)
    dk = partial_rope(dk.reshape(b, t, DIFF_HEADS, 2, HEAD_DIM), pos)
    dv = dv.reshape(b, t, DIFF_HEADS, 2 * HEAD_DIM)
    fq = fq.reshape(b, t, FOX_HEADS, HEAD_DIM)
    fk = fk.reshape(b, t, FOX_HEADS, HEAD_DIM)
    fv = fv.reshape(b, t, FOX_HEADS, HEAD_DIM)
    logf = jax.nn.log_sigmoid(fl.astype(jnp.float32) + b_forget.astype(jnp.float32))
    new_rows = (dk.reshape(b, t, DIFF_HEADS, 2 * HEAD_DIM), dv, fk, fv, logf)
    if past is None:
        k_d, v_d, k_f, v_f, logf_all, k_pos = dk, dv, fk, fv, logf, pos
    else:
        pk_d, pv_d, pk_f, pv_f, plogf = past
        p_len = pk_d.shape[1]
        k_d = jnp.concatenate([pk_d.reshape(b, p_len, DIFF_HEADS, 2, HEAD_DIM), dk], axis=1)
        v_d = jnp.concatenate([pv_d, dv], axis=1)
        k_f = jnp.concatenate([pk_f, fk], axis=1)
        v_f = jnp.concatenate([pv_f, fv], axis=1)
        logf_all = jnp.concatenate([plogf.astype(jnp.float32), logf], axis=1)
        k_pos = jnp.arange(p_len + t, dtype=jnp.int32)
    c = jnp.cumsum(logf_all, axis=1)
    c_q = c[:, -t:]
    c_k = jnp.transpose(c, (0, 2, 1))
    lam = (jnp.exp(jnp.sum(lq1.astype(jnp.float32) * lk1.astype(jnp.float32)))
           - jnp.exp(jnp.sum(lq2.astype(jnp.float32) * lk2.astype(jnp.float32))) + lam_init)
    q_pos = pos[None, :]
    diff_fn = lambda q, qp: diff_attention(q, qp, k_d, v_d, k_pos, lam)
    fox_fn = lambda q, cq, qp: forgetting_attention(q, cq, qp, k_f, v_f, c_k, k_pos)
    if sweep:
        out_d = sweep_query_blocks(diff_fn, (dq, q_pos))
        out_f = sweep_query_blocks(fox_fn, (fq, c_q, q_pos))
    else:
        out_d = diff_fn(dq, q_pos)
        out_f = fox_fn(fq, c_q, q_pos)
    out_d = rms_norm(out_d, diff_g) * (1.0 - lam_init)
    out_f = rms_norm(out_f, fox_g)
    mixed = jnp.concatenate([out_d.reshape(b, t, DIFF_WIDTH), out_f.reshape(b, t, FOX_WIDTH)], axis=-1)
    return x + mixed @ w_out, new_rows


def moe_ffn(xn, w_router, b_router, w_gate, b_gate, w_up, b_up, w_down, b_down):
    b, t, d = xn.shape
    n_tok = b * t
    n_assign = n_tok * TOP_K
    n_blocks = -(-n_assign // MOE_BLOCK) + N_EXPERTS
    n_rows = n_blocks * MOE_BLOCK
    xt = xn.reshape(n_tok, d)
    logits = (xt @ w_router + b_router).astype(jnp.float32)
    top_logit, top_idx = lax.top_k(logits, TOP_K)
    gates = jax.nn.softmax(top_logit, axis=-1)
    expert_of = top_idx.reshape(-1).astype(jnp.int32)
    token_of = jnp.arange(n_assign, dtype=jnp.int32) // TOP_K
    gate_of = gates.reshape(-1)
    order = jnp.argsort(expert_of)
    e_sorted = expert_of[order]
    counts = jnp.bincount(expert_of, length=N_EXPERTS).astype(jnp.int32)
    starts = jnp.cumsum(counts) - counts
    padded = (counts + MOE_BLOCK - 1) // MOE_BLOCK * MOE_BLOCK
    padded_end = jnp.cumsum(padded)
    padded_start = padded_end - padded
    dest = padded_start[e_sorted] + (jnp.arange(n_assign, dtype=jnp.int32) - starts[e_sorted])
    row_tok = jnp.zeros((n_rows,), jnp.int32).at[dest].set(token_of[order])
    row_gate = jnp.zeros((n_rows,), jnp.float32).at[dest].set(gate_of[order])
    block_start = jnp.arange(n_blocks, dtype=jnp.int32) * MOE_BLOCK
    block_expert = jnp.minimum(jnp.searchsorted(padded_end, block_start, side='right'), N_EXPERTS - 1)

    def expert_block(args):
        tok, gate, e = args
        xb = xt[tok]
        g = jnp.minimum(xb @ w_gate[e] + b_gate[e], SWIGLU_LIMIT)
        u = jnp.clip(xb @ w_up[e] + b_up[e], -SWIGLU_LIMIT, SWIGLU_LIMIT)
        h = (u + 1.0) * (g * jax.nn.sigmoid(SWIGLU_ALPHA * g))
        return (h @ w_down[e] + b_down[e]) * gate[:, None].astype(xb.dtype)

    rows = lax.map(expert_block, (row_tok.reshape(n_blocks, MOE_BLOCK),
                                  row_gate.reshape(n_blocks, MOE_BLOCK), block_expert))
    y = jax.ops.segment_sum(rows.reshape(n_rows, d), row_tok, num_segments=n_tok)
    return y.reshape(b, t, d)


def ffn_sublayer(x, norm_g, w_router, b_router, w_gate, b_gate, w_up, b_up, w_down, b_down):
    return x + moe_ffn(rms_norm(x, norm_g), w_router, b_router, w_gate, b_gate, w_up, b_up, w_down, b_down)


def setup_inputs(seed: int = 0) -> dict:
    key = jax.random.key(seed)
    ks = jax.random.split(key, 32)

    def nrm(k, shape, scale):
        return jax.random.normal(k, shape, jnp.float32) * scale

    return {
        'x_prompt': nrm(ks[0], (BATCH, SEQ, D_MODEL), 1.0),
        'x_sample': nrm(ks[1], (DEC_BATCH, DEC_SEQ, D_MODEL), 1.0),
        'cache_diff_k': nrm(ks[2], (DEPTH, DEC_BATCH, PAST_LEN, DIFF_HEADS, 2 * HEAD_DIM), 1.0),
        'cache_diff_v': nrm(ks[3], (DEPTH, DEC_BATCH, PAST_LEN, DIFF_HEADS, 2 * HEAD_DIM), 1.0),
        'cache_fox_k': nrm(ks[4], (DEPTH, DEC_BATCH, PAST_LEN, FOX_HEADS, HEAD_DIM), 1.0),
        'cache_fox_v': nrm(ks[5], (DEPTH, DEC_BATCH, PAST_LEN, FOX_HEADS, HEAD_DIM), 1.0),
        'cache_fox_logf': jax.nn.log_sigmoid(FORGET_BIAS_INIT + nrm(ks[6], (DEPTH, DEC_BATCH, PAST_LEN, FOX_HEADS), 1.0)),
        'attn_norm_g': 1.0 + nrm(ks[7], (DEPTH, D_MODEL), 0.05),
        'w_in': nrm(ks[8], (DEPTH, D_MODEL, IN_COLS), D_MODEL ** -0.5),
        'b_forget': FORGET_BIAS_INIT + nrm(ks[9], (DEPTH, FOX_HEADS), 0.1),
        'lambda_q1': nrm(ks[10], (DEPTH, HEAD_DIM), 0.1),
        'lambda_k1': nrm(ks[11], (DEPTH, HEAD_DIM), 0.1),
        'lambda_q2': nrm(ks[12], (DEPTH, HEAD_DIM), 0.1),
        'lambda_k2': nrm(ks[13], (DEPTH, HEAD_DIM), 0.1),
        'diff_out_norm_g': 1.0 + nrm(ks[14], (DEPTH, 2 * HEAD_DIM), 0.05),
        'fox_out_norm_g': 1.0 + nrm(ks[15], (DEPTH, HEAD_DIM), 0.05),
        'w_out': nrm(ks[16], (DEPTH, MIX_WIDTH, D_MODEL), MIX_WIDTH ** -0.5),
        'ffn_norm_g': 1.0 + nrm(ks[17], (DEPTH, D_MODEL), 0.05),
        'w_router': nrm(ks[18], (DEPTH, D_MODEL, N_EXPERTS), D_MODEL ** -0.5),
        'b_router': nrm(ks[19], (DEPTH, N_EXPERTS), 0.01),
        'w_gate': nrm(ks[20], (DEPTH, N_EXPERTS, D_MODEL, D_FF), D_MODEL ** -0.5),
        'b_gate': nrm(ks[21], (DEPTH, N_EXPERTS, D_FF), 0.01),
        'w_up': nrm(ks[22], (DEPTH, N_EXPERTS, D_MODEL, D_FF), D_MODEL ** -0.5),
        'b_up': nrm(ks[23], (DEPTH, N_EXPERTS, D_FF), 0.01),
        'w_down': nrm(ks[24], (DEPTH, N_EXPERTS, D_FF, D_MODEL), D_FF ** -0.5),
        'b_down': nrm(ks[25], (DEPTH, N_EXPERTS, D_MODEL), 0.01),
        'final_norm_g': 1.0 + nrm(ks[26], (D_MODEL,), 0.05),
    }


def reference(x_prompt, x_sample, cache_diff_k, cache_diff_v, cache_fox_k, cache_fox_v, cache_fox_logf,
              attn_norm_g, w_in, b_forget, lambda_q1, lambda_k1, lambda_q2, lambda_k2,
              diff_out_norm_g, fox_out_norm_g, w_out, ffn_norm_g, w_router, b_router,
              w_gate, b_gate, w_up, b_up, w_down, b_down, final_norm_g):
    past_len = cache_diff_k.shape[2]
    pos_prompt = jnp.arange(x_prompt.shape[1], dtype=jnp.int32)
    pos_sample = past_len + jnp.arange(x_sample.shape[1], dtype=jnp.int32)
    hp, hs = x_prompt, x_sample
    rows_p, rows_s = [], []
    for l in range(DEPTH):
        lam_init = 0.8 - 0.6 * math.exp(-0.3 * l)
        mix_w = (attn_norm_g[l], w_in[l], b_forget[l], lambda_q1[l], lambda_k1[l], lambda_q2[l],
                 lambda_k2[l], diff_out_norm_g[l], fox_out_norm_g[l], w_out[l])
        past = (cache_diff_k[l], cache_diff_v[l], cache_fox_k[l], cache_fox_v[l], cache_fox_logf[l])
        hp, rp = mixer_sublayer(hp, pos_prompt, None, True, lam_init, *mix_w)
        hs, rs = mixer_sublayer(hs, pos_sample, past, False, lam_init, *mix_w)
        moe_w = (ffn_norm_g[l], w_router[l], b_router[l], w_gate[l], b_gate[l],
                 w_up[l], b_up[l], w_down[l], b_down[l])
        hp = ffn_sublayer(hp, *moe_w)
        hs = ffn_sublayer(hs, *moe_w)
        rows_p.append(rp)
        rows_s.append(rs)
    y_prompt = rms_norm(hp, final_norm_g)
    y_sample = rms_norm(hs, final_norm_g)
    new_diff_k_p = jnp.stack([r[0] for r in rows_p])
    new_diff_v_p = jnp.stack([r[1] for r in rows_p])
    new_fox_k_p = jnp.stack([r[2] for r in rows_p])
    new_fox_v_p = jnp.stack([r[3] for r in rows_p])
    new_fox_logf_p = jnp.stack([r[4] for r in rows_p])
    new_diff_k_s = jnp.stack([r[0] for r in rows_s])
    new_diff_v_s = jnp.stack([r[1] for r in rows_s])
    new_fox_k_s = jnp.stack([r[2] for r in rows_s])
    new_fox_v_s = jnp.stack([r[3] for r in rows_s])
    new_fox_logf_s = jnp.stack([r[4] for r in rows_s])
    return (y_prompt, y_sample, new_diff_k_p, new_diff_v_p, new_fox_k_p, new_fox_v_p, new_fox_logf_p,
            new_diff_k_s, new_diff_v_s, new_fox_k_s, new_fox_v_s, new_fox_logf_s)
```

```python
import functools
import math

import jax
import jax.numpy as jnp
from jax import lax
from jax.experimental import pallas as pl
from jax.experimental.pallas import tpu as pltpu

HEAD_DIM = 128
CHUNK_SHIFT = 6
ROT_DIM = HEAD_DIM // 4
ROPE_THETA = 500000.0
TOP_K = 4
SWIGLU_LIMIT = 7.0
SWIGLU_ALPHA = 1.702
RMS_EPS = 1e-6
NEG_INF = -1e30
LOG2E = 1.4426950408889634
LANES = 128
MOE_ROWS = 256
VMEM_LIMIT = 56 * 1024 * 1024

F32 = jnp.float32
BF16 = jnp.bfloat16


def _cparams(sem, vmem=VMEM_LIMIT):
    return pltpu.CompilerParams(dimension_semantics=sem, vmem_limit_bytes=vmem)


def _tile(n, pref):
    if n <= pref:
        return n
    t = max(c for c in range(8, pref + 1, 8) if n % c == 0)
    return t


def _dot_nt(a, b):
    return lax.dot_general(a, b, (((1,), (1,)), ((), ())), preferred_element_type=F32)


def _dot(a, b):
    return jnp.dot(a, b, preferred_element_type=F32)


def _rmsnorm_kernel(x_ref, g_ref, o_ref):
    x = x_ref[...]
    y = x * lax.rsqrt(jnp.mean(x * x, axis=-1, keepdims=True) + RMS_EPS)
    o_ref[...] = (y * g_ref[...]).astype(o_ref.dtype)


def _rmsnorm_bf16(x, g):
    m, d = x.shape
    tm = _tile(m, 512)
    return pl.pallas_call(
        _rmsnorm_kernel,
        out_shape=jax.ShapeDtypeStruct((m, d), BF16),
        grid=(m // tm,),
        in_specs=[pl.BlockSpec((tm, d), lambda i: (i, 0)),
                  pl.BlockSpec((1, d), lambda i: (0, 0))],
        out_specs=pl.BlockSpec((tm, d), lambda i: (i, 0)),
        compiler_params=_cparams(("parallel",)),
        name="rmsnorm_bf16",
    )(x, g.reshape(1, d))


def _proj_kernel(*refs, rope, scale, want_f32, want_bf16, head_major, head_w):
    it = iter(refs)
    x_ref = next(it)
    w_ref = next(it)
    if rope:
        c_ref, sa_ref, sb_ref = next(it), next(it), next(it)
    o32_ref = next(it) if want_f32 else None
    o16_ref = next(it) if want_bf16 else None

    n = w_ref.shape[1]
    y = _dot(x_ref[...], w_ref[...])
    for h in range(n // HEAD_DIM):
        sl = slice(h * HEAD_DIM, (h + 1) * HEAD_DIM)
        yh = y[:, sl]
        if rope:
            yh = (yh * c_ref[...]
                  + pltpu.roll(yh, HEAD_DIM - ROT_DIM // 2, 1) * sa_ref[...]
                  + pltpu.roll(yh, ROT_DIM // 2, 1) * sb_ref[...])
        elif scale != 1.0:
            yh = yh * scale
        if want_f32:
            o32_ref[:, sl] = yh
        if want_bf16:
            if head_major:
                per = head_w // HEAD_DIM
                o16_ref[h // per, :, (h % per) * HEAD_DIM:(h % per + 1) * HEAD_DIM] = yh.astype(BF16)
            else:
                o16_ref[:, sl] = yh.astype(BF16)


def _proj(xn, w, rope_tabs=None, scale=1.0, want_f32=False, want_bf16=True,
          head_major=False, head_w=HEAD_DIM):
    m, d = xn.shape
    n = w.shape[1]
    tm = _tile(m, 512)
    rope = rope_tabs is not None
    in_specs = [pl.BlockSpec((tm, d), lambda i: (i, 0)),
                pl.BlockSpec((d, n), lambda i: (0, 0))]
    args = [xn, w]
    if rope:
        in_specs += [pl.BlockSpec((tm, HEAD_DIM), lambda i: (i, 0))] * 3
        args += list(rope_tabs)
    out_shape, out_specs = [], []
    if want_f32:
        out_shape.append(jax.ShapeDtypeStruct((m, n), F32))
        out_specs.append(pl.BlockSpec((tm, n), lambda i: (i, 0)))
    if want_bf16:
        if head_major:
            nh = n // head_w
            out_shape.append(jax.ShapeDtypeStruct((nh, m, head_w), BF16))
            out_specs.append(pl.BlockSpec((nh, tm, head_w), lambda i: (0, i, 0)))
        else:
            out_shape.append(jax.ShapeDtypeStruct((m, n), BF16))
            out_specs.append(pl.BlockSpec((tm, n), lambda i: (i, 0)))
    kern = functools.partial(_proj_kernel, rope=rope, scale=scale, want_f32=want_f32,
                             want_bf16=want_bf16, head_major=head_major, head_w=head_w)
    return pl.pallas_call(
        kern, out_shape=out_shape, grid=(m // tm,), in_specs=in_specs, out_specs=out_specs,
        compiler_params=_cparams(("parallel",)), name="in_proj",
    )(*args)


def _rope_tables(pos, scale):
    half = ROT_DIM // 2
    t = pos.shape[0]
    inv_freq = jnp.power(ROPE_THETA, -jnp.arange(half, dtype=F32) * (2.0 / ROT_DIM))
    ang = pos.astype(F32)[:, None] * inv_freq[None, :]
    cos, sin = jnp.cos(ang), jnp.sin(ang)
    zeros = lambda w: jnp.zeros((t, w), F32)
    c = jnp.concatenate([cos, cos, jnp.ones((t, HEAD_DIM - ROT_DIM), F32)], axis=1)
    sa = jnp.concatenate([-sin, zeros(HEAD_DIM - half)], axis=1)
    sb = jnp.concatenate([zeros(half), sin, zeros(HEAD_DIM - ROT_DIM)], axis=1)
    return c * scale, sa * scale, sb * scale


def _logf_kernel(x_ref, w_ref, b_ref, o_ref):
    z = _dot(x_ref[...], w_ref[...]) + b_ref[...]
    o_ref[...] = jnp.minimum(z, 0.0) - jnp.log(1.0 + jnp.exp(-jnp.abs(z)))


def _logf(xn, w_pad, b_pad):
    m, d = xn.shape
    tm = _tile(m, 512)
    return pl.pallas_call(
        _logf_kernel,
        out_shape=jax.ShapeDtypeStruct((m, LANES), F32),
        grid=(m // tm,),
        in_specs=[pl.BlockSpec((tm, d), lambda i: (i, 0)),
                  pl.BlockSpec((d, LANES), lambda i: (0, 0)),
                  pl.BlockSpec((1, LANES), lambda i: (0, 0))],
        out_specs=pl.BlockSpec((tm, LANES), lambda i: (i, 0)),
        compiler_params=_cparams(("parallel",)), name="forget_logits",
    )(xn, w_pad, b_pad)


def _cumsum_kernel(x_ref, o_ref, carry_ref):
    @pl.when(pl.program_id(0) == 0)
    def _():
        carry_ref[...] = jnp.zeros_like(carry_ref)

    tb = x_ref.shape[0]
    row = lax.broadcasted_iota(jnp.int32, (tb, tb), 0)
    col = lax.broadcasted_iota(jnp.int32, (tb, tb), 1)
    tri = jnp.where(col <= row, 1.0, 0.0).astype(F32)
    c = jnp.dot(tri, x_ref[...], preferred_element_type=F32,
                precision=lax.Precision.HIGHEST) + carry_ref[...]
    o_ref[...] = c
    carry_ref[...] = c[tb - 1:tb, :]


def _cumsum_rows(x, tb):
    t, c = x.shape
    assert t % tb == 0 and c % LANES == 0
    return pl.pallas_call(
        _cumsum_kernel,
        out_shape=jax.ShapeDtypeStruct((t, c), F32),
        grid=(t // tb,),
        in_specs=[pl.BlockSpec((tb, c), lambda i: (i, 0))],
        out_specs=pl.BlockSpec((tb, c), lambda i: (i, 0)),
        scratch_shapes=[pltpu.VMEM((1, c), F32)],
        compiler_params=_cparams(("arbitrary",)), name="cumsum_rows",
    )(x)


def _lambda_value(lq1, lk1, lq2, lk2, lam_init):
    a = jnp.sum(lq1[...] * lk1[...], axis=-1, keepdims=True)
    b = jnp.sum(lq2[...] * lk2[...], axis=-1, keepdims=True)
    return jnp.exp(a) - jnp.exp(b) + lam_init


def _head_rmsnorm(o, g):
    return o * lax.rsqrt(jnp.mean(o * o, axis=-1, keepdims=True) + RMS_EPS) * g


def _online_update(s, m, l, acc_ref, v):
    m_new = jnp.maximum(m, jnp.max(s, axis=-1, keepdims=True))
    alpha = jnp.exp2(m - m_new)
    p = jnp.exp2(s - m_new)
    l_new = alpha * l + jnp.sum(p, axis=-1, keepdims=True)
    acc_ref[...] = alpha * acc_ref[...] + _dot(p.astype(BF16), v)
    return m_new, l_new


def _diff_prompt_kernel(q_ref, k_ref, v_ref, g_ref, lq1, lk1, lq2, lk2, o_ref,
                        acc1, acc2, *, tq, lam_init):
    i = pl.program_id(1)
    q = q_ref[0]
    q1, q2 = q[:, :HEAD_DIM], q[:, HEAD_DIM:]
    acc1[...] = jnp.zeros_like(acc1)
    acc2[...] = jnp.zeros_like(acc2)

    def tile(j, state, masked):
        m1, l1, m2, l2 = state
        start = pl.multiple_of(j * tq, tq)
        k = k_ref[0, pl.ds(start, tq), :]
        v = v_ref[0, pl.ds(start, tq), :]
        s1 = _dot_nt(q1, k[:, :HEAD_DIM])
        s2 = _dot_nt(q2, k[:, HEAD_DIM:])
        if masked:
            row = lax.broadcasted_iota(jnp.int32, (tq, tq), 0)
            col = lax.broadcasted_iota(jnp.int32, (tq, tq), 1)
            ok = (col >> CHUNK_SHIFT) <= (row >> CHUNK_SHIFT)
            s1 = jnp.where(ok, s1, NEG_INF)
            s2 = jnp.where(ok, s2, NEG_INF)
        m1, l1 = _online_update(s1, m1, l1, acc1, v)
        m2, l2 = _online_update(s2, m2, l2, acc2, v)
        return m1, l1, m2, l2

    neg = jnp.full((tq, 1), NEG_INF, F32)
    zero = jnp.zeros((tq, 1), F32)
    state = lax.fori_loop(0, i, lambda j, s: tile(j, s, False), (neg, zero, neg, zero))
    _, l1, _, l2 = tile(i, state, True)

    lam = _lambda_value(lq1, lk1, lq2, lk2, lam_init)
    o = acc1[...] / l1 - lam * (acc2[...] / l2)
    o_ref[...] = (_head_rmsnorm(o, g_ref[...]) * (1.0 - lam_init)).astype(o_ref.dtype)


def _diff_prompt_attention(q, k, v, g, lams, lam_init):
    nh, t, w = q.shape
    tq = _tile(t, 512)
    assert tq % (1 << CHUNK_SHIFT) == 0
    vec = pl.BlockSpec((1, HEAD_DIM), lambda h, i: (0, 0))
    kern = functools.partial(_diff_prompt_kernel, tq=tq, lam_init=lam_init)
    return pl.pallas_call(
        kern,
        out_shape=jax.ShapeDtypeStruct((t, nh * w), BF16),
        grid=(nh, t // tq),
        in_specs=[pl.BlockSpec((1, tq, w), lambda h, i: (h, i, 0)),
                  pl.BlockSpec((1, t, w), lambda h, i: (h, 0, 0)),
                  pl.BlockSpec((1, t, w), lambda h, i: (h, 0, 0)),
                  pl.BlockSpec((1, w), lambda h, i: (0, 0)),
                  vec, vec, vec, vec],
        out_specs=pl.BlockSpec((tq, w), lambda h, i: (i, h)),
        scratch_shapes=[pltpu.VMEM((tq, w), F32), pltpu.VMEM((tq, w), F32)],
        compiler_params=_cparams(("parallel", "arbitrary")), name="diff_prompt_attention",
    )(q, k, v, g, *lams)


def _fox_prompt_kernel(q_ref, k_ref, v_ref, ck_ref, g_ref, o_ref, acc, *, tq):
    i = pl.program_id(1)
    q = q_ref[0]
    acc[...] = jnp.zeros_like(acc)

    def tile(j, state, masked):
        m, l = state
        start = pl.multiple_of(j * tq, tq)
        k = k_ref[0, pl.ds(start, tq), :]
        v = v_ref[0, pl.ds(start, tq), :]
        s = _dot_nt(q, k) - ck_ref[0, j] * LOG2E
        if masked:
            row = lax.broadcasted_iota(jnp.int32, (tq, tq), 0)
            col = lax.broadcasted_iota(jnp.int32, (tq, tq), 1)
            s = jnp.where(col <= row, s, NEG_INF)
        return _online_update(s, m, l, acc, v)

    state = (jnp.full((tq, 1), NEG_INF, F32), jnp.zeros((tq, 1), F32))
    state = lax.fori_loop(0, i, lambda j, s: tile(j, s, False), state)
    _, l = tile(i, state, True)
    o_ref[...] = _head_rmsnorm(acc[...] / l, g_ref[...]).astype(o_ref.dtype)


def _fox_prompt_attention(q, k, v, ck, g):
    nh, t, w = q.shape
    tq = ck.shape[-1]
    kern = functools.partial(_fox_prompt_kernel, tq=tq)
    return pl.pallas_call(
        kern,
        out_shape=jax.ShapeDtypeStruct((t, nh * w), BF16),
        grid=(nh, t // tq),
        in_specs=[pl.BlockSpec((1, tq, w), lambda h, i: (h, i, 0)),
                  pl.BlockSpec((1, t, w), lambda h, i: (h, 0, 0)),
                  pl.BlockSpec((1, t, w), lambda h, i: (h, 0, 0)),
                  pl.BlockSpec((1, t // tq, 1, tq), lambda h, i: (h, 0, 0, 0)),
                  pl.BlockSpec((1, w), lambda h, i: (0, 0))],
        out_specs=pl.BlockSpec((tq, w), lambda h, i: (i, h)),
        scratch_shapes=[pltpu.VMEM((tq, w), F32)],
        compiler_params=_cparams(("parallel", "arbitrary")), name="fox_prompt_attention",
    )(q, k, v, ck, g)


def _two_segment_softmax(sp, sn):
    m = jnp.maximum(jnp.max(sp, axis=-1, keepdims=True), jnp.max(sn, axis=-1, keepdims=True))
    pp = jnp.exp2(sp - m)
    pn = jnp.exp2(sn - m)
    l = jnp.sum(pp, axis=-1, keepdims=True) + jnp.sum(pn, axis=-1, keepdims=True)
    return pp.astype(BF16), pn.astype(BF16), l


def _diff_sample_kernel(q_ref, kn_ref, vn_ref, kp_ref, vp_ref, g_ref, lq1, lk1, lq2, lk2,
                        o_ref, *, past_len, lam_init, n_heads):
    t = q_ref.shape[1]
    w = 2 * HEAD_DIM
    row = lax.broadcasted_iota(jnp.int32, (t, t), 0) + past_len
    col = lax.broadcasted_iota(jnp.int32, (t, t), 1) + past_len
    ok = (col >> CHUNK_SHIFT) <= (row >> CHUNK_SHIFT)
    lam = _lambda_value(lq1, lk1, lq2, lk2, lam_init)
    for h in range(n_heads):
        sl = slice(h * w, (h + 1) * w)
        q = q_ref[0, :, sl]
        kn = kn_ref[0, :, sl]
        vn = vn_ref[0, :, sl]
        kp = kp_ref[0, :, sl].astype(BF16)
        vp = vp_ref[0, :, sl].astype(BF16)
        outs = []
        for half in range(2):
            hs = slice(half * HEAD_DIM, (half + 1) * HEAD_DIM)
            sp = _dot_nt(q[:, hs], kp[:, hs])
            sn = jnp.where(ok, _dot_nt(q[:, hs], kn[:, hs]), NEG_INF)
            pp, pn, l = _two_segment_softmax(sp, sn)
            outs.append((_dot(pp, vp) + _dot(pn, vn)) / l)
        o = outs[0] - lam * outs[1]
        o_ref[0, :, sl] = (_head_rmsnorm(o, g_ref[...]) * (1.0 - lam_init)).astype(o_ref.dtype)


def _diff_sample_attention(q, kn, vn, kp, vp, g, lams, lam_init):
    b, t, n = q.shape
    past = kp.shape[1]
    new = pl.BlockSpec((1, t, n), lambda i: (i, 0, 0))
    old = pl.BlockSpec((1, past, n), lambda i: (i, 0, 0))
    vec = pl.BlockSpec((1, HEAD_DIM), lambda i: (0, 0))
    kern = functools.partial(_diff_sample_kernel, past_len=past, lam_init=lam_init,
                             n_heads=n // (2 * HEAD_DIM))
    return pl.pallas_call(
        kern,
        out_shape=jax.ShapeDtypeStruct((b, t, n), BF16),
        grid=(b,),
        in_specs=[new, new, new, old, old,
                  pl.BlockSpec((1, 2 * HEAD_DIM), lambda i: (0, 0)), vec, vec, vec, vec],
        out_specs=new,
        compiler_params=_cparams(("parallel",)), name="diff_sample_attention",
    )(q, kn, vn, kp, vp, g, *lams)


def _fox_sample_kernel(q_ref, kn_ref, vn_ref, kp_ref, vp_ref, ckp_ref, ckn_ref, g_ref, o_ref,
                       *, n_heads):
    t = q_ref.shape[1]
    row = lax.broadcasted_iota(jnp.int32, (t, t), 0)
    col = lax.broadcasted_iota(jnp.int32, (t, t), 1)
    ok = col <= row
    for h in range(n_heads):
        sl = slice(h * HEAD_DIM, (h + 1) * HEAD_DIM)
        q = q_ref[0, :, sl]
        kp = kp_ref[0, :, sl].astype(BF16)
        vp = vp_ref[0, :, sl].astype(BF16)
        sp = _dot_nt(q, kp) - ckp_ref[0, h] * LOG2E
        sn = jnp.where(ok, _dot_nt(q, kn_ref[0, :, sl]) - ckn_ref[0, h] * LOG2E, NEG_INF)
        pp, pn, l = _two_segment_softmax(sp, sn)
        o = (_dot(pp, vp) + _dot(pn, vn_ref[0, :, sl])) / l
        o_ref[0, :, sl] = _head_rmsnorm(o, g_ref[...]).astype(o_ref.dtype)


def _fox_sample_attention(q, kn, vn, kp, vp, ckp, ckn, g):
    b, t, n = q.shape
    past = kp.shape[1]
    nh = n // HEAD_DIM
    new = pl.BlockSpec((1, t, n), lambda i: (i, 0, 0))
    old = pl.BlockSpec((1, past, n), lambda i: (i, 0, 0))
    kern = functools.partial(_fox_sample_kernel, n_heads=nh)
    return pl.pallas_call(
        kern,
        out_shape=jax.ShapeDtypeStruct((b, t, n), BF16),
        grid=(b,),
        in_specs=[new, new, new, old, old,
                  pl.BlockSpec((1, nh, 1, past), lambda i: (i, 0, 0, 0)),
                  pl.BlockSpec((1, nh, 1, t), lambda i: (i, 0, 0, 0)),
                  pl.BlockSpec((1, HEAD_DIM), lambda i: (0, 0))],
        out_specs=new,
        compiler_params=_cparams(("parallel",)), name="fox_sample_attention",
    )(q, kn, vn, kp, vp, ckp, ckn, g)


def _outproj_router_kernel(md_ref, mf_ref, wd_ref, wf_ref, x_ref, g_ref, wr_ref, br_ref,
                           h_ref, hn_ref, idx_ref, gate_ref):
    h = x_ref[...] + _dot(md_ref[...], wd_ref[...]) + _dot(mf_ref[...], wf_ref[...])
    h_ref[...] = h
    hn = h * lax.rsqrt(jnp.mean(h * h, axis=-1, keepdims=True) + RMS_EPS) * g_ref[...]
    hn_ref[...] = hn
    logits = jnp.dot(hn, wr_ref[...], preferred_element_type=F32,
                     precision=lax.Precision.HIGHEST) + br_ref[...]
    lane = lax.broadcasted_iota(jnp.int32, logits.shape, 1)
    work = logits
    idx_out = jnp.zeros(logits.shape, jnp.int32)
    val_out = jnp.zeros(logits.shape, F32)
    top = None
    denom = None
    for k in range(TOP_K):
        mx = jnp.max(work, axis=-1, keepdims=True)
        first = jnp.min(jnp.where(work == mx, lane, LANES), axis=-1, keepdims=True)
        work = jnp.where(lane == first, -jnp.inf, work)
        if k == 0:
            top = mx
        e = jnp.exp(mx - top)
        denom = e if k == 0 else denom + e
        idx_out = jnp.where(lane == k, first, idx_out)
        val_out = jnp.where(lane == k, e, val_out)
    idx_ref[...] = idx_out
    gate_ref[...] = val_out / denom


def _outproj_router(mix_d, mix_f, w_d, w_f, x, g, w_router_pad, b_router_pad):
    m, d = x.shape
    kd, kf = mix_d.shape[1], mix_f.shape[1]
    tm = _tile(m, 256)
    row = lambda w: pl.BlockSpec((tm, w), lambda i: (i, 0))
    full = lambda r, c: pl.BlockSpec((r, c), lambda i: (0, 0))
    return pl.pallas_call(
        _outproj_router_kernel,
        out_shape=[jax.ShapeDtypeStruct((m, d), F32), jax.ShapeDtypeStruct((m, d), F32),
                   jax.ShapeDtypeStruct((m, LANES), jnp.int32),
                   jax.ShapeDtypeStruct((m, LANES), F32)],
        grid=(m // tm,),
        in_specs=[row(kd), row(kf), full(kd, d), full(kf, d), row(d), full(1, d),
                  full(d, LANES), full(1, LANES)],
        out_specs=[row(d), row(d), row(LANES), row(LANES)],
        compiler_params=_cparams(("parallel",)), name="outproj_router",
    )(mix_d, mix_f, w_d, w_f, x, g, w_router_pad, b_router_pad)


def _rank_kernel(idx_ref, rank_ref, count_ref, carry_ref):
    @pl.when(pl.program_id(0) == 0)
    def _():
        carry_ref[...] = jnp.zeros_like(carry_ref)

    tm = idx_ref.shape[0]
    idx = idx_ref[...]
    lane = lax.broadcasted_iota(jnp.int32, (tm, LANES), 1)
    picks = [lane == idx[:, k:k + 1] for k in range(TOP_K)]
    onehot = jnp.zeros((tm, LANES), F32)
    for p in picks:
        onehot = onehot + jnp.where(p, 1.0, 0.0)
    row = lax.broadcasted_iota(jnp.int32, (tm, tm), 0)
    col = lax.broadcasted_iota(jnp.int32, (tm, tm), 1)
    strict = jnp.where(col < row, 1.0, 0.0).astype(BF16)
    before = _dot(strict, onehot.astype(BF16)) + carry_ref[...]
    rank = jnp.zeros((tm, LANES), F32)
    for k, p in enumerate(picks):
        r = jnp.sum(jnp.where(p, before, 0.0), axis=-1, keepdims=True)
        rank = jnp.where(lane == k, r, rank)
    rank_ref[...] = rank.astype(jnp.int32)
    total = before[tm - 1:tm, :] + onehot[tm - 1:tm, :]
    carry_ref[...] = total
    count_ref[...] = total.astype(jnp.int32)


def _expert_ranks(idx):
    n_tok = idx.shape[0]
    tm = _tile(n_tok, 512)
    return pl.pallas_call(
        _rank_kernel,
        out_shape=[jax.ShapeDtypeStruct((n_tok, LANES), jnp.int32),
                   jax.ShapeDtypeStruct((1, LANES), jnp.int32)],
        grid=(n_tok // tm,),
        in_specs=[pl.BlockSpec((tm, LANES), lambda i: (i, 0))],
        out_specs=[pl.BlockSpec((tm, LANES), lambda i: (i, 0)),
                   pl.BlockSpec((1, LANES), lambda i: (0, 0))],
        scratch_shapes=[pltpu.VMEM((1, LANES), F32)],
        compiler_params=_cparams(("arbitrary",)), name="expert_ranks",
    )(idx)


def _row_copy(src_ref, dst_ref, sem, src_row, dst_row):
    return pltpu.make_async_copy(src_ref.at[pl.ds(src_row, 1)], dst_ref.at[pl.ds(dst_row, 1)], sem)


def _gather_rows(idx_ref, src_ref, dst_ref, sem, n_rows):
    def start(r, carry):
        _row_copy(src_ref, dst_ref, sem, idx_ref[0, 0, r], r).start()
        return carry

    def wait(r, carry):
        _row_copy(src_ref, dst_ref, sem, 0, r).wait()
        return carry

    lax.fori_loop(0, n_rows, start, 0)
    lax.fori_loop(0, n_rows, wait, 0)


def _dispatch_kernel(tok_ref, hn_ref, o_ref, buf, sem):
    _gather_rows(tok_ref, hn_ref, buf, sem, buf.shape[0])
    o_ref[...] = buf[...].astype(o_ref.dtype)


def _dispatch(row_tok, hn):
    n_blocks, _, rows = row_tok.shape
    d = hn.shape[1]
    return pl.pallas_call(
        _dispatch_kernel,
        out_shape=jax.ShapeDtypeStruct((n_blocks * rows, d), BF16),
        grid=(n_blocks,),
        in_specs=[pl.BlockSpec((1, 1, rows), lambda b: (b, 0, 0), memory_space=pltpu.SMEM),
                  pl.BlockSpec(memory_space=pl.ANY)],
        out_specs=pl.BlockSpec((rows, d), lambda b: (b, 0)),
        scratch_shapes=[pltpu.VMEM((rows, d), hn.dtype), pltpu.SemaphoreType.DMA],
        compiler_params=_cparams(("arbitrary",)), name="moe_dispatch",
    )(row_tok, hn)


def _combine_kernel(dest_ref, y_ref, h_ref, gate_ref, g_ref, o_ref, buf, sem, *, tm):
    _gather_rows(dest_ref, y_ref, buf, sem, TOP_K * tm)
    gates = gate_ref[...]
    out = h_ref[...]
    for k in range(TOP_K):
        out = out + buf[k * tm:(k + 1) * tm, :] * gates[:, k:k + 1]
    out = out * lax.rsqrt(jnp.mean(out * out, axis=-1, keepdims=True) + RMS_EPS)
    o_ref[...] = out * g_ref[...]


def _combine(dest, y_rows, h, gates, g):
    n_tiles, _, per = dest.shape
    tm = per // TOP_K
    m, d = h.shape
    kern = functools.partial(_combine_kernel, tm=tm)
    return pl.pallas_call(
        kern,
        out_shape=jax.ShapeDtypeStruct((m, d), F32),
        grid=(n_tiles,),
        in_specs=[pl.BlockSpec((1, 1, per), lambda i: (i, 0, 0), memory_space=pltpu.SMEM),
                  pl.BlockSpec(memory_space=pl.ANY),
                  pl.BlockSpec((tm, d), lambda i: (i, 0)),
                  pl.BlockSpec((tm, LANES), lambda i: (i, 0)),
                  pl.BlockSpec((1, d), lambda i: (0, 0))],
        out_specs=pl.BlockSpec((tm, d), lambda i: (i, 0)),
        scratch_shapes=[pltpu.VMEM((per, d), y_rows.dtype), pltpu.SemaphoreType.DMA],
        compiler_params=_cparams(("arbitrary",)), name="moe_combine",
    )(dest, y_rows, h, gates, g)


def _expert_changed(be_ref, rb):
    prev = be_ref[jnp.maximum(rb - 1, 0)]
    return jnp.logical_or(rb == 0, be_ref[rb] != prev)


def _moe_up_kernel(be_ref, used_ref, x_ref, wg_ref, wu_ref, bg_ref, bu_ref, o_ref, wg16, wu16):
    rb = pl.program_id(1)

    @pl.when(_expert_changed(be_ref, rb))
    def _():
        wg16[...] = wg_ref[0].astype(BF16)
        wu16[...] = wu_ref[0].astype(BF16)

    @pl.when(rb < used_ref[0])
    def _():
        x = x_ref[...]
        gate = jnp.minimum(_dot(x, wg16[...]) + bg_ref[0], SWIGLU_LIMIT)
        up = jnp.clip(_dot(x, wu16[...]) + bu_ref[0], -SWIGLU_LIMIT, SWIGLU_LIMIT)
        act = (up + 1.0) * (gate * jax.nn.sigmoid(SWIGLU_ALPHA * gate))
        o_ref[...] = act.astype(o_ref.dtype)

    @pl.when(rb >= used_ref[0])
    def _():
        o_ref[...] = jnp.zeros_like(o_ref)


def _moe_up(block_expert, n_used, xs, w_gate, w_up, b_gate, b_up, tn):
    n_rows, d = xs.shape
    n_exp, _, d_ff = w_gate.shape
    rows = MOE_ROWS
    n_blocks = n_rows // rows
    wspec = pl.BlockSpec((1, d, tn), lambda n, rb, be, used: (be[rb], 0, n))
    bspec = pl.BlockSpec((1, 1, tn), lambda n, rb, be, used: (be[rb], 0, n))
    return pl.pallas_call(
        _moe_up_kernel,
        out_shape=jax.ShapeDtypeStruct((n_rows, d_ff), BF16),
        grid_spec=pltpu.PrefetchScalarGridSpec(
            num_scalar_prefetch=2,
            grid=(d_ff // tn, n_blocks),
            in_specs=[pl.BlockSpec((rows, d), lambda n, rb, be, used: (rb, 0)),
                      wspec, wspec, bspec, bspec],
            out_specs=pl.BlockSpec((rows, tn), lambda n, rb, be, used: (rb, n)),
            scratch_shapes=[pltpu.VMEM((d, tn), BF16), pltpu.VMEM((d, tn), BF16)]),
        compiler_params=_cparams(("arbitrary", "arbitrary")), name="moe_gate_up",
    )(block_expert, n_used, xs, w_gate, w_up, b_gate.reshape(n_exp, 1, d_ff),
      b_up.reshape(n_exp, 1, d_ff))


def _moe_down_kernel(be_ref, used_ref, a_ref, w_ref, b_ref, o_ref, w16):
    rb = pl.program_id(1)

    @pl.when(_expert_changed(be_ref, rb))
    def _():
        w16[...] = w_ref[0].astype(BF16)

    @pl.when(rb < used_ref[0])
    def _():
        o_ref[...] = _dot(a_ref[...], w16[...]) + b_ref[0]

    @pl.when(rb >= used_ref[0])
    def _():
        o_ref[...] = jnp.zeros_like(o_ref)


def _moe_down(block_expert, n_used, act, w_down, b_down, tn):
    n_rows, d_ff = act.shape
    n_exp, _, d = w_down.shape
    rows = MOE_ROWS
    return pl.pallas_call(
        _moe_down_kernel,
        out_shape=jax.ShapeDtypeStruct((n_rows, d), F32),
        grid_spec=pltpu.PrefetchScalarGridSpec(
            num_scalar_prefetch=2,
            grid=(d // tn, n_rows // rows),
            in_specs=[pl.BlockSpec((rows, d_ff), lambda n, rb, be, used: (rb, 0)),
                      pl.BlockSpec((1, d_ff, tn), lambda n, rb, be, used: (be[rb], 0, n)),
                      pl.BlockSpec((1, 1, tn), lambda n, rb, be, used: (be[rb], 0, n))],
            out_specs=pl.BlockSpec((rows, tn), lambda n, rb, be, used: (rb, n)),
            scratch_shapes=[pltpu.VMEM((d_ff, tn), BF16)]),
        compiler_params=_cparams(("arbitrary", "arbitrary")), name="moe_down",
    )(block_expert, n_used, act, w_down, b_down.reshape(n_exp, 1, d))


def _pad_lanes(a, fill=0.0):
    return jnp.pad(a, ((0, 0), (0, LANES - a.shape[1])), constant_values=fill)


def _mixer_projections(x2d, pos, norm_g, w_in, b_forget, head_major):
    d = x2d.shape[1]
    dw = (w_in.shape[1] - 8) // 6
    q_scale = HEAD_DIM ** -0.5 * LOG2E
    xn = _rmsnorm_bf16(x2d, norm_g)
    w16 = w_in.astype(BF16)
    wcol = lambda j: w16[:, j * dw:(j + 1) * dw]
    tabs_q = _rope_tables(pos, q_scale)
    tabs_k = _rope_tables(pos, 1.0)
    hm = dict(head_major=head_major)
    (dq,) = _proj(xn, wcol(0), rope_tabs=tabs_q, head_w=2 * HEAD_DIM, **hm)
    dk32, dk = _proj(xn, wcol(1), rope_tabs=tabs_k, want_f32=True, head_w=2 * HEAD_DIM, **hm)
    dv32, dv = _proj(xn, wcol(2), want_f32=True, head_w=2 * HEAD_DIM, **hm)
    (fq,) = _proj(xn, wcol(3), scale=q_scale, **hm)
    fk32, fk = _proj(xn, wcol(4), want_f32=True, **hm)
    fv32, fv = _proj(xn, wcol(5), want_f32=True, **hm)
    logf = _logf(xn, _pad_lanes(w16[:, 6 * dw:]), _pad_lanes(b_forget.reshape(1, -1)))
    return (dq, dk, dv, fq, fk, fv), (dk32, dv32, fk32, fv32), logf


def _moe(hn_all, idx_all, w_gate, w_up, w_down, b_gate, b_up, b_down):
    n_tok, d = hn_all.shape
    n_exp = w_gate.shape[0]
    rows = MOE_ROWS
    n_blocks = -(-(n_tok * TOP_K) // rows) + n_exp
    n_rows = n_blocks * rows

    rank, counts = _expert_ranks(idx_all)
    experts = idx_all[:, :TOP_K]
    counts = counts[0, :n_exp]
    padded = (counts + rows - 1) // rows * rows
    padded_end = jnp.cumsum(padded)
    padded_start = padded_end - padded
    dest = padded_start[experts] + rank[:, :TOP_K]
    token_of = jnp.broadcast_to(jnp.arange(n_tok, dtype=jnp.int32)[:, None], dest.shape)
    row_tok = jnp.zeros((n_rows,), jnp.int32).at[dest.reshape(-1)].set(token_of.reshape(-1))
    block_start = jnp.arange(n_blocks, dtype=jnp.int32) * rows
    block_expert = jnp.minimum(jnp.searchsorted(padded_end, block_start, side='right'),
                               n_exp - 1).astype(jnp.int32)
    n_used = (padded_end[-1] // rows).astype(jnp.int32).reshape(1)

    xs = _dispatch(row_tok.reshape(n_blocks, 1, rows), hn_all)
    act = _moe_up(block_expert, n_used, xs, w_gate, w_up, b_gate, b_up, tn=512)
    y_rows = _moe_down(block_expert, n_used, act, w_down, b_down, tn=512)
    return y_rows, dest


def _combine_tokens(dest, y_rows, h, gates, g):
    m = h.shape[0]
    tm = _tile(m, 256)
    dest_t = dest.reshape(m // tm, tm, TOP_K).transpose(0, 2, 1).reshape(m // tm, 1, TOP_K * tm)
    return _combine(dest_t, y_rows, h, gates, g.reshape(1, -1))


def kernel(x_prompt, x_sample, cache_diff_k, cache_diff_v, cache_fox_k, cache_fox_v, cache_fox_logf,
           attn_norm_g, w_in, b_forget, lambda_q1, lambda_k1, lambda_q2, lambda_k2,
           diff_out_norm_g, fox_out_norm_g, w_out, ffn_norm_g, w_router, b_router,
           w_gate, b_gate, w_up, b_up, w_down, b_down, final_norm_g):
    depth = w_in.shape[0]
    assert depth == 1, "single-layer stack"
    l = 0
    lam_init = 0.8 - 0.6 * math.exp(-0.3 * l)
    bp, t, d = x_prompt.shape
    assert bp == 1
    bs, ts, _ = x_sample.shape
    past = cache_diff_k.shape[2]
    n_dh = cache_diff_k.shape[3]
    n_fh = cache_fox_k.shape[3]
    dw = n_dh * 2 * HEAD_DIM
    fw = n_fh * HEAD_DIM
    n_exp = w_router.shape[2]
    lams = [v[l].reshape(1, HEAD_DIM) for v in (lambda_q1, lambda_k1, lambda_q2, lambda_k2)]
    diff_g = diff_out_norm_g[l].reshape(1, -1)
    fox_g = fox_out_norm_g[l].reshape(1, -1)

    xp = x_prompt.reshape(t, d)
    pos_p = jnp.arange(t, dtype=jnp.int32)
    (dq, dk, dv, fq, fk, fv), cache_p, logf_p = _mixer_projections(
        xp, pos_p, attn_norm_g[l], w_in[l], b_forget[l], head_major=True)
    tq = _tile(t, 512)
    c_p = _cumsum_rows(logf_p, _tile(t, 512))
    ck_p = c_p[:, :n_fh].T.reshape(n_fh, t // tq, 1, tq)
    mix_d_p = _diff_prompt_attention(dq, dk, dv, diff_g, lams, lam_init)
    mix_f_p = _fox_prompt_attention(fq, fk, fv, ck_p, fox_g)

    xs = x_sample.reshape(bs * ts, d)
    pos_s = past + jnp.arange(ts, dtype=jnp.int32)
    (sdq, sdk, sdv, sfq, sfk, sfv), cache_s, logf_s = _mixer_projections(
        xs, jnp.tile(pos_s, bs), attn_norm_g[l], w_in[l], b_forget[l], head_major=False)
    logf_all = jnp.concatenate([cache_fox_logf[l].astype(F32),
                                logf_s[:, :n_fh].reshape(bs, ts, n_fh)], axis=1)
    tt = past + ts
    lf_cols = logf_all.transpose(1, 0, 2).reshape(tt, bs * n_fh)
    pad_c = (-lf_cols.shape[1]) % LANES
    lf_cols = jnp.pad(lf_cols, ((0, 0), (0, pad_c)))
    tb = max(tb_ for tb_ in range(8, 513, 8) if tt % tb_ == 0)
    c_s = _cumsum_rows(lf_cols, tb)[:, :bs * n_fh].reshape(tt, bs, n_fh).transpose(1, 2, 0)
    ckp_s = c_s[:, :, None, :past]
    ckn_s = c_s[:, :, None, past:]
    r3 = lambda a: a.reshape(bs, ts, -1)
    mix_d_s = _diff_sample_attention(
        r3(sdq), r3(sdk), r3(sdv), cache_diff_k[l].reshape(bs, past, dw),
        cache_diff_v[l].reshape(bs, past, dw), diff_g, lams, lam_init).reshape(bs * ts, dw)
    mix_f_s = _fox_sample_attention(
        r3(sfq), r3(sfk), r3(sfv), cache_fox_k[l].reshape(bs, past, fw),
        cache_fox_v[l].reshape(bs, past, fw), ckp_s, ckn_s, fox_g).reshape(bs * ts, fw)

    w_out16 = w_out[l].astype(BF16)
    w_router_pad = _pad_lanes(w_router[l])
    b_router_pad = _pad_lanes(b_router[l].reshape(1, -1), fill=NEG_INF)
    ffn_g = ffn_norm_g[l].reshape(1, -1)
    h_p, hn_p, idx_p, gate_p = _outproj_router(mix_d_p, mix_f_p, w_out16[:dw], w_out16[dw:],
                                               xp, ffn_g, w_router_pad, b_router_pad)
    h_s, hn_s, idx_s, gate_s = _outproj_router(mix_d_s, mix_f_s, w_out16[:dw], w_out16[dw:],
                                               xs, ffn_g, w_router_pad, b_router_pad)

    hn_all = jnp.concatenate([hn_p, hn_s], axis=0)
    idx_all = jnp.concatenate([idx_p, idx_s], axis=0)
    y_rows, dest = _moe(hn_all, idx_all, w_gate[l], w_up[l], w_down[l],
                        b_gate[l], b_up[l], b_down[l])
    y_p = _combine_tokens(dest[:t], y_rows, h_p, gate_p, final_norm_g)
    y_s = _combine_tokens(dest[t:], y_rows, h_s, gate_s, final_norm_g)

    dk32, dv32, fk32, fv32 = cache_p
    sdk32, sdv32, sfk32, sfv32 = cache_s
    return (y_p.reshape(1, t, d), y_s.reshape(bs, ts, d),
            dk32.reshape(1, 1, t, n_dh, 2 * HEAD_DIM), dv32.reshape(1, 1, t, n_dh, 2 * HEAD_DIM),
            fk32.reshape(1, 1, t, n_fh, HEAD_DIM), fv32.reshape(1, 1, t, n_fh, HEAD_DIM),
            logf_p[:, :n_fh].reshape(1, 1, t, n_fh),
            sdk32.reshape(1, bs, ts, n_dh, 2 * HEAD_DIM), sdv32.reshape(1, bs, ts, n_dh, 2 * HEAD_DIM),
            sfk32.reshape(1, bs, ts, n_fh, HEAD_DIM), sfv32.reshape(1, bs, ts, n_fh, HEAD_DIM),
            logf_s[:, :n_fh].reshape(1, bs, ts, n_fh))
```

```python
import functools
import math

import jax
import jax.numpy as jnp
from jax import lax
from jax.experimental import pallas as pl
from jax.experimental.pallas import tpu as pltpu

HEAD_DIM = 128
CHUNK_SHIFT = 6
ROT_DIM = HEAD_DIM // 4
ROPE_THETA = 500000.0
TOP_K = 4
SWIGLU_LIMIT = 7.0
SWIGLU_ALPHA = 1.702
RMS_EPS = 1e-6
NEG_INF = -1e30
LOG2E = 1.4426950408889634
LANES = 128
MOE_ROWS = 256
GATHER_UNROLL = 8
VMEM_LIMIT = 56 * 1024 * 1024

F32 = jnp.float32
BF16 = jnp.bfloat16


def _cparams(sem, vmem=VMEM_LIMIT):
    return pltpu.CompilerParams(dimension_semantics=sem, vmem_limit_bytes=vmem)


def _tile(n, pref):
    if n <= pref:
        return n
    t = max(c for c in range(8, pref + 1, 8) if n % c == 0)
    return t


def _dot_nt(a, b):
    return lax.dot_general(a, b, (((1,), (1,)), ((), ())), preferred_element_type=F32)


def _dot(a, b):
    return jnp.dot(a, b, preferred_element_type=F32)


def _rmsnorm_kernel(x_ref, g_ref, o_ref):
    x = x_ref[...]
    y = x * lax.rsqrt(jnp.mean(x * x, axis=-1, keepdims=True) + RMS_EPS)
    o_ref[...] = (y * g_ref[...]).astype(o_ref.dtype)


def _rmsnorm_bf16(x, g):
    m, d = x.shape
    tm = _tile(m, 512)
    return pl.pallas_call(
        _rmsnorm_kernel,
        out_shape=jax.ShapeDtypeStruct((m, d), BF16),
        grid=(m // tm,),
        in_specs=[pl.BlockSpec((tm, d), lambda i: (i, 0)),
                  pl.BlockSpec((1, d), lambda i: (0, 0))],
        out_specs=pl.BlockSpec((tm, d), lambda i: (i, 0)),
        compiler_params=_cparams(("parallel",)),
        name="rmsnorm_bf16",
    )(x, g.reshape(1, d))


def _proj_kernel(*refs, rope, scale, want_f32, want_bf16, head_major, head_w):
    it = iter(refs)
    x_ref = next(it)
    w_ref = next(it)
    if rope:
        c_ref, sa_ref, sb_ref = next(it), next(it), next(it)
    o32_ref = next(it) if want_f32 else None
    o16_ref = next(it) if want_bf16 else None

    n = w_ref.shape[1]
    y = _dot(x_ref[...], w_ref[...])
    for h in range(n // HEAD_DIM):
        sl = slice(h * HEAD_DIM, (h + 1) * HEAD_DIM)
        yh = y[:, sl]
        if rope:
            yh = (yh * c_ref[...]
                  + pltpu.roll(yh, HEAD_DIM - ROT_DIM // 2, 1) * sa_ref[...]
                  + pltpu.roll(yh, ROT_DIM // 2, 1) * sb_ref[...])
        elif scale != 1.0:
            yh = yh * scale
        if want_f32:
            o32_ref[:, sl] = yh
        if want_bf16:
            if head_major:
                per = head_w // HEAD_DIM
                o16_ref[h // per, :, (h % per) * HEAD_DIM:(h % per + 1) * HEAD_DIM] = yh.astype(BF16)
            else:
                o16_ref[:, sl] = yh.astype(BF16)


def _proj(xn, w, rope_tabs=None, scale=1.0, want_f32=False, want_bf16=True,
          head_major=False, head_w=HEAD_DIM):
    m, d = xn.shape
    n = w.shape[1]
    tm = _tile(m, 512)
    rope = rope_tabs is not None
    in_specs = [pl.BlockSpec((tm, d), lambda i: (i, 0)),
                pl.BlockSpec((d, n), lambda i: (0, 0))]
    args = [xn, w]
    if rope:
        in_specs += [pl.BlockSpec((tm, HEAD_DIM), lambda i: (i, 0))] * 3
        args += list(rope_tabs)
    out_shape, out_specs = [], []
    if want_f32:
        out_shape.append(jax.ShapeDtypeStruct((m, n), F32))
        out_specs.append(pl.BlockSpec((tm, n), lambda i: (i, 0)))
    if want_bf16:
        if head_major:
            nh = n // head_w
            out_shape.append(jax.ShapeDtypeStruct((nh, m, head_w), BF16))
            out_specs.append(pl.BlockSpec((nh, tm, head_w), lambda i: (0, i, 0)))
        else:
            out_shape.append(jax.ShapeDtypeStruct((m, n), BF16))
            out_specs.append(pl.BlockSpec((tm, n), lambda i: (i, 0)))
    kern = functools.partial(_proj_kernel, rope=rope, scale=scale, want_f32=want_f32,
                             want_bf16=want_bf16, head_major=head_major, head_w=head_w)
    return pl.pallas_call(
        kern, out_shape=out_shape, grid=(m // tm,), in_specs=in_specs, out_specs=out_specs,
        compiler_params=_cparams(("parallel",)), name="in_proj",
    )(*args)


def _rope_tables(pos, scale):
    half = ROT_DIM // 2
    t = pos.shape[0]
    inv_freq = jnp.power(ROPE_THETA, -jnp.arange(half, dtype=F32) * (2.0 / ROT_DIM))
    ang = pos.astype(F32)[:, None] * inv_freq[None, :]
    cos, sin = jnp.cos(ang), jnp.sin(ang)
    zeros = lambda w: jnp.zeros((t, w), F32)
    c = jnp.concatenate([cos, cos, jnp.ones((t, HEAD_DIM - ROT_DIM), F32)], axis=1)
    sa = jnp.concatenate([-sin, zeros(HEAD_DIM - half)], axis=1)
    sb = jnp.concatenate([zeros(half), sin, zeros(HEAD_DIM - ROT_DIM)], axis=1)
    return c * scale, sa * scale, sb * scale


def _logf_kernel(x_ref, w_ref, b_ref, o_ref):
    z = _dot(x_ref[...], w_ref[...]) + b_ref[...]
    o_ref[...] = jnp.minimum(z, 0.0) - jnp.log(1.0 + jnp.exp(-jnp.abs(z)))


def _logf(xn, w_pad, b_pad):
    m, d = xn.shape
    tm = _tile(m, 512)
    return pl.pallas_call(
        _logf_kernel,
        out_shape=jax.ShapeDtypeStruct((m, LANES), F32),
        grid=(m // tm,),
        in_specs=[pl.BlockSpec((tm, d), lambda i: (i, 0)),
                  pl.BlockSpec((d, LANES), lambda i: (0, 0)),
                  pl.BlockSpec((1, LANES), lambda i: (0, 0))],
        out_specs=pl.BlockSpec((tm, LANES), lambda i: (i, 0)),
        compiler_params=_cparams(("parallel",)), name="forget_logits",
    )(xn, w_pad, b_pad)


def _cumsum_kernel(x_ref, o_ref, carry_ref):
    @pl.when(pl.program_id(0) == 0)
    def _():
        carry_ref[...] = jnp.zeros_like(carry_ref)

    tb = x_ref.shape[0]
    row = lax.broadcasted_iota(jnp.int32, (tb, tb), 0)
    col = lax.broadcasted_iota(jnp.int32, (tb, tb), 1)
    tri = jnp.where(col <= row, 1.0, 0.0).astype(F32)
    c = jnp.dot(tri, x_ref[...], preferred_element_type=F32,
                precision=lax.Precision.HIGHEST) + carry_ref[...]
    o_ref[...] = c
    carry_ref[...] = c[tb - 1:tb, :]


def _cumsum_rows(x, tb):
    t, c = x.shape
    assert t % tb == 0 and c % LANES == 0
    return pl.pallas_call(
        _cumsum_kernel,
        out_shape=jax.ShapeDtypeStruct((t, c), F32),
        grid=(t // tb,),
        in_specs=[pl.BlockSpec((tb, c), lambda i: (i, 0))],
        out_specs=pl.BlockSpec((tb, c), lambda i: (i, 0)),
        scratch_shapes=[pltpu.VMEM((1, c), F32)],
        compiler_params=_cparams(("arbitrary",)), name="cumsum_rows",
    )(x)


def _lambda_value(lq1, lk1, lq2, lk2, lam_init):
    a = jnp.sum(lq1[...] * lk1[...], axis=-1, keepdims=True)
    b = jnp.sum(lq2[...] * lk2[...], axis=-1, keepdims=True)
    return jnp.exp(a) - jnp.exp(b) + lam_init


def _head_rmsnorm(o, g):
    return o * lax.rsqrt(jnp.mean(o * o, axis=-1, keepdims=True) + RMS_EPS) * g


def _online_update(s, m, l, acc_ref, v):
    m_new = jnp.maximum(m, jnp.max(s, axis=-1, keepdims=True))
    alpha = jnp.exp2(m - m_new)
    p = jnp.exp2(s - m_new)
    l_new = alpha * l + jnp.sum(p, axis=-1, keepdims=True)
    acc_ref[...] = alpha * acc_ref[...] + _dot(p.astype(BF16), v)
    return m_new, l_new


def _diff_prompt_kernel(q_ref, k_ref, v_ref, g_ref, lq1, lk1, lq2, lk2, o_ref,
                        acc1, acc2, *, tq, lam_init):
    i = pl.program_id(1)
    q = q_ref[0]
    q1, q2 = q[:, :HEAD_DIM], q[:, HEAD_DIM:]
    acc1[...] = jnp.zeros_like(acc1)
    acc2[...] = jnp.zeros_like(acc2)

    def tile(j, state, masked):
        m1, l1, m2, l2 = state
        start = pl.multiple_of(j * tq, tq)
        k = k_ref[0, pl.ds(start, tq), :]
        v = v_ref[0, pl.ds(start, tq), :]
        s1 = _dot_nt(q1, k[:, :HEAD_DIM])
        s2 = _dot_nt(q2, k[:, HEAD_DIM:])
        if masked:
            row = lax.broadcasted_iota(jnp.int32, (tq, tq), 0)
            col = lax.broadcasted_iota(jnp.int32, (tq, tq), 1)
            ok = (col >> CHUNK_SHIFT) <= (row >> CHUNK_SHIFT)
            s1 = jnp.where(ok, s1, NEG_INF)
            s2 = jnp.where(ok, s2, NEG_INF)
        m1, l1 = _online_update(s1, m1, l1, acc1, v)
        m2, l2 = _online_update(s2, m2, l2, acc2, v)
        return m1, l1, m2, l2

    neg = jnp.full((tq, 1), NEG_INF, F32)
    zero = jnp.zeros((tq, 1), F32)
    state = lax.fori_loop(0, i, lambda j, s: tile(j, s, False), (neg, zero, neg, zero))
    _, l1, _, l2 = tile(i, state, True)

    lam = _lambda_value(lq1, lk1, lq2, lk2, lam_init)
    o = acc1[...] / l1 - lam * (acc2[...] / l2)
    o_ref[...] = (_head_rmsnorm(o, g_ref[...]) * (1.0 - lam_init)).astype(o_ref.dtype)


def _diff_prompt_attention(q, k, v, g, lams, lam_init):
    nh, t, w = q.shape
    tq = _tile(t, 512)
    assert tq % (1 << CHUNK_SHIFT) == 0
    vec = pl.BlockSpec((1, HEAD_DIM), lambda h, i: (0, 0))
    kern = functools.partial(_diff_prompt_kernel, tq=tq, lam_init=lam_init)
    return pl.pallas_call(
        kern,
        out_shape=jax.ShapeDtypeStruct((t, nh * w), BF16),
        grid=(nh, t // tq),
        in_specs=[pl.BlockSpec((1, tq, w), lambda h, i: (h, i, 0)),
                  pl.BlockSpec((1, t, w), lambda h, i: (h, 0, 0)),
                  pl.BlockSpec((1, t, w), lambda h, i: (h, 0, 0)),
                  pl.BlockSpec((1, w), lambda h, i: (0, 0)),
                  vec, vec, vec, vec],
        out_specs=pl.BlockSpec((tq, w), lambda h, i: (i, h)),
        scratch_shapes=[pltpu.VMEM((tq, w), F32), pltpu.VMEM((tq, w), F32)],
        compiler_params=_cparams(("parallel", "arbitrary")), name="diff_prompt_attention",
    )(q, k, v, g, *lams)


def _fox_prompt_kernel(q_ref, k_ref, v_ref, ck_ref, g_ref, o_ref, acc, *, tq):
    i = pl.program_id(1)
    q = q_ref[0]
    acc[...] = jnp.zeros_like(acc)

    def tile(j, state, masked):
        m, l = state
        start = pl.multiple_of(j * tq, tq)
        k = k_ref[0, pl.ds(start, tq), :]
        v = v_ref[0, pl.ds(start, tq), :]
        s = _dot_nt(q, k) - ck_ref[0, j] * LOG2E
        if masked:
            row = lax.broadcasted_iota(jnp.int32, (tq, tq), 0)
            col = lax.broadcasted_iota(jnp.int32, (tq, tq), 1)
            s = jnp.where(col <= row, s, NEG_INF)
        return _online_update(s, m, l, acc, v)

    state = (jnp.full((tq, 1), NEG_INF, F32), jnp.zeros((tq, 1), F32))
    state = lax.fori_loop(0, i, lambda j, s: tile(j, s, False), state)
    _, l = tile(i, state, True)
    o_ref[...] = _head_rmsnorm(acc[...] / l, g_ref[...]).astype(o_ref.dtype)


def _fox_prompt_attention(q, k, v, ck, g):
    nh, t, w = q.shape
    tq = ck.shape[-1]
    kern = functools.partial(_fox_prompt_kernel, tq=tq)
    return pl.pallas_call(
        kern,
        out_shape=jax.ShapeDtypeStruct((t, nh * w), BF16),
        grid=(nh, t // tq),
        in_specs=[pl.BlockSpec((1, tq, w), lambda h, i: (h, i, 0)),
                  pl.BlockSpec((1, t, w), lambda h, i: (h, 0, 0)),
                  pl.BlockSpec((1, t, w), lambda h, i: (h, 0, 0)),
                  pl.BlockSpec((1, t // tq, 1, tq), lambda h, i: (h, 0, 0, 0)),
                  pl.BlockSpec((1, w), lambda h, i: (0, 0))],
        out_specs=pl.BlockSpec((tq, w), lambda h, i: (i, h)),
        scratch_shapes=[pltpu.VMEM((tq, w), F32)],
        compiler_params=_cparams(("parallel", "arbitrary")), name="fox_prompt_attention",
    )(q, k, v, ck, g)


def _two_segment_softmax(sp, sn):
    m = jnp.maximum(jnp.max(sp, axis=-1, keepdims=True), jnp.max(sn, axis=-1, keepdims=True))
    pp = jnp.exp2(sp - m)
    pn = jnp.exp2(sn - m)
    l = jnp.sum(pp, axis=-1, keepdims=True) + jnp.sum(pn, axis=-1, keepdims=True)
    return pp.astype(BF16), pn.astype(BF16), l


def _diff_sample_kernel(q_ref, kn_ref, vn_ref, kp_ref, vp_ref, g_ref, lq1, lk1, lq2, lk2,
                        o_ref, *, past_len, lam_init, n_heads):
    t = q_ref.shape[1]
    w = 2 * HEAD_DIM
    row = lax.broadcasted_iota(jnp.int32, (t, t), 0) + past_len
    col = lax.broadcasted_iota(jnp.int32, (t, t), 1) + past_len
    ok = (col >> CHUNK_SHIFT) <= (row >> CHUNK_SHIFT)
    lam = _lambda_value(lq1, lk1, lq2, lk2, lam_init)
    for h in range(n_heads):
        sl = slice(h * w, (h + 1) * w)
        q = q_ref[0, :, sl]
        kn = kn_ref[0, :, sl]
        vn = vn_ref[0, :, sl]
        kp = kp_ref[0, :, sl].astype(BF16)
        vp = vp_ref[0, :, sl].astype(BF16)
        outs = []
        for half in range(2):
            hs = slice(half * HEAD_DIM, (half + 1) * HEAD_DIM)
            sp = _dot_nt(q[:, hs], kp[:, hs])
            sn = jnp.where(ok, _dot_nt(q[:, hs], kn[:, hs]), NEG_INF)
            pp, pn, l = _two_segment_softmax(sp, sn)
            outs.append((_dot(pp, vp) + _dot(pn, vn)) / l)
        o = outs[0] - lam * outs[1]
        o_ref[0, :, sl] = (_head_rmsnorm(o, g_ref[...]) * (1.0 - lam_init)).astype(o_ref.dtype)


def _diff_sample_attention(q, kn, vn, kp, vp, g, lams, lam_init):
    b, t, n = q.shape
    past = kp.shape[1]
    new = pl.BlockSpec((1, t, n), lambda i: (i, 0, 0))
    old = pl.BlockSpec((1, past, n), lambda i: (i, 0, 0))
    vec = pl.BlockSpec((1, HEAD_DIM), lambda i: (0, 0))
    kern = functools.partial(_diff_sample_kernel, past_len=past, lam_init=lam_init,
                             n_heads=n // (2 * HEAD_DIM))
    return pl.pallas_call(
        kern,
        out_shape=jax.ShapeDtypeStruct((b, t, n), BF16),
        grid=(b,),
        in_specs=[new, new, new, old, old,
                  pl.BlockSpec((1, 2 * HEAD_DIM), lambda i: (0, 0)), vec, vec, vec, vec],
        out_specs=new,
        compiler_params=_cparams(("parallel",)), name="diff_sample_attention",
    )(q, kn, vn, kp, vp, g, *lams)


def _fox_sample_kernel(q_ref, kn_ref, vn_ref, kp_ref, vp_ref, ckp_ref, ckn_ref, g_ref, o_ref,
                       *, n_heads):
    t = q_ref.shape[1]
    row = lax.broadcasted_iota(jnp.int32, (t, t), 0)
    col = lax.broadcasted_iota(jnp.int32, (t, t), 1)
    ok = col <= row
    for h in range(n_heads):
        sl = slice(h * HEAD_DIM, (h + 1) * HEAD_DIM)
        q = q_ref[0, :, sl]
        kp = kp_ref[0, :, sl].astype(BF16)
        vp = vp_ref[0, :, sl].astype(BF16)
        sp = _dot_nt(q, kp) - ckp_ref[0, h] * LOG2E
        sn = jnp.where(ok, _dot_nt(q, kn_ref[0, :, sl]) - ckn_ref[0, h] * LOG2E, NEG_INF)
        pp, pn, l = _two_segment_softmax(sp, sn)
        o = (_dot(pp, vp) + _dot(pn, vn_ref[0, :, sl])) / l
        o_ref[0, :, sl] = _head_rmsnorm(o, g_ref[...]).astype(o_ref.dtype)


def _fox_sample_attention(q, kn, vn, kp, vp, ckp, ckn, g):
    b, t, n = q.shape
    past = kp.shape[1]
    nh = n // HEAD_DIM
    new = pl.BlockSpec((1, t, n), lambda i: (i, 0, 0))
    old = pl.BlockSpec((1, past, n), lambda i: (i, 0, 0))
    kern = functools.partial(_fox_sample_kernel, n_heads=nh)
    return pl.pallas_call(
        kern,
        out_shape=jax.ShapeDtypeStruct((b, t, n), BF16),
        grid=(b,),
        in_specs=[new, new, new, old, old,
                  pl.BlockSpec((1, nh, 1, past), lambda i: (i, 0, 0, 0)),
                  pl.BlockSpec((1, nh, 1, t), lambda i: (i, 0, 0, 0)),
                  pl.BlockSpec((1, HEAD_DIM), lambda i: (0, 0))],
        out_specs=new,
        compiler_params=_cparams(("parallel",)), name="fox_sample_attention",
    )(q, kn, vn, kp, vp, ckp, ckn, g)


def _outproj_router_kernel(md_ref, mf_ref, wd_ref, wf_ref, x_ref, g_ref, wr_ref, br_ref,
                           h_ref, hn_ref, idx_ref, gate_ref):
    h = x_ref[...] + _dot(md_ref[...], wd_ref[...]) + _dot(mf_ref[...], wf_ref[...])
    h_ref[...] = h
    hn = h * lax.rsqrt(jnp.mean(h * h, axis=-1, keepdims=True) + RMS_EPS) * g_ref[...]
    hn_ref[...] = hn
    logits = jnp.dot(hn, wr_ref[...], preferred_element_type=F32,
                     precision=lax.Precision.HIGHEST) + br_ref[...]
    lane = lax.broadcasted_iota(jnp.int32, logits.shape, 1)
    work = logits
    idx_out = jnp.zeros(logits.shape, jnp.int32)
    val_out = jnp.zeros(logits.shape, F32)
    top = None
    denom = None
    for k in range(TOP_K):
        mx = jnp.max(work, axis=-1, keepdims=True)
        first = jnp.min(jnp.where(work == mx, lane, LANES), axis=-1, keepdims=True)
        work = jnp.where(lane == first, -jnp.inf, work)
        if k == 0:
            top = mx
        e = jnp.exp(mx - top)
        denom = e if k == 0 else denom + e
        idx_out = jnp.where(lane == k, first, idx_out)
        val_out = jnp.where(lane == k, e, val_out)
    idx_ref[...] = idx_out
    gate_ref[...] = val_out / denom


def _outproj_router(mix_d, mix_f, w_d, w_f, x, g, w_router_pad, b_router_pad):
    m, d = x.shape
    kd, kf = mix_d.shape[1], mix_f.shape[1]
    tm = _tile(m, 256)
    row = lambda w: pl.BlockSpec((tm, w), lambda i: (i, 0))
    full = lambda r, c: pl.BlockSpec((r, c), lambda i: (0, 0))
    return pl.pallas_call(
        _outproj_router_kernel,
        out_shape=[jax.ShapeDtypeStruct((m, d), F32), jax.ShapeDtypeStruct((m, d), F32),
                   jax.ShapeDtypeStruct((m, LANES), jnp.int32),
                   jax.ShapeDtypeStruct((m, LANES), F32)],
        grid=(m // tm,),
        in_specs=[row(kd), row(kf), full(kd, d), full(kf, d), row(d), full(1, d),
                  full(d, LANES), full(1, LANES)],
        out_specs=[row(d), row(d), row(LANES), row(LANES)],
        compiler_params=_cparams(("parallel",)), name="outproj_router",
    )(mix_d, mix_f, w_d, w_f, x, g, w_router_pad, b_router_pad)


def _rank_kernel(idx_ref, rank_ref, count_ref, carry_ref):
    @pl.when(pl.program_id(0) == 0)
    def _():
        carry_ref[...] = jnp.zeros_like(carry_ref)

    tm = idx_ref.shape[0]
    idx = idx_ref[...]
    lane = lax.broadcasted_iota(jnp.int32, (tm, LANES), 1)
    picks = [lane == idx[:, k:k + 1] for k in range(TOP_K)]
    onehot = jnp.zeros((tm, LANES), F32)
    for p in picks:
        onehot = onehot + jnp.where(p, 1.0, 0.0)
    row = lax.broadcasted_iota(jnp.int32, (tm, tm), 0)
    col = lax.broadcasted_iota(jnp.int32, (tm, tm), 1)
    strict = jnp.where(col < row, 1.0, 0.0).astype(BF16)
    before = _dot(strict, onehot.astype(BF16)) + carry_ref[...]
    rank = jnp.zeros((tm, LANES), F32)
    for k, p in enumerate(picks):
        r = jnp.sum(jnp.where(p, before, 0.0), axis=-1, keepdims=True)
        rank = jnp.where(lane == k, r, rank)
    rank_ref[...] = rank.astype(jnp.int32)
    total = before[tm - 1:tm, :] + onehot[tm - 1:tm, :]
    carry_ref[...] = total
    count_ref[...] = total.astype(jnp.int32)


def _expert_ranks(idx):
    n_tok = idx.shape[0]
    tm = _tile(n_tok, 512)
    return pl.pallas_call(
        _rank_kernel,
        out_shape=[jax.ShapeDtypeStruct((n_tok, LANES), jnp.int32),
                   jax.ShapeDtypeStruct((1, LANES), jnp.int32)],
        grid=(n_tok // tm,),
        in_specs=[pl.BlockSpec((tm, LANES), lambda i: (i, 0))],
        out_specs=[pl.BlockSpec((tm, LANES), lambda i: (i, 0)),
                   pl.BlockSpec((1, LANES), lambda i: (0, 0))],
        scratch_shapes=[pltpu.VMEM((1, LANES), F32)],
        compiler_params=_cparams(("arbitrary",)), name="expert_ranks",
    )(idx)


def _row_copy(src_ref, dst_ref, sem, src_row, dst_row):
    return pltpu.make_async_copy(src_ref.at[pl.ds(src_row, 1)], dst_ref.at[pl.ds(dst_row, 1)], sem)


def _gather_rows(idx_ref, src_ref, dst_ref, sem, n_rows):
    assert n_rows % GATHER_UNROLL == 0

    def start(g, carry):
        base = pl.multiple_of(g * GATHER_UNROLL, GATHER_UNROLL)
        for u in range(GATHER_UNROLL):
            _row_copy(src_ref, dst_ref, sem, idx_ref[0, 0, base + u], base + u).start()
        return carry

    def wait(g, carry):
        base = pl.multiple_of(g * GATHER_UNROLL, GATHER_UNROLL)
        for u in range(GATHER_UNROLL):
            _row_copy(src_ref, dst_ref, sem, 0, base + u).wait()
        return carry

    lax.fori_loop(0, n_rows // GATHER_UNROLL, start, 0)
    lax.fori_loop(0, n_rows // GATHER_UNROLL, wait, 0)


def _dispatch_kernel(tok_ref, hn_ref, o_ref, buf, sem):
    _gather_rows(tok_ref, hn_ref, buf, sem, buf.shape[0])
    o_ref[...] = buf[...].astype(o_ref.dtype)


def _dispatch(row_tok, hn):
    n_blocks, _, rows = row_tok.shape
    d = hn.shape[1]
    return pl.pallas_call(
        _dispatch_kernel,
        out_shape=jax.ShapeDtypeStruct((n_blocks * rows, d), BF16),
        grid=(n_blocks,),
        in_specs=[pl.BlockSpec((1, 1, rows), lambda b: (b, 0, 0), memory_space=pltpu.SMEM),
                  pl.BlockSpec(memory_space=pl.ANY)],
        out_specs=pl.BlockSpec((rows, d), lambda b: (b, 0)),
        scratch_shapes=[pltpu.VMEM((rows, d), hn.dtype), pltpu.SemaphoreType.DMA],
        compiler_params=_cparams(("arbitrary",)), name="moe_dispatch",
    )(row_tok, hn)


def _combine_kernel(dest_ref, y_ref, h_ref, gate_ref, g_ref, o_ref, buf, sem, *, tm):
    _gather_rows(dest_ref, y_ref, buf, sem, TOP_K * tm)
    gates = gate_ref[...]
    out = h_ref[...]
    for k in range(TOP_K):
        out = out + buf[k * tm:(k + 1) * tm, :] * gates[:, k:k + 1]
    out = out * lax.rsqrt(jnp.mean(out * out, axis=-1, keepdims=True) + RMS_EPS)
    o_ref[...] = out * g_ref[...]


def _combine(dest, y_rows, h, gates, g):
    n_tiles, _, per = dest.shape
    tm = per // TOP_K
    m, d = h.shape
    kern = functools.partial(_combine_kernel, tm=tm)
    return pl.pallas_call(
        kern,
        out_shape=jax.ShapeDtypeStruct((m, d), F32),
        grid=(n_tiles,),
        in_specs=[pl.BlockSpec((1, 1, per), lambda i: (i, 0, 0), memory_space=pltpu.SMEM),
                  pl.BlockSpec(memory_space=pl.ANY),
                  pl.BlockSpec((tm, d), lambda i: (i, 0)),
                  pl.BlockSpec((tm, LANES), lambda i: (i, 0)),
                  pl.BlockSpec((1, d), lambda i: (0, 0))],
        out_specs=pl.BlockSpec((tm, d), lambda i: (i, 0)),
        scratch_shapes=[pltpu.VMEM((per, d), y_rows.dtype), pltpu.SemaphoreType.DMA],
        compiler_params=_cparams(("arbitrary",)), name="moe_combine",
    )(dest, y_rows, h, gates, g)


def _expert_changed(be_ref, rb):
    prev = be_ref[jnp.maximum(rb - 1, 0)]
    return jnp.logical_or(rb == 0, be_ref[rb] != prev)


def _moe_up_kernel(be_ref, used_ref, x_ref, wg_ref, wu_ref, bg_ref, bu_ref, o_ref, wg16, wu16):
    rb = pl.program_id(1)

    @pl.when(_expert_changed(be_ref, rb))
    def _():
        wg16[...] = wg_ref[0].astype(BF16)
        wu16[...] = wu_ref[0].astype(BF16)

    @pl.when(rb < used_ref[0])
    def _():
        x = x_ref[...]
        gate = jnp.minimum(_dot(x, wg16[...]) + bg_ref[0], SWIGLU_LIMIT)
        up = jnp.clip(_dot(x, wu16[...]) + bu_ref[0], -SWIGLU_LIMIT, SWIGLU_LIMIT)
        act = (up + 1.0) * (gate * jax.nn.sigmoid(SWIGLU_ALPHA * gate))
        o_ref[...] = act.astype(o_ref.dtype)

    @pl.when(rb >= used_ref[0])
    def _():
        o_ref[...] = jnp.zeros_like(o_ref)


def _moe_up(block_expert, n_used, xs, w_gate, w_up, b_gate, b_up, tn):
    n_rows, d = xs.shape
    n_exp, _, d_ff = w_gate.shape
    rows = MOE_ROWS
    n_blocks = n_rows // rows
    wspec = pl.BlockSpec((1, d, tn), lambda n, rb, be, used: (be[rb], 0, n))
    bspec = pl.BlockSpec((1, 1, tn), lambda n, rb, be, used: (be[rb], 0, n))
    return pl.pallas_call(
        _moe_up_kernel,
        out_shape=jax.ShapeDtypeStruct((n_rows, d_ff), BF16),
        grid_spec=pltpu.PrefetchScalarGridSpec(
            num_scalar_prefetch=2,
            grid=(d_ff // tn, n_blocks),
            in_specs=[pl.BlockSpec((rows, d), lambda n, rb, be, used: (rb, 0)),
                      wspec, wspec, bspec, bspec],
            out_specs=pl.BlockSpec((rows, tn), lambda n, rb, be, used: (rb, n)),
            scratch_shapes=[pltpu.VMEM((d, tn), BF16), pltpu.VMEM((d, tn), BF16)]),
        compiler_params=_cparams(("arbitrary", "arbitrary")), name="moe_gate_up",
    )(block_expert, n_used, xs, w_gate, w_up, b_gate.reshape(n_exp, 1, d_ff),
      b_up.reshape(n_exp, 1, d_ff))


def _moe_down_kernel(be_ref, used_ref, a_ref, w_ref, b_ref, o_ref, w16):
    rb = pl.program_id(1)

    @pl.when(_expert_changed(be_ref, rb))
    def _():
        w16[...] = w_ref[0].astype(BF16)

    @pl.when(rb < used_ref[0])
    def _():
        o_ref[...] = _dot(a_ref[...], w16[...]) + b_ref[0]

    @pl.when(rb >= used_ref[0])
    def _():
        o_ref[...] = jnp.zeros_like(o_ref)


def _moe_down(block_expert, n_used, act, w_down, b_down, tn):
    n_rows, d_ff = act.shape
    n_exp, _, d = w_down.shape
    rows = MOE_ROWS
    return pl.pallas_call(
        _moe_down_kernel,
        out_shape=jax.ShapeDtypeStruct((n_rows, d), F32),
        grid_spec=pltpu.PrefetchScalarGridSpec(
            num_scalar_prefetch=2,
            grid=(d // tn, n_rows // rows),
            in_specs=[pl.BlockSpec((rows, d_ff), lambda n, rb, be, used: (rb, 0)),
                      pl.BlockSpec((1, d_ff, tn), lambda n, rb, be, used: (be[rb], 0, n)),
                      pl.BlockSpec((1, 1, tn), lambda n, rb, be, used: (be[rb], 0, n))],
            out_specs=pl.BlockSpec((rows, tn), lambda n, rb, be, used: (rb, n)),
            scratch_shapes=[pltpu.VMEM((d_ff, tn), BF16)]),
        compiler_params=_cparams(("arbitrary", "arbitrary")), name="moe_down",
    )(block_expert, n_used, act, w_down, b_down.reshape(n_exp, 1, d))


def _pad_lanes(a, fill=0.0):
    return jnp.pad(a, ((0, 0), (0, LANES - a.shape[1])), constant_values=fill)


def _mixer_projections(x2d, pos, norm_g, w_in, b_forget, head_major):
    d = x2d.shape[1]
    dw = (w_in.shape[1] - 8) // 6
    q_scale = HEAD_DIM ** -0.5 * LOG2E
    xn = _rmsnorm_bf16(x2d, norm_g)
    w16 = w_in.astype(BF16)
    wcol = lambda j: w16[:, j * dw:(j + 1) * dw]
    tabs_q = _rope_tables(pos, q_scale)
    tabs_k = _rope_tables(pos, 1.0)
    hm = dict(head_major=head_major)
    (dq,) = _proj(xn, wcol(0), rope_tabs=tabs_q, head_w=2 * HEAD_DIM, **hm)
    dk32, dk = _proj(xn, wcol(1), rope_tabs=tabs_k, want_f32=True, head_w=2 * HEAD_DIM, **hm)
    dv32, dv = _proj(xn, wcol(2), want_f32=True, head_w=2 * HEAD_DIM, **hm)
    (fq,) = _proj(xn, wcol(3), scale=q_scale, **hm)
    fk32, fk = _proj(xn, wcol(4), want_f32=True, **hm)
    fv32, fv = _proj(xn, wcol(5), want_f32=True, **hm)
    logf = _logf(xn, _pad_lanes(w16[:, 6 * dw:]), _pad_lanes(b_forget.reshape(1, -1)))
    return (dq, dk, dv, fq, fk, fv), (dk32, dv32, fk32, fv32), logf


def _moe(hn_all, idx_all, w_gate, w_up, w_down, b_gate, b_up, b_down):
    n_tok, d = hn_all.shape
    n_exp = w_gate.shape[0]
    rows = MOE_ROWS
    n_blocks = -(-(n_tok * TOP_K) // rows) + n_exp
    n_rows = n_blocks * rows

    rank, counts = _expert_ranks(idx_all)
    experts = idx_all[:, :TOP_K]
    counts = counts[0, :n_exp]
    padded = (counts + rows - 1) // rows * rows
    padded_end = jnp.cumsum(padded)
    padded_start = padded_end - padded
    dest = padded_start[experts] + rank[:, :TOP_K]
    token_of = jnp.broadcast_to(jnp.arange(n_tok, dtype=jnp.int32)[:, None], dest.shape)
    row_tok = jnp.zeros((n_rows,), jnp.int32).at[dest.reshape(-1)].set(token_of.reshape(-1))
    block_start = jnp.arange(n_blocks, dtype=jnp.int32) * rows
    block_expert = jnp.minimum(
        jnp.sum((padded_end[None, :] <= block_start[:, None]).astype(jnp.int32), axis=1), n_exp - 1)
    n_used = (padded_end[-1] // rows).astype(jnp.int32).reshape(1)

    xs = _dispatch(row_tok.reshape(n_blocks, 1, rows), hn_all)
    act = _moe_up(block_expert, n_used, xs, w_gate, w_up, b_gate, b_up, tn=512)
    y_rows = _moe_down(block_expert, n_used, act, w_down, b_down, tn=min(1024, w_down.shape[2]))
    return y_rows, dest


def _combine_tokens(dest, y_rows, h, gates, g):
    m = h.shape[0]
    tm = _tile(m, 256)
    dest_t = dest.reshape(m // tm, tm, TOP_K).transpose(0, 2, 1).reshape(m // tm, 1, TOP_K * tm)
    return _combine(dest_t, y_rows, h, gates, g.reshape(1, -1))


def kernel(x_prompt, x_sample, cache_diff_k, cache_diff_v, cache_fox_k, cache_fox_v, cache_fox_logf,
           attn_norm_g, w_in, b_forget, lambda_q1, lambda_k1, lambda_q2, lambda_k2,
           diff_out_norm_g, fox_out_norm_g, w_out, ffn_norm_g, w_router, b_router,
           w_gate, b_gate, w_up, b_up, w_down, b_down, final_norm_g):
    depth = w_in.shape[0]
    assert depth == 1, "single-layer stack"
    lam_init = 0.8 - 0.6 * math.exp(-0.3 * 0)
    drop = lambda a: a.reshape(a.shape[1:])
    (cache_diff_k, cache_diff_v, cache_fox_k, cache_fox_v, cache_fox_logf, attn_norm_g, w_in,
     b_forget, lambda_q1, lambda_k1, lambda_q2, lambda_k2, diff_out_norm_g, fox_out_norm_g, w_out,
     ffn_norm_g, w_router, b_router, w_gate, b_gate, w_up, b_up, w_down, b_down) = map(drop, (
         cache_diff_k, cache_diff_v, cache_fox_k, cache_fox_v, cache_fox_logf, attn_norm_g, w_in,
         b_forget, lambda_q1, lambda_k1, lambda_q2, lambda_k2, diff_out_norm_g, fox_out_norm_g,
         w_out, ffn_norm_g, w_router, b_router, w_gate, b_gate, w_up, b_up, w_down, b_down))
    bp, t, d = x_prompt.shape
    assert bp == 1
    bs, ts, _ = x_sample.shape
    past = cache_diff_k.shape[1]
    n_dh = cache_diff_k.shape[2]
    n_fh = cache_fox_k.shape[2]
    dw = n_dh * 2 * HEAD_DIM
    fw = n_fh * HEAD_DIM
    lams = [v.reshape(1, HEAD_DIM) for v in (lambda_q1, lambda_k1, lambda_q2, lambda_k2)]
    diff_g = diff_out_norm_g.reshape(1, -1)
    fox_g = fox_out_norm_g.reshape(1, -1)

    xp = x_prompt.reshape(t, d)
    pos_p = jnp.arange(t, dtype=jnp.int32)
    (dq, dk, dv, fq, fk, fv), cache_p, logf_p = _mixer_projections(
        xp, pos_p, attn_norm_g, w_in, b_forget, head_major=True)
    tq = _tile(t, 512)
    c_p = _cumsum_rows(logf_p, _tile(t, 512))
    ck_p = c_p[:, :n_fh].T.reshape(n_fh, t // tq, 1, tq)
    mix_d_p = _diff_prompt_attention(dq, dk, dv, diff_g, lams, lam_init)
    mix_f_p = _fox_prompt_attention(fq, fk, fv, ck_p, fox_g)

    xs = x_sample.reshape(bs * ts, d)
    pos_s = past + jnp.arange(ts, dtype=jnp.int32)
    (sdq, sdk, sdv, sfq, sfk, sfv), cache_s, logf_s = _mixer_projections(
        xs, jnp.tile(pos_s, bs), attn_norm_g, w_in, b_forget, head_major=False)
    logf_all = jnp.concatenate([cache_fox_logf.astype(F32),
                                logf_s[:, :n_fh].reshape(bs, ts, n_fh)], axis=1)
    tt = past + ts
    lf_cols = logf_all.transpose(1, 0, 2).reshape(tt, bs * n_fh)
    pad_c = (-lf_cols.shape[1]) % LANES
    lf_cols = jnp.pad(lf_cols, ((0, 0), (0, pad_c)))
    tb = max(tb_ for tb_ in range(8, 513, 8) if tt % tb_ == 0)
    c_s = _cumsum_rows(lf_cols, tb)[:, :bs * n_fh].reshape(tt, bs, n_fh).transpose(1, 2, 0)
    ckp_s = c_s[:, :, None, :past]
    ckn_s = c_s[:, :, None, past:]
    r3 = lambda a: a.reshape(bs, ts, -1)
    mix_d_s = _diff_sample_attention(
        r3(sdq), r3(sdk), r3(sdv), cache_diff_k.reshape(bs, past, dw),
        cache_diff_v.reshape(bs, past, dw), diff_g, lams, lam_init).reshape(bs * ts, dw)
    mix_f_s = _fox_sample_attention(
        r3(sfq), r3(sfk), r3(sfv), cache_fox_k.reshape(bs, past, fw),
        cache_fox_v.reshape(bs, past, fw), ckp_s, ckn_s, fox_g).reshape(bs * ts, fw)

    w_out16 = w_out.astype(BF16)
    w_router_pad = _pad_lanes(w_router)
    b_router_pad = _pad_lanes(b_router.reshape(1, -1), fill=NEG_INF)
    ffn_g = ffn_norm_g.reshape(1, -1)
    h_p, hn_p, idx_p, gate_p = _outproj_router(mix_d_p, mix_f_p, w_out16[:dw], w_out16[dw:],
                                               xp, ffn_g, w_router_pad, b_router_pad)
    h_s, hn_s, idx_s, gate_s = _outproj_router(mix_d_s, mix_f_s, w_out16[:dw], w_out16[dw:],
                                               xs, ffn_g, w_router_pad, b_router_pad)

    hn_all = jnp.concatenate([hn_p, hn_s], axis=0)
    idx_all = jnp.concatenate([idx_p, idx_s], axis=0)
    y_rows, dest = _moe(hn_all, idx_all, w_gate, w_up, w_down, b_gate, b_up, b_down)
    y_p = _combine_tokens(dest[:t], y_rows, h_p, gate_p, final_norm_g)
    y_s = _combine_tokens(dest[t:], y_rows, h_s, gate_s, final_norm_g)

    dk32, dv32, fk32, fv32 = cache_p
    sdk32, sdv32, sfk32, sfv32 = cache_s
    return (y_p.reshape(1, t, d), y_s.reshape(bs, ts, d),
            dk32.reshape(1, 1, t, n_dh, 2 * HEAD_DIM), dv32.reshape(1, 1, t, n_dh, 2 * HEAD_DIM),
            fk32.reshape(1, 1, t, n_fh, HEAD_DIM), fv32.reshape(1, 1, t, n_fh, HEAD_DIM),
            logf_p[:, :n_fh].reshape(1, 1, t, n_fh),
            sdk32.reshape(1, bs, ts, n_dh, 2 * HEAD_DIM), sdv32.reshape(1, bs, ts, n_dh, 2 * HEAD_DIM),
            sfk32.reshape(1, bs, ts, n_fh, HEAD_DIM), sfv32.reshape(1, bs, ts, n_fh, HEAD_DIM),
            logf_s[:, :n_fh].reshape(1, bs, ts, n_fh))
```

```python
import functools
import math

import jax
import jax.numpy as jnp
from jax import lax
from jax.experimental import pallas as pl
from jax.experimental.pallas import tpu as pltpu

HEAD_DIM = 128
CHUNK_SHIFT = 6
ROT_DIM = HEAD_DIM // 4
ROPE_THETA = 500000.0
TOP_K = 4
SWIGLU_LIMIT = 7.0
SWIGLU_ALPHA = 1.702
RMS_EPS = 1e-6
NEG_INF = -1e30
LOG2E = 1.4426950408889634
LANES = 128
MOE_ROWS = 256
GATHER_UNROLL = 8
DISPATCH_ROWS = 1024
VMEM_LIMIT = 56 * 1024 * 1024

F32 = jnp.float32
BF16 = jnp.bfloat16


def _cparams(sem, vmem=VMEM_LIMIT):
    return pltpu.CompilerParams(dimension_semantics=sem, vmem_limit_bytes=vmem)


def _tile(n, pref):
    if n <= pref:
        return n
    t = max(c for c in range(8, pref + 1, 8) if n % c == 0)
    return t


def _dot_nt(a, b):
    return lax.dot_general(a, b, (((1,), (1,)), ((), ())), preferred_element_type=F32)


def _dot(a, b):
    return jnp.dot(a, b, preferred_element_type=F32)


def _rmsnorm_kernel(x_ref, g_ref, o_ref):
    x = x_ref[...]
    y = x * lax.rsqrt(jnp.mean(x * x, axis=-1, keepdims=True) + RMS_EPS)
    o_ref[...] = (y * g_ref[...]).astype(o_ref.dtype)


def _rmsnorm_bf16(x, g):
    m, d = x.shape
    tm = _tile(m, 512)
    return pl.pallas_call(
        _rmsnorm_kernel,
        out_shape=jax.ShapeDtypeStruct((m, d), BF16),
        grid=(m // tm,),
        in_specs=[pl.BlockSpec((tm, d), lambda i: (i, 0)),
                  pl.BlockSpec((1, d), lambda i: (0, 0))],
        out_specs=pl.BlockSpec((tm, d), lambda i: (i, 0)),
        compiler_params=_cparams(("parallel",)),
        name="rmsnorm_bf16",
    )(x, g.reshape(1, d))


def _proj_kernel(*refs, rope, scale, want_f32, want_bf16, head_major, head_w):
    it = iter(refs)
    x_ref = next(it)
    w_ref = next(it)
    if rope:
        c_ref, sa_ref, sb_ref = next(it), next(it), next(it)
    o32_ref = next(it) if want_f32 else None
    o16_ref = next(it) if want_bf16 else None

    n = w_ref.shape[1]
    y = _dot(x_ref[...], w_ref[...])
    for h in range(n // HEAD_DIM):
        sl = slice(h * HEAD_DIM, (h + 1) * HEAD_DIM)
        yh = y[:, sl]
        if rope:
            yh = (yh * c_ref[...]
                  + pltpu.roll(yh, HEAD_DIM - ROT_DIM // 2, 1) * sa_ref[...]
                  + pltpu.roll(yh, ROT_DIM // 2, 1) * sb_ref[...])
        elif scale != 1.0:
            yh = yh * scale
        per = head_w // HEAD_DIM
        if want_f32:
            o32_ref[:, h // per, (h % per) * HEAD_DIM:(h % per + 1) * HEAD_DIM] = yh
        if want_bf16:
            if head_major:
                o16_ref[h // per, :, (h % per) * HEAD_DIM:(h % per + 1) * HEAD_DIM] = yh.astype(BF16)
            else:
                o16_ref[:, sl] = yh.astype(BF16)


def _proj(xn, w, rope_tabs=None, scale=1.0, want_f32=False, want_bf16=True,
          head_major=False, head_w=HEAD_DIM):
    m, d = xn.shape
    n = w.shape[1]
    tm = _tile(m, 512)
    rope = rope_tabs is not None
    in_specs = [pl.BlockSpec((tm, d), lambda i: (i, 0)),
                pl.BlockSpec((d, n), lambda i: (0, 0))]
    args = [xn, w]
    if rope:
        in_specs += [pl.BlockSpec((tm, HEAD_DIM), lambda i: (i, 0))] * 3
        args += list(rope_tabs)
    out_shape, out_specs = [], []
    if want_f32:
        out_shape.append(jax.ShapeDtypeStruct((m, n // head_w, head_w), F32))
        out_specs.append(pl.BlockSpec((tm, n // head_w, head_w), lambda i: (i, 0, 0)))
    if want_bf16:
        if head_major:
            nh = n // head_w
            out_shape.append(jax.ShapeDtypeStruct((nh, m, head_w), BF16))
            out_specs.append(pl.BlockSpec((nh, tm, head_w), lambda i: (0, i, 0)))
        else:
            out_shape.append(jax.ShapeDtypeStruct((m, n), BF16))
            out_specs.append(pl.BlockSpec((tm, n), lambda i: (i, 0)))
    kern = functools.partial(_proj_kernel, rope=rope, scale=scale, want_f32=want_f32,
                             want_bf16=want_bf16, head_major=head_major, head_w=head_w)
    return pl.pallas_call(
        kern, out_shape=out_shape, grid=(m // tm,), in_specs=in_specs, out_specs=out_specs,
        compiler_params=_cparams(("parallel",)), name="in_proj",
    )(*args)


def _rope_tables(pos, scale):
    half = ROT_DIM // 2
    t = pos.shape[0]
    inv_freq = jnp.power(ROPE_THETA, -jnp.arange(half, dtype=F32) * (2.0 / ROT_DIM))
    ang = pos.astype(F32)[:, None] * inv_freq[None, :]
    cos, sin = jnp.cos(ang), jnp.sin(ang)
    zeros = lambda w: jnp.zeros((t, w), F32)
    c = jnp.concatenate([cos, cos, jnp.ones((t, HEAD_DIM - ROT_DIM), F32)], axis=1)
    sa = jnp.concatenate([-sin, zeros(HEAD_DIM - half)], axis=1)
    sb = jnp.concatenate([zeros(half), sin, zeros(HEAD_DIM - ROT_DIM)], axis=1)
    return c * scale, sa * scale, sb * scale


def _logf_kernel(x_ref, w_ref, b_ref, o_ref):
    z = _dot(x_ref[...], w_ref[...]) + b_ref[...]
    o_ref[...] = jnp.minimum(z, 0.0) - jnp.log(1.0 + jnp.exp(-jnp.abs(z)))


def _logf(xn, w_pad, b_pad):
    m, d = xn.shape
    tm = _tile(m, 512)
    return pl.pallas_call(
        _logf_kernel,
        out_shape=jax.ShapeDtypeStruct((m, LANES), F32),
        grid=(m // tm,),
        in_specs=[pl.BlockSpec((tm, d), lambda i: (i, 0)),
                  pl.BlockSpec((d, LANES), lambda i: (0, 0)),
                  pl.BlockSpec((1, LANES), lambda i: (0, 0))],
        out_specs=pl.BlockSpec((tm, LANES), lambda i: (i, 0)),
        compiler_params=_cparams(("parallel",)), name="forget_logits",
    )(xn, w_pad, b_pad)


def _cumsum_kernel(x_ref, o_ref, carry_ref):
    @pl.when(pl.program_id(0) == 0)
    def _():
        carry_ref[...] = jnp.zeros_like(carry_ref)

    tb = x_ref.shape[0]
    row = lax.broadcasted_iota(jnp.int32, (tb, tb), 0)
    col = lax.broadcasted_iota(jnp.int32, (tb, tb), 1)
    tri = jnp.where(col <= row, 1.0, 0.0).astype(F32)
    c = jnp.dot(tri, x_ref[...], preferred_element_type=F32,
                precision=lax.Precision.HIGHEST) + carry_ref[...]
    o_ref[...] = c
    carry_ref[...] = c[tb - 1:tb, :]


def _cumsum_rows(x, tb):
    t, c = x.shape
    assert t % tb == 0 and c % LANES == 0
    return pl.pallas_call(
        _cumsum_kernel,
        out_shape=jax.ShapeDtypeStruct((t, c), F32),
        grid=(t // tb,),
        in_specs=[pl.BlockSpec((tb, c), lambda i: (i, 0))],
        out_specs=pl.BlockSpec((tb, c), lambda i: (i, 0)),
        scratch_shapes=[pltpu.VMEM((1, c), F32)],
        compiler_params=_cparams(("arbitrary",)), name="cumsum_rows",
    )(x)


def _lambda_value(lq1, lk1, lq2, lk2, lam_init):
    a = jnp.sum(lq1[...] * lk1[...], axis=-1, keepdims=True)
    b = jnp.sum(lq2[...] * lk2[...], axis=-1, keepdims=True)
    return jnp.exp(a) - jnp.exp(b) + lam_init


def _head_rmsnorm(o, g):
    return o * lax.rsqrt(jnp.mean(o * o, axis=-1, keepdims=True) + RMS_EPS) * g


def _online_update(s, m, l, acc_ref, v):
    m_new = jnp.maximum(m, jnp.max(s, axis=-1, keepdims=True))
    alpha = jnp.exp2(m - m_new)
    p = jnp.exp2(s - m_new)
    l_new = alpha * l + jnp.sum(p, axis=-1, keepdims=True)
    acc_ref[...] = alpha * acc_ref[...] + _dot(p.astype(BF16), v)
    return m_new, l_new


def _causal_sweep(i, scores_fn, values_fn, mask_fn, s_a, s_b, m_ref, l_ref, acc_refs):
    n_maps = len(acc_refs)
    for n in range(n_maps):
        m_ref[n] = jnp.full(m_ref.shape[1:], NEG_INF, F32)
        l_ref[n] = jnp.zeros(l_ref.shape[1:], F32)
        acc_refs[n][...] = jnp.zeros_like(acc_refs[n])

    def produce(bufs, j):
        for ref, s in zip(bufs, scores_fn(j)):
            ref[...] = s

    def consume(tiles, j):
        v = values_fn(j)
        for n in range(n_maps):
            m_new, l_new = _online_update(tiles[n], m_ref[n], l_ref[n], acc_refs[n], v)
            m_ref[n] = m_new
            l_ref[n] = l_new

    def pair(p, carry):
        produce(s_b, 2 * p + 1)
        consume([r[...] for r in s_a], 2 * p)
        produce(s_a, 2 * p + 2)
        consume([r[...] for r in s_b], 2 * p + 1)
        return carry

    produce(s_a, 0)
    lax.fori_loop(0, i // 2, pair, 0)
    diag = [mask_fn(s) for s in scores_fn(i)]

    @pl.when(i % 2 == 1)
    def _():
        consume([r[...] for r in s_a], i - 1)

    consume(diag, i)


def _key_tile(ref, j, tk):
    return ref[0, pl.ds(pl.multiple_of(j * tk, tk), tk), :]


def _diff_prompt_kernel(q_ref, k_ref, v_ref, g_ref, lq1, lk1, lq2, lk2, o_ref,
                        acc1, acc2, a1, a2, b1, b2, m_ref, l_ref, *, tq, lam_init):
    i = pl.program_id(1)
    q = q_ref[0]
    q1, q2 = q[:, :HEAD_DIM], q[:, HEAD_DIM:]

    def scores(j):
        k = _key_tile(k_ref, j, tq)
        return _dot_nt(q1, k[:, :HEAD_DIM]), _dot_nt(q2, k[:, HEAD_DIM:])

    def mask(s):
        row = lax.broadcasted_iota(jnp.int32, (tq, tq), 0)
        col = lax.broadcasted_iota(jnp.int32, (tq, tq), 1)
        return jnp.where((col >> CHUNK_SHIFT) <= (row >> CHUNK_SHIFT), s, NEG_INF)

    _causal_sweep(i, scores, lambda j: _key_tile(v_ref, j, tq), mask,
                  (a1, a2), (b1, b2), m_ref, l_ref, (acc1, acc2))
    lam = _lambda_value(lq1, lk1, lq2, lk2, lam_init)
    o = acc1[...] / l_ref[0] - lam * (acc2[...] / l_ref[1])
    o_ref[...] = (_head_rmsnorm(o, g_ref[...]) * (1.0 - lam_init)).astype(o_ref.dtype)


def _diff_prompt_attention(q, k, v, g, lams, lam_init):
    nh, t, w = q.shape
    tq = _tile(t, 512)
    assert tq % (1 << CHUNK_SHIFT) == 0
    vec = pl.BlockSpec((1, HEAD_DIM), lambda h, i: (0, 0))
    kern = functools.partial(_diff_prompt_kernel, tq=tq, lam_init=lam_init)
    return pl.pallas_call(
        kern,
        out_shape=jax.ShapeDtypeStruct((t, nh * w), BF16),
        grid=(nh, t // tq),
        in_specs=[pl.BlockSpec((1, tq, w), lambda h, i: (h, i, 0)),
                  pl.BlockSpec((1, t, w), lambda h, i: (h, 0, 0)),
                  pl.BlockSpec((1, t, w), lambda h, i: (h, 0, 0)),
                  pl.BlockSpec((1, w), lambda h, i: (0, 0)),
                  vec, vec, vec, vec],
        out_specs=pl.BlockSpec((tq, w), lambda h, i: (i, h)),
        scratch_shapes=[pltpu.VMEM((tq, w), F32)] * 2 + [pltpu.VMEM((tq, tq), F32)] * 4
        + [pltpu.VMEM((2, tq, 1), F32)] * 2,
        compiler_params=_cparams(("parallel", "arbitrary")), name="diff_prompt_attention",
    )(q, k, v, g, *lams)


def _fox_prompt_kernel(q_ref, k_ref, v_ref, ck_ref, g_ref, o_ref, acc, s_a, s_b, m_ref, l_ref, *, tq):
    i = pl.program_id(1)
    q = q_ref[0]

    def scores(j):
        return (_dot_nt(q, _key_tile(k_ref, j, tq)) - ck_ref[0, j] * LOG2E,)

    def mask(s):
        row = lax.broadcasted_iota(jnp.int32, (tq, tq), 0)
        col = lax.broadcasted_iota(jnp.int32, (tq, tq), 1)
        return jnp.where(col <= row, s, NEG_INF)

    _causal_sweep(i, scores, lambda j: _key_tile(v_ref, j, tq), mask,
                  (s_a,), (s_b,), m_ref, l_ref, (acc,))
    o_ref[...] = _head_rmsnorm(acc[...] / l_ref[0], g_ref[...]).astype(o_ref.dtype)


def _fox_prompt_attention(q, k, v, ck, g):
    nh, t, w = q.shape
    tq = ck.shape[-1]
    kern = functools.partial(_fox_prompt_kernel, tq=tq)
    return pl.pallas_call(
        kern,
        out_shape=jax.ShapeDtypeStruct((t, nh * w), BF16),
        grid=(nh, t // tq),
        in_specs=[pl.BlockSpec((1, tq, w), lambda h, i: (h, i, 0)),
                  pl.BlockSpec((1, t, w), lambda h, i: (h, 0, 0)),
                  pl.BlockSpec((1, t, w), lambda h, i: (h, 0, 0)),
                  pl.BlockSpec((1, t // tq, 1, tq), lambda h, i: (h, 0, 0, 0)),
                  pl.BlockSpec((1, w), lambda h, i: (0, 0))],
        out_specs=pl.BlockSpec((tq, w), lambda h, i: (i, h)),
        scratch_shapes=[pltpu.VMEM((tq, w), F32)] + [pltpu.VMEM((tq, tq), F32)] * 2
        + [pltpu.VMEM((1, tq, 1), F32)] * 2,
        compiler_params=_cparams(("parallel", "arbitrary")), name="fox_prompt_attention",
    )(q, k, v, ck, g)


def _two_segment_softmax(sp, sn):
    m = jnp.maximum(jnp.max(sp, axis=-1, keepdims=True), jnp.max(sn, axis=-1, keepdims=True))
    pp = jnp.exp2(sp - m)
    pn = jnp.exp2(sn - m)
    l = jnp.sum(pp, axis=-1, keepdims=True) + jnp.sum(pn, axis=-1, keepdims=True)
    return pp.astype(BF16), pn.astype(BF16), l


def _cache_view(cache):
    b, t, nh, w = cache.shape
    nc = w // LANES
    return cache.reshape(b, t, nh, nc, LANES).transpose(0, 1, 3, 2, 4).reshape(b, t * nc * nh, LANES)


def _cache_rows(ref, head, lane_block, n_heads, n_lane_blocks, n_rows):
    stride = n_heads * n_lane_blocks
    return ref[0, pl.ds(lane_block * n_heads + head, n_rows, stride=stride), :].astype(BF16)


def _diff_sample_kernel(q_ref, kn_ref, vn_ref, kp_ref, vp_ref, g_ref, lq1, lk1, lq2, lk2,
                        o_ref, *, past_len, lam_init, n_heads):
    t = q_ref.shape[1]
    w = 2 * HEAD_DIM
    row = lax.broadcasted_iota(jnp.int32, (t, t), 0) + past_len
    col = lax.broadcasted_iota(jnp.int32, (t, t), 1) + past_len
    ok = (col >> CHUNK_SHIFT) <= (row >> CHUNK_SHIFT)
    lam = _lambda_value(lq1, lk1, lq2, lk2, lam_init)
    for h in range(n_heads):
        sl = slice(h * w, (h + 1) * w)
        q = q_ref[0, :, sl]
        kn = kn_ref[0, :, sl]
        vn = vn_ref[0, :, sl]
        vp = [_cache_rows(vp_ref, h, c, n_heads, 2, past_len) for c in range(2)]
        outs = []
        for half in range(2):
            hs = slice(half * HEAD_DIM, (half + 1) * HEAD_DIM)
            sp = _dot_nt(q[:, hs], _cache_rows(kp_ref, h, half, n_heads, 2, past_len))
            sn = jnp.where(ok, _dot_nt(q[:, hs], kn[:, hs]), NEG_INF)
            pp, pn, l = _two_segment_softmax(sp, sn)
            o_past = jnp.concatenate([_dot(pp, vp[0]), _dot(pp, vp[1])], axis=1)
            outs.append((o_past + _dot(pn, vn)) / l)
        o = outs[0] - lam * outs[1]
        o_ref[0, :, sl] = (_head_rmsnorm(o, g_ref[...]) * (1.0 - lam_init)).astype(o_ref.dtype)


def _diff_sample_attention(q, kn, vn, kp, vp, g, lams, lam_init):
    b, t, n = q.shape
    past = kp.shape[1]
    kp, vp = _cache_view(kp), _cache_view(vp)
    new = pl.BlockSpec((1, t, n), lambda i: (i, 0, 0))
    old = pl.BlockSpec((1,) + kp.shape[1:], lambda i: (i, 0, 0))
    vec = pl.BlockSpec((1, HEAD_DIM), lambda i: (0, 0))
    kern = functools.partial(_diff_sample_kernel, past_len=past, lam_init=lam_init,
                             n_heads=n // (2 * HEAD_DIM))
    return pl.pallas_call(
        kern,
        out_shape=jax.ShapeDtypeStruct((b, t, n), BF16),
        grid=(b,),
        in_specs=[new, new, new, old, old,
                  pl.BlockSpec((1, 2 * HEAD_DIM), lambda i: (0, 0)), vec, vec, vec, vec],
        out_specs=new,
        compiler_params=_cparams(("parallel",)), name="diff_sample_attention",
    )(q, kn, vn, kp, vp, g, *lams)


def _fox_sample_kernel(q_ref, kn_ref, vn_ref, kp_ref, vp_ref, ckp_ref, ckn_ref, g_ref, o_ref,
                       *, n_heads):
    t = q_ref.shape[1]
    row = lax.broadcasted_iota(jnp.int32, (t, t), 0)
    col = lax.broadcasted_iota(jnp.int32, (t, t), 1)
    ok = col <= row
    for h in range(n_heads):
        sl = slice(h * HEAD_DIM, (h + 1) * HEAD_DIM)
        q = q_ref[0, :, sl]
        past = ckp_ref.shape[-1]
        kp = _cache_rows(kp_ref, h, 0, n_heads, 1, past)
        vp = _cache_rows(vp_ref, h, 0, n_heads, 1, past)
        sp = _dot_nt(q, kp) - ckp_ref[0, h] * LOG2E
        sn = jnp.where(ok, _dot_nt(q, kn_ref[0, :, sl]) - ckn_ref[0, h] * LOG2E, NEG_INF)
        pp, pn, l = _two_segment_softmax(sp, sn)
        o = (_dot(pp, vp) + _dot(pn, vn_ref[0, :, sl])) / l
        o_ref[0, :, sl] = _head_rmsnorm(o, g_ref[...]).astype(o_ref.dtype)


def _fox_sample_attention(q, kn, vn, kp, vp, ckp, ckn, g):
    b, t, n = q.shape
    past = kp.shape[1]
    nh = n // HEAD_DIM
    kp, vp = _cache_view(kp), _cache_view(vp)
    new = pl.BlockSpec((1, t, n), lambda i: (i, 0, 0))
    old = pl.BlockSpec((1,) + kp.shape[1:], lambda i: (i, 0, 0))
    kern = functools.partial(_fox_sample_kernel, n_heads=nh)
    return pl.pallas_call(
        kern,
        out_shape=jax.ShapeDtypeStruct((b, t, n), BF16),
        grid=(b,),
        in_specs=[new, new, new, old, old,
                  pl.BlockSpec((1, nh, 1, past), lambda i: (i, 0, 0, 0)),
                  pl.BlockSpec((1, nh, 1, t), lambda i: (i, 0, 0, 0)),
                  pl.BlockSpec((1, HEAD_DIM), lambda i: (0, 0))],
        out_specs=new,
        compiler_params=_cparams(("parallel",)), name="fox_sample_attention",
    )(q, kn, vn, kp, vp, ckp, ckn, g)


def _outproj_router_kernel(md_ref, mf_ref, wd_ref, wf_ref, x_ref, g_ref, wr_ref, br_ref,
                           h_ref, hn_ref, idx_ref, gate_ref):
    h = x_ref[...] + _dot(md_ref[...], wd_ref[...]) + _dot(mf_ref[...], wf_ref[...])
    h_ref[...] = h
    hn = h * lax.rsqrt(jnp.mean(h * h, axis=-1, keepdims=True) + RMS_EPS) * g_ref[...]
    hn_ref[...] = hn
    logits = jnp.dot(hn, wr_ref[...], preferred_element_type=F32,
                     precision=lax.Precision.HIGHEST) + br_ref[...]
    lane = lax.broadcasted_iota(jnp.int32, logits.shape, 1)
    work = logits
    idx_out = jnp.zeros(logits.shape, jnp.int32)
    val_out = jnp.zeros(logits.shape, F32)
    top = None
    denom = None
    for k in range(TOP_K):
        mx = jnp.max(work, axis=-1, keepdims=True)
        first = jnp.min(jnp.where(work == mx, lane, LANES), axis=-1, keepdims=True)
        work = jnp.where(lane == first, -jnp.inf, work)
        if k == 0:
            top = mx
        e = jnp.exp(mx - top)
        denom = e if k == 0 else denom + e
        idx_out = jnp.where(lane == k, first, idx_out)
        val_out = jnp.where(lane == k, e, val_out)
    idx_ref[...] = idx_out
    gate_ref[...] = val_out / denom


def _outproj_router(mix_d, mix_f, w_d, w_f, x, g, w_router_pad, b_router_pad):
    m, d = x.shape
    kd, kf = mix_d.shape[1], mix_f.shape[1]
    tm = _tile(m, 256)
    row = lambda w: pl.BlockSpec((tm, w), lambda i: (i, 0))
    full = lambda r, c: pl.BlockSpec((r, c), lambda i: (0, 0))
    return pl.pallas_call(
        _outproj_router_kernel,
        out_shape=[jax.ShapeDtypeStruct((m, d), F32), jax.ShapeDtypeStruct((m, d), F32),
                   jax.ShapeDtypeStruct((m, LANES), jnp.int32),
                   jax.ShapeDtypeStruct((m, LANES), F32)],
        grid=(m // tm,),
        in_specs=[row(kd), row(kf), full(kd, d), full(kf, d), row(d), full(1, d),
                  full(d, LANES), full(1, LANES)],
        out_specs=[row(d), row(d), row(LANES), row(LANES)],
        compiler_params=_cparams(("parallel",)), name="outproj_router",
    )(mix_d, mix_f, w_d, w_f, x, g, w_router_pad, b_router_pad)


def _rank_kernel(idx_ref, rank_ref, count_ref, carry_ref):
    @pl.when(pl.program_id(0) == 0)
    def _():
        carry_ref[...] = jnp.zeros_like(carry_ref)

    tm = idx_ref.shape[0]
    idx = idx_ref[...]
    lane = lax.broadcasted_iota(jnp.int32, (tm, LANES), 1)
    picks = [lane == idx[:, k:k + 1] for k in range(TOP_K)]
    onehot = jnp.zeros((tm, LANES), F32)
    for p in picks:
        onehot = onehot + jnp.where(p, 1.0, 0.0)
    row = lax.broadcasted_iota(jnp.int32, (tm, tm), 0)
    col = lax.broadcasted_iota(jnp.int32, (tm, tm), 1)
    strict = jnp.where(col < row, 1.0, 0.0).astype(BF16)
    before = _dot(strict, onehot.astype(BF16)) + carry_ref[...]
    rank = jnp.zeros((tm, LANES), F32)
    for k, p in enumerate(picks):
        r = jnp.sum(jnp.where(p, before, 0.0), axis=-1, keepdims=True)
        rank = jnp.where(lane == k, r, rank)
    rank_ref[...] = rank.astype(jnp.int32)
    total = before[tm - 1:tm, :] + onehot[tm - 1:tm, :]
    carry_ref[...] = total
    count_ref[...] = total.astype(jnp.int32)


def _expert_ranks(idx):
    n_tok = idx.shape[0]
    tm = _tile(n_tok, 512)
    return pl.pallas_call(
        _rank_kernel,
        out_shape=[jax.ShapeDtypeStruct((n_tok, LANES), jnp.int32),
                   jax.ShapeDtypeStruct((1, LANES), jnp.int32)],
        grid=(n_tok // tm,),
        in_specs=[pl.BlockSpec((tm, LANES), lambda i: (i, 0))],
        out_specs=[pl.BlockSpec((tm, LANES), lambda i: (i, 0)),
                   pl.BlockSpec((1, LANES), lambda i: (0, 0))],
        scratch_shapes=[pltpu.VMEM((1, LANES), F32)],
        compiler_params=_cparams(("arbitrary",)), name="expert_ranks",
    )(idx)


def _row_copy(src_ref, dst_ref, sem, src_row, dst_row):
    return pltpu.make_async_copy(src_ref.at[pl.ds(src_row, 1)], dst_ref.at[pl.ds(dst_row, 1)], sem)


def _gather_rows(idx_ref, src_ref, dst_ref, sem, n_rows):
    assert n_rows % GATHER_UNROLL == 0

    def start(g, carry):
        base = pl.multiple_of(g * GATHER_UNROLL, GATHER_UNROLL)
        for u in range(GATHER_UNROLL):
            _row_copy(src_ref, dst_ref, sem, idx_ref[0, 0, base + u], base + u).start()
        return carry

    def wait(g, carry):
        base = pl.multiple_of(g * GATHER_UNROLL, GATHER_UNROLL)
        for u in range(GATHER_UNROLL):
            _row_copy(src_ref, dst_ref, sem, 0, base + u).wait()
        return carry

    lax.fori_loop(0, n_rows // GATHER_UNROLL, start, 0)
    lax.fori_loop(0, n_rows // GATHER_UNROLL, wait, 0)


def _dispatch_kernel(tok_ref, hn_ref, o_ref, buf, sem):
    _gather_rows(tok_ref, hn_ref, buf, sem, buf.shape[0])
    o_ref[...] = buf[...].astype(o_ref.dtype)


def _dispatch(row_tok, hn):
    n_blocks, _, rows = row_tok.shape
    d = hn.shape[1]
    return pl.pallas_call(
        _dispatch_kernel,
        out_shape=jax.ShapeDtypeStruct((n_blocks * rows, d), BF16),
        grid=(n_blocks,),
        in_specs=[pl.BlockSpec((1, 1, rows), lambda b: (b, 0, 0), memory_space=pltpu.SMEM),
                  pl.BlockSpec(memory_space=pl.ANY)],
        out_specs=pl.BlockSpec((rows, d), lambda b: (b, 0)),
        scratch_shapes=[pltpu.VMEM((rows, d), hn.dtype), pltpu.SemaphoreType.DMA],
        compiler_params=_cparams(("arbitrary",)), name="moe_dispatch",
    )(row_tok, hn)


def _combine_kernel(dest_ref, y_ref, h_ref, gate_ref, g_ref, o_ref, buf, sem, *, tm):
    _gather_rows(dest_ref, y_ref, buf, sem, TOP_K * tm)
    gates = gate_ref[...]
    out = h_ref[...]
    for k in range(TOP_K):
        out = out + buf[k * tm:(k + 1) * tm, :] * gates[:, k:k + 1]
    out = out * lax.rsqrt(jnp.mean(out * out, axis=-1, keepdims=True) + RMS_EPS)
    o_ref[...] = out * g_ref[...]


def _combine(dest, y_rows, h, gates, g):
    n_tiles, _, per = dest.shape
    tm = per // TOP_K
    m, d = h.shape
    kern = functools.partial(_combine_kernel, tm=tm)
    return pl.pallas_call(
        kern,
        out_shape=jax.ShapeDtypeStruct((m, d), F32),
        grid=(n_tiles,),
        in_specs=[pl.BlockSpec((1, 1, per), lambda i: (i, 0, 0), memory_space=pltpu.SMEM),
                  pl.BlockSpec(memory_space=pl.ANY),
                  pl.BlockSpec((tm, d), lambda i: (i, 0)),
                  pl.BlockSpec((tm, LANES), lambda i: (i, 0)),
                  pl.BlockSpec((1, d), lambda i: (0, 0))],
        out_specs=pl.BlockSpec((tm, d), lambda i: (i, 0)),
        scratch_shapes=[pltpu.VMEM((per, d), y_rows.dtype), pltpu.SemaphoreType.DMA],
        compiler_params=_cparams(("arbitrary",)), name="moe_combine",
    )(dest, y_rows, h, gates, g)


def _expert_changed(be_ref, rb):
    prev = be_ref[jnp.maximum(rb - 1, 0)]
    return jnp.logical_or(rb == 0, be_ref[rb] != prev)


def _moe_up_kernel(be_ref, used_ref, x_ref, wg_ref, wu_ref, bg_ref, bu_ref, o_ref, wg16, wu16):
    rb = pl.program_id(1)

    @pl.when(_expert_changed(be_ref, rb))
    def _():
        wg16[...] = wg_ref[0].astype(BF16)
        wu16[...] = wu_ref[0].astype(BF16)

    @pl.when(rb < used_ref[0])
    def _():
        x = x_ref[...]
        gate = jnp.minimum(_dot(x, wg16[...]) + bg_ref[0], SWIGLU_LIMIT)
        up = jnp.clip(_dot(x, wu16[...]) + bu_ref[0], -SWIGLU_LIMIT, SWIGLU_LIMIT)
        act = (up + 1.0) * (gate * jax.nn.sigmoid(SWIGLU_ALPHA * gate))
        o_ref[...] = act.astype(o_ref.dtype)

    @pl.when(rb >= used_ref[0])
    def _():
        o_ref[...] = jnp.zeros_like(o_ref)


def _moe_up(block_expert, n_used, xs, w_gate, w_up, b_gate, b_up, tn):
    n_rows, d = xs.shape
    n_exp, _, d_ff = w_gate.shape
    rows = MOE_ROWS
    n_blocks = n_rows // rows
    wspec = pl.BlockSpec((1, d, tn), lambda n, rb, be, used: (be[rb], 0, n))
    bspec = pl.BlockSpec((1, 1, tn), lambda n, rb, be, used: (be[rb], 0, n))
    return pl.pallas_call(
        _moe_up_kernel,
        out_shape=jax.ShapeDtypeStruct((n_rows, d_ff), BF16),
        grid_spec=pltpu.PrefetchScalarGridSpec(
            num_scalar_prefetch=2,
            grid=(d_ff // tn, n_blocks),
            in_specs=[pl.BlockSpec((rows, d), lambda n, rb, be, used: (rb, 0)),
                      wspec, wspec, bspec, bspec],
            out_specs=pl.BlockSpec((rows, tn), lambda n, rb, be, used: (rb, n)),
            scratch_shapes=[pltpu.VMEM((d, tn), BF16), pltpu.VMEM((d, tn), BF16)]),
        compiler_params=_cparams(("arbitrary", "arbitrary")), name="moe_gate_up",
    )(block_expert, n_used, xs, w_gate, w_up, b_gate.reshape(n_exp, 1, d_ff),
      b_up.reshape(n_exp, 1, d_ff))


def _moe_down_kernel(be_ref, used_ref, a_ref, w_ref, b_ref, o_ref, w16):
    rb = pl.program_id(1)

    @pl.when(_expert_changed(be_ref, rb))
    def _():
        w16[...] = w_ref[0].astype(BF16)

    @pl.when(rb < used_ref[0])
    def _():
        o_ref[...] = _dot(a_ref[...], w16[...]) + b_ref[0]

    @pl.when(rb >= used_ref[0])
    def _():
        o_ref[...] = jnp.zeros_like(o_ref)


def _moe_down(block_expert, n_used, act, w_down, b_down, tn):
    n_rows, d_ff = act.shape
    n_exp, _, d = w_down.shape
    rows = MOE_ROWS
    return pl.pallas_call(
        _moe_down_kernel,
        out_shape=jax.ShapeDtypeStruct((n_rows, d), F32),
        grid_spec=pltpu.PrefetchScalarGridSpec(
            num_scalar_prefetch=2,
            grid=(d // tn, n_rows // rows),
            in_specs=[pl.BlockSpec((rows, d_ff), lambda n, rb, be, used: (rb, 0)),
                      pl.BlockSpec((1, d_ff, tn), lambda n, rb, be, used: (be[rb], 0, n)),
                      pl.BlockSpec((1, 1, tn), lambda n, rb, be, used: (be[rb], 0, n))],
            out_specs=pl.BlockSpec((rows, tn), lambda n, rb, be, used: (rb, n)),
            scratch_shapes=[pltpu.VMEM((d_ff, tn), BF16)]),
        compiler_params=_cparams(("arbitrary", "arbitrary")), name="moe_down",
    )(block_expert, n_used, act, w_down, b_down.reshape(n_exp, 1, d))


def _pad_lanes(a, fill=0.0):
    return jnp.pad(a, ((0, 0), (0, LANES - a.shape[1])), constant_values=fill)


def _mixer_projections(x2d, pos, norm_g, w_in, b_forget, head_major):
    d = x2d.shape[1]
    dw = (w_in.shape[1] - 8) // 6
    q_scale = HEAD_DIM ** -0.5 * LOG2E
    xn = _rmsnorm_bf16(x2d, norm_g)
    w16 = w_in.astype(BF16)
    wcol = lambda j: w16[:, j * dw:(j + 1) * dw]
    tabs_q = _rope_tables(pos, q_scale)
    tabs_k = _rope_tables(pos, 1.0)
    hm = dict(head_major=head_major)
    (dq,) = _proj(xn, wcol(0), rope_tabs=tabs_q, head_w=2 * HEAD_DIM, **hm)
    dk32, dk = _proj(xn, wcol(1), rope_tabs=tabs_k, want_f32=True, head_w=2 * HEAD_DIM, **hm)
    dv32, dv = _proj(xn, wcol(2), want_f32=True, head_w=2 * HEAD_DIM, **hm)
    (fq,) = _proj(xn, wcol(3), scale=q_scale, **hm)
    fk32, fk = _proj(xn, wcol(4), want_f32=True, **hm)
    fv32, fv = _proj(xn, wcol(5), want_f32=True, **hm)
    logf = _logf(xn, _pad_lanes(w16[:, 6 * dw:]), _pad_lanes(b_forget.reshape(1, -1)))
    return (dq, dk, dv, fq, fk, fv), (dk32, dv32, fk32, fv32), logf


def _moe(hn_all, idx_all, w_gate, w_up, w_down, b_gate, b_up, b_down):
    n_tok, d = hn_all.shape
    n_exp = w_gate.shape[0]
    rows = MOE_ROWS
    n_blocks = -(-(n_tok * TOP_K) // rows) + n_exp
    n_rows = n_blocks * rows

    rank, counts = _expert_ranks(idx_all)
    experts = idx_all[:, :TOP_K]
    counts = counts[0, :n_exp]
    padded = (counts + rows - 1) // rows * rows
    padded_end = jnp.cumsum(padded)
    padded_start = padded_end - padded
    dest = padded_start[experts] + rank[:, :TOP_K]
    token_of = jnp.broadcast_to(jnp.arange(n_tok, dtype=jnp.int32)[:, None], dest.shape)
    row_tok = jnp.zeros((n_rows,), jnp.int32).at[dest.reshape(-1)].set(token_of.reshape(-1))
    block_start = jnp.arange(n_blocks, dtype=jnp.int32) * rows
    block_expert = jnp.minimum(
        jnp.sum((padded_end[None, :] <= block_start[:, None]).astype(jnp.int32), axis=1), n_exp - 1)
    n_used = (padded_end[-1] // rows).astype(jnp.int32).reshape(1)

    step_rows = _tile(n_rows, DISPATCH_ROWS)
    xs = _dispatch(row_tok.reshape(n_rows // step_rows, 1, step_rows), hn_all)
    act = _moe_up(block_expert, n_used, xs, w_gate, w_up, b_gate, b_up, tn=512)
    y_rows = _moe_down(block_expert, n_used, act, w_down, b_down, tn=min(1024, w_down.shape[2]))
    return y_rows, dest


def _combine_tokens(dest, y_rows, h, gates, g):
    m = h.shape[0]
    tm = _tile(m, 256)
    dest_t = dest.reshape(m // tm, tm, TOP_K).transpose(0, 2, 1).reshape(m // tm, 1, TOP_K * tm)
    return _combine(dest_t, y_rows, h, gates, g.reshape(1, -1))


def kernel(x_prompt, x_sample, cache_diff_k, cache_diff_v, cache_fox_k, cache_fox_v, cache_fox_logf,
           attn_norm_g, w_in, b_forget, lambda_q1, lambda_k1, lambda_q2, lambda_k2,
           diff_out_norm_g, fox_out_norm_g, w_out, ffn_norm_g, w_router, b_router,
           w_gate, b_gate, w_up, b_up, w_down, b_down, final_norm_g):
    depth = w_in.shape[0]
    assert depth == 1, "single-layer stack"
    lam_init = 0.8 - 0.6 * math.exp(-0.3 * 0)
    drop = lambda a: a.reshape(a.shape[1:])
    (cache_diff_k, cache_diff_v, cache_fox_k, cache_fox_v, cache_fox_logf, attn_norm_g, w_in,
     b_forget, lambda_q1, lambda_k1, lambda_q2, lambda_k2, diff_out_norm_g, fox_out_norm_g, w_out,
     ffn_norm_g, w_router, b_router, w_gate, b_gate, w_up, b_up, w_down, b_down) = map(drop, (
         cache_diff_k, cache_diff_v, cache_fox_k, cache_fox_v, cache_fox_logf, attn_norm_g, w_in,
         b_forget, lambda_q1, lambda_k1, lambda_q2, lambda_k2, diff_out_norm_g, fox_out_norm_g,
         w_out, ffn_norm_g, w_router, b_router, w_gate, b_gate, w_up, b_up, w_down, b_down))
    bp, t, d = x_prompt.shape
    assert bp == 1
    bs, ts, _ = x_sample.shape
    past = cache_diff_k.shape[1]
    n_dh = cache_diff_k.shape[2]
    n_fh = cache_fox_k.shape[2]
    dw = n_dh * 2 * HEAD_DIM
    fw = n_fh * HEAD_DIM
    lams = [v.reshape(1, HEAD_DIM) for v in (lambda_q1, lambda_k1, lambda_q2, lambda_k2)]
    diff_g = diff_out_norm_g.reshape(1, -1)
    fox_g = fox_out_norm_g.reshape(1, -1)

    xp = x_prompt.reshape(t, d)
    pos_p = jnp.arange(t, dtype=jnp.int32)
    (dq, dk, dv, fq, fk, fv), cache_p, logf_p = _mixer_projections(
        xp, pos_p, attn_norm_g, w_in, b_forget, head_major=True)
    tq = _tile(t, 512)
    c_p = _cumsum_rows(logf_p, _tile(t, 512))
    ck_p = c_p[:, :n_fh].T.reshape(n_fh, t // tq, 1, tq)
    mix_d_p = _diff_prompt_attention(dq, dk, dv, diff_g, lams, lam_init)
    mix_f_p = _fox_prompt_attention(fq, fk, fv, ck_p, fox_g)

    xs = x_sample.reshape(bs * ts, d)
    pos_s = past + jnp.arange(ts, dtype=jnp.int32)
    (sdq, sdk, sdv, sfq, sfk, sfv), cache_s, logf_s = _mixer_projections(
        xs, jnp.tile(pos_s, bs), attn_norm_g, w_in, b_forget, head_major=False)
    logf_all = jnp.concatenate([cache_fox_logf.astype(F32),
                                logf_s[:, :n_fh].reshape(bs, ts, n_fh)], axis=1)
    tt = past + ts
    lf_cols = logf_all.transpose(1, 0, 2).reshape(tt, bs * n_fh)
    pad_c = (-lf_cols.shape[1]) % LANES
    lf_cols = jnp.pad(lf_cols, ((0, 0), (0, pad_c)))
    tb = max(tb_ for tb_ in range(8, 513, 8) if tt % tb_ == 0)
    c_s = _cumsum_rows(lf_cols, tb)[:, :bs * n_fh].reshape(tt, bs, n_fh).transpose(1, 2, 0)
    ckp_s = c_s[:, :, None, :past]
    ckn_s = c_s[:, :, None, past:]
    r3 = lambda a: a.reshape(bs, ts, -1)
    mix_d_s = _diff_sample_attention(
        r3(sdq), r3(sdk), r3(sdv), cache_diff_k, cache_diff_v,
        diff_g, lams, lam_init).reshape(bs * ts, dw)
    mix_f_s = _fox_sample_attention(
        r3(sfq), r3(sfk), r3(sfv), cache_fox_k, cache_fox_v,
        ckp_s, ckn_s, fox_g).reshape(bs * ts, fw)

    w_out16 = w_out.astype(BF16)
    w_router_pad = _pad_lanes(w_router)
    b_router_pad = _pad_lanes(b_router.reshape(1, -1), fill=NEG_INF)
    ffn_g = ffn_norm_g.reshape(1, -1)
    h_p, hn_p, idx_p, gate_p = _outproj_router(mix_d_p, mix_f_p, w_out16[:dw], w_out16[dw:],
                                               xp, ffn_g, w_router_pad, b_router_pad)
    h_s, hn_s, idx_s, gate_s = _outproj_router(mix_d_s, mix_f_s, w_out16[:dw], w_out16[dw:],
                                               xs, ffn_g, w_router_pad, b_router_pad)

    hn_all = jnp.concatenate([hn_p, hn_s], axis=0)
    idx_all = jnp.concatenate([idx_p, idx_s], axis=0)
    y_rows, dest = _moe(hn_all, idx_all, w_gate, w_up, w_down, b_gate, b_up, b_down)
    y_p = _combine_tokens(dest[:t], y_rows, h_p, gate_p, final_norm_g)
    y_s = _combine_tokens(dest[t:], y_rows, h_s, gate_s, final_norm_g)

    dk32, dv32, fk32, fv32 = cache_p
    sdk32, sdv32, sfk32, sfv32 = cache_s
    return (y_p.reshape(1, t, d), y_s.reshape(bs, ts, d),
            dk32.reshape(1, 1, t, n_dh, 2 * HEAD_DIM), dv32.reshape(1, 1, t, n_dh, 2 * HEAD_DIM),
            fk32.reshape(1, 1, t, n_fh, HEAD_DIM), fv32.reshape(1, 1, t, n_fh, HEAD_DIM),
            logf_p[:, :n_fh].reshape(1, 1, t, n_fh),
            sdk32.reshape(1, bs, ts, n_dh, 2 * HEAD_DIM), sdv32.reshape(1, bs, ts, n_dh, 2 * HEAD_DIM),
            sfk32.reshape(1, bs, ts, n_fh, HEAD_DIM), sfv32.reshape(1, bs, ts, n_fh, HEAD_DIM),
            logf_s[:, :n_fh].reshape(1, bs, ts, n_fh))
```

```python
import functools
import math

import jax
import jax.numpy as jnp
from jax import lax
from jax.experimental import pallas as pl
from jax.experimental.pallas import tpu as pltpu

HEAD_DIM = 128
CHUNK_SHIFT = 6
ROT_DIM = HEAD_DIM // 4
ROPE_THETA = 500000.0
TOP_K = 4
SWIGLU_LIMIT = 7.0
SWIGLU_ALPHA = 1.702
RMS_EPS = 1e-6
NEG_INF = -1e30
LOG2E = 1.4426950408889634
LANES = 128
MOE_ROWS = 256
GATHER_UNROLL = 8
DISPATCH_ROWS = 1024
VMEM_LIMIT = 56 * 1024 * 1024

F32 = jnp.float32
BF16 = jnp.bfloat16


def _cparams(sem, vmem=VMEM_LIMIT):
    return pltpu.CompilerParams(dimension_semantics=sem, vmem_limit_bytes=vmem)


def _tile(n, pref):
    if n <= pref:
        return n
    t = max(c for c in range(8, pref + 1, 8) if n % c == 0)
    return t


def _dot_nt(a, b):
    return lax.dot_general(a, b, (((1,), (1,)), ((), ())), preferred_element_type=F32)


def _dot(a, b):
    return jnp.dot(a, b, preferred_element_type=F32)


def _rmsnorm_kernel(x_ref, g_ref, o_ref):
    x = x_ref[...]
    y = x * lax.rsqrt(jnp.mean(x * x, axis=-1, keepdims=True) + RMS_EPS)
    o_ref[...] = (y * g_ref[...]).astype(o_ref.dtype)


def _rmsnorm_bf16(x, g):
    m, d = x.shape
    tm = _tile(m, 512)
    return pl.pallas_call(
        _rmsnorm_kernel,
        out_shape=jax.ShapeDtypeStruct((m, d), BF16),
        grid=(m // tm,),
        in_specs=[pl.BlockSpec((tm, d), lambda i: (i, 0)),
                  pl.BlockSpec((1, d), lambda i: (0, 0))],
        out_specs=pl.BlockSpec((tm, d), lambda i: (i, 0)),
        compiler_params=_cparams(("parallel",)),
        name="rmsnorm_bf16",
    )(x, g.reshape(1, d))


def _proj_kernel(*refs, rope, scale, want_f32, want_bf16, head_major, head_w):
    it = iter(refs)
    x_ref = next(it)
    w_ref = next(it)
    if rope:
        c_ref, sa_ref, sb_ref = next(it), next(it), next(it)
    o32_ref = next(it) if want_f32 else None
    o16_ref = next(it) if want_bf16 else None

    n = w_ref.shape[1]
    y = _dot(x_ref[...], w_ref[...])
    for h in range(n // HEAD_DIM):
        sl = slice(h * HEAD_DIM, (h + 1) * HEAD_DIM)
        yh = y[:, sl]
        if rope:
            yh = (yh * c_ref[...]
                  + pltpu.roll(yh, HEAD_DIM - ROT_DIM // 2, 1) * sa_ref[...]
                  + pltpu.roll(yh, ROT_DIM // 2, 1) * sb_ref[...])
        elif scale != 1.0:
            yh = yh * scale
        per = head_w // HEAD_DIM
        if want_f32:
            o32_ref[:, h // per, (h % per) * HEAD_DIM:(h % per + 1) * HEAD_DIM] = yh
        if want_bf16:
            if head_major:
                o16_ref[h // per, :, (h % per) * HEAD_DIM:(h % per + 1) * HEAD_DIM] = yh.astype(BF16)
            else:
                o16_ref[:, sl] = yh.astype(BF16)


def _proj(xn, w, rope_tabs=None, scale=1.0, want_f32=False, want_bf16=True,
          head_major=False, head_w=HEAD_DIM):
    m, d = xn.shape
    n = w.shape[1]
    tm = _tile(m, 512)
    rope = rope_tabs is not None
    in_specs = [pl.BlockSpec((tm, d), lambda i: (i, 0)),
                pl.BlockSpec((d, n), lambda i: (0, 0))]
    args = [xn, w]
    if rope:
        in_specs += [pl.BlockSpec((tm, HEAD_DIM), lambda i: (i, 0))] * 3
        args += list(rope_tabs)
    out_shape, out_specs = [], []
    if want_f32:
        out_shape.append(jax.ShapeDtypeStruct((m, n // head_w, head_w), F32))
        out_specs.append(pl.BlockSpec((tm, n // head_w, head_w), lambda i: (i, 0, 0)))
    if want_bf16:
        if head_major:
            nh = n // head_w
            out_shape.append(jax.ShapeDtypeStruct((nh, m, head_w), BF16))
            out_specs.append(pl.BlockSpec((nh, tm, head_w), lambda i: (0, i, 0)))
        else:
            out_shape.append(jax.ShapeDtypeStruct((m, n), BF16))
            out_specs.append(pl.BlockSpec((tm, n), lambda i: (i, 0)))
    kern = functools.partial(_proj_kernel, rope=rope, scale=scale, want_f32=want_f32,
                             want_bf16=want_bf16, head_major=head_major, head_w=head_w)
    return pl.pallas_call(
        kern, out_shape=out_shape, grid=(m // tm,), in_specs=in_specs, out_specs=out_specs,
        compiler_params=_cparams(("parallel",)), name="in_proj",
    )(*args)


def _rope_tables(pos, scale):
    half = ROT_DIM // 2
    t = pos.shape[0]
    inv_freq = jnp.power(ROPE_THETA, -jnp.arange(half, dtype=F32) * (2.0 / ROT_DIM))
    ang = pos.astype(F32)[:, None] * inv_freq[None, :]
    cos, sin = jnp.cos(ang), jnp.sin(ang)
    zeros = lambda w: jnp.zeros((t, w), F32)
    c = jnp.concatenate([cos, cos, jnp.ones((t, HEAD_DIM - ROT_DIM), F32)], axis=1)
    sa = jnp.concatenate([-sin, zeros(HEAD_DIM - half)], axis=1)
    sb = jnp.concatenate([zeros(half), sin, zeros(HEAD_DIM - ROT_DIM)], axis=1)
    return c * scale, sa * scale, sb * scale


def _logf_kernel(x_ref, w_ref, b_ref, o_ref):
    z = _dot(x_ref[...], w_ref[...]) + b_ref[...]
    o_ref[...] = jnp.minimum(z, 0.0) - jnp.log(1.0 + jnp.exp(-jnp.abs(z)))


def _logf(xn, w_pad, b_pad):
    m, d = xn.shape
    tm = _tile(m, 512)
    return pl.pallas_call(
        _logf_kernel,
        out_shape=jax.ShapeDtypeStruct((m, LANES), F32),
        grid=(m // tm,),
        in_specs=[pl.BlockSpec((tm, d), lambda i: (i, 0)),
                  pl.BlockSpec((d, LANES), lambda i: (0, 0)),
                  pl.BlockSpec((1, LANES), lambda i: (0, 0))],
        out_specs=pl.BlockSpec((tm, LANES), lambda i: (i, 0)),
        compiler_params=_cparams(("parallel",)), name="forget_logits",
    )(xn, w_pad, b_pad)


def _cumsum_kernel(x_ref, o_ref, carry_ref):
    @pl.when(pl.program_id(0) == 0)
    def _():
        carry_ref[...] = jnp.zeros_like(carry_ref)

    tb = x_ref.shape[0]
    row = lax.broadcasted_iota(jnp.int32, (tb, tb), 0)
    col = lax.broadcasted_iota(jnp.int32, (tb, tb), 1)
    tri = jnp.where(col <= row, 1.0, 0.0).astype(F32)
    c = jnp.dot(tri, x_ref[...], preferred_element_type=F32,
                precision=lax.Precision.HIGHEST) + carry_ref[...]
    o_ref[...] = c
    carry_ref[...] = c[tb - 1:tb, :]


def _cumsum_rows(x, tb):
    t, c = x.shape
    assert t % tb == 0 and c % LANES == 0
    return pl.pallas_call(
        _cumsum_kernel,
        out_shape=jax.ShapeDtypeStruct((t, c), F32),
        grid=(t // tb,),
        in_specs=[pl.BlockSpec((tb, c), lambda i: (i, 0))],
        out_specs=pl.BlockSpec((tb, c), lambda i: (i, 0)),
        scratch_shapes=[pltpu.VMEM((1, c), F32)],
        compiler_params=_cparams(("arbitrary",)), name="cumsum_rows",
    )(x)


def _lambda_value(lq1, lk1, lq2, lk2, lam_init):
    a = jnp.sum(lq1[...] * lk1[...], axis=-1, keepdims=True)
    b = jnp.sum(lq2[...] * lk2[...], axis=-1, keepdims=True)
    return jnp.exp(a) - jnp.exp(b) + lam_init


def _head_rmsnorm(o, g):
    return o * lax.rsqrt(jnp.mean(o * o, axis=-1, keepdims=True) + RMS_EPS) * g


def _online_update(s, m, l, acc_ref, v):
    m_new = jnp.maximum(m, jnp.max(s, axis=-1, keepdims=True))
    alpha = jnp.exp2(m - m_new)
    p = jnp.exp2(s - m_new)
    l_new = alpha * l + jnp.sum(p, axis=-1, keepdims=True)
    acc_ref[...] = alpha * acc_ref[...] + _dot(p.astype(BF16), v)
    return m_new, l_new


def _causal_sweep(n_full, scores_fn, values_fn, mask_fn, s_a, s_b, m_ref, l_ref, acc_refs):
    n_maps = len(acc_refs)
    for n in range(n_maps):
        m_ref[n] = jnp.full(m_ref.shape[1:], NEG_INF, F32)
        l_ref[n] = jnp.zeros(l_ref.shape[1:], F32)
        acc_refs[n][...] = jnp.zeros_like(acc_refs[n])

    def produce(bufs, j):
        for ref, s in zip(bufs, scores_fn(j)):
            ref[...] = s

    def consume(tiles, j):
        v = values_fn(j)
        for n in range(n_maps):
            m_new, l_new = _online_update(tiles[n], m_ref[n], l_ref[n], acc_refs[n], v)
            m_ref[n] = m_new
            l_ref[n] = l_new

    def pair(p, carry):
        produce(s_b, 2 * p + 1)
        consume([r[...] for r in s_a], 2 * p)
        produce(s_a, 2 * p + 2)
        consume([r[...] for r in s_b], 2 * p + 1)
        return carry

    produce(s_a, 0)
    lax.fori_loop(0, n_full // 2, pair, 0)
    diag = [mask_fn(s) for s in scores_fn(n_full)]

    @pl.when(n_full % 2 == 1)
    def _():
        consume([r[...] for r in s_a], n_full - 1)

    consume(diag, n_full)


def _key_tile(ref, j, tk):
    return ref[0, pl.ds(pl.multiple_of(j * tk, tk), tk), :]


def _sweep_tiles(t):
    tq = _tile(t, 512)
    tk = 2 * tq if t % (2 * tq) == 0 else tq
    assert tq % (1 << CHUNK_SHIFT) == 0
    return tq, tk


def _diag_position(i, tq, tk):
    n_full = (i * tq) // tk
    return n_full, i * tq - n_full * tk


def _diff_prompt_kernel(q_ref, k_ref, v_ref, g_ref, lq1, lk1, lq2, lk2, o_ref,
                        acc1, acc2, a1, a2, b1, b2, m_ref, l_ref, *, tq, tk, lam_init):
    n_full, offset = _diag_position(pl.program_id(1), tq, tk)
    q = q_ref[0]
    q1, q2 = q[:, :HEAD_DIM], q[:, HEAD_DIM:]

    def scores(j):
        k = _key_tile(k_ref, j, tk)
        return _dot_nt(q1, k[:, :HEAD_DIM]), _dot_nt(q2, k[:, HEAD_DIM:])

    def mask(s):
        row = lax.broadcasted_iota(jnp.int32, (tq, tk), 0) + offset
        col = lax.broadcasted_iota(jnp.int32, (tq, tk), 1)
        return jnp.where((col >> CHUNK_SHIFT) <= (row >> CHUNK_SHIFT), s, NEG_INF)

    _causal_sweep(n_full, scores, lambda j: _key_tile(v_ref, j, tk), mask,
                  (a1, a2), (b1, b2), m_ref, l_ref, (acc1, acc2))
    lam = _lambda_value(lq1, lk1, lq2, lk2, lam_init)
    o = acc1[...] / l_ref[0] - lam * (acc2[...] / l_ref[1])
    o_ref[...] = (_head_rmsnorm(o, g_ref[...]) * (1.0 - lam_init)).astype(o_ref.dtype)


def _diff_prompt_attention(q, k, v, g, lams, lam_init):
    nh, t, w = q.shape
    tq, tk = _sweep_tiles(t)
    vec = pl.BlockSpec((1, HEAD_DIM), lambda h, i: (0, 0))
    kern = functools.partial(_diff_prompt_kernel, tq=tq, tk=tk, lam_init=lam_init)
    return pl.pallas_call(
        kern,
        out_shape=jax.ShapeDtypeStruct((t, nh * w), BF16),
        grid=(nh, t // tq),
        in_specs=[pl.BlockSpec((1, tq, w), lambda h, i: (h, i, 0)),
                  pl.BlockSpec((1, t, w), lambda h, i: (h, 0, 0)),
                  pl.BlockSpec((1, t, w), lambda h, i: (h, 0, 0)),
                  pl.BlockSpec((1, w), lambda h, i: (0, 0)),
                  vec, vec, vec, vec],
        out_specs=pl.BlockSpec((tq, w), lambda h, i: (i, h)),
        scratch_shapes=[pltpu.VMEM((tq, w), F32)] * 2 + [pltpu.VMEM((tq, tk), F32)] * 4
        + [pltpu.VMEM((2, tq, 1), F32)] * 2,
        compiler_params=_cparams(("parallel", "arbitrary")), name="diff_prompt_attention",
    )(q, k, v, g, *lams)


def _fox_prompt_kernel(q_ref, k_ref, v_ref, ck_ref, g_ref, o_ref, acc, s_a, s_b, m_ref, l_ref,
                       *, tq, tk):
    n_full, offset = _diag_position(pl.program_id(1), tq, tk)
    q = q_ref[0]

    def scores(j):
        return (_dot_nt(q, _key_tile(k_ref, j, tk)) - ck_ref[0, j] * LOG2E,)

    def mask(s):
        row = lax.broadcasted_iota(jnp.int32, (tq, tk), 0) + offset
        col = lax.broadcasted_iota(jnp.int32, (tq, tk), 1)
        return jnp.where(col <= row, s, NEG_INF)

    _causal_sweep(n_full, scores, lambda j: _key_tile(v_ref, j, tk), mask,
                  (s_a,), (s_b,), m_ref, l_ref, (acc,))
    o_ref[...] = _head_rmsnorm(acc[...] / l_ref[0], g_ref[...]).astype(o_ref.dtype)


def _fox_prompt_attention(q, k, v, ck, g):
    nh, t, w = q.shape
    tq, tk = _sweep_tiles(t)
    assert ck.shape == (nh, t // tk, 1, tk)
    kern = functools.partial(_fox_prompt_kernel, tq=tq, tk=tk)
    return pl.pallas_call(
        kern,
        out_shape=jax.ShapeDtypeStruct((t, nh * w), BF16),
        grid=(nh, t // tq),
        in_specs=[pl.BlockSpec((1, tq, w), lambda h, i: (h, i, 0)),
                  pl.BlockSpec((1, t, w), lambda h, i: (h, 0, 0)),
                  pl.BlockSpec((1, t, w), lambda h, i: (h, 0, 0)),
                  pl.BlockSpec((1, t // tk, 1, tk), lambda h, i: (h, 0, 0, 0)),
                  pl.BlockSpec((1, w), lambda h, i: (0, 0))],
        out_specs=pl.BlockSpec((tq, w), lambda h, i: (i, h)),
        scratch_shapes=[pltpu.VMEM((tq, w), F32)] + [pltpu.VMEM((tq, tk), F32)] * 2
        + [pltpu.VMEM((1, tq, 1), F32)] * 2,
        compiler_params=_cparams(("parallel", "arbitrary")), name="fox_prompt_attention",
    )(q, k, v, ck, g)


def _two_segment_softmax(sp, sn):
    m = jnp.maximum(jnp.max(sp, axis=-1, keepdims=True), jnp.max(sn, axis=-1, keepdims=True))
    pp = jnp.exp2(sp - m)
    pn = jnp.exp2(sn - m)
    l = jnp.sum(pp, axis=-1, keepdims=True) + jnp.sum(pn, axis=-1, keepdims=True)
    return pp.astype(BF16), pn.astype(BF16), l


def _cache_view(cache):
    b, t, nh, w = cache.shape
    nc = w // LANES
    return cache.reshape(b, t, nh, nc, LANES).transpose(0, 1, 3, 2, 4).reshape(b, t * nc * nh, LANES)


def _cache_rows(ref, head, lane_block, n_heads, n_lane_blocks, n_rows):
    stride = n_heads * n_lane_blocks
    return ref[0, pl.ds(lane_block * n_heads + head, n_rows, stride=stride), :].astype(BF16)


def _diff_sample_kernel(q_ref, kn_ref, vn_ref, kp_ref, vp_ref, g_ref, lq1, lk1, lq2, lk2,
                        o_ref, *, past_len, lam_init, n_heads):
    t = q_ref.shape[1]
    w = 2 * HEAD_DIM
    row = lax.broadcasted_iota(jnp.int32, (t, t), 0) + past_len
    col = lax.broadcasted_iota(jnp.int32, (t, t), 1) + past_len
    ok = (col >> CHUNK_SHIFT) <= (row >> CHUNK_SHIFT)
    lam = _lambda_value(lq1, lk1, lq2, lk2, lam_init)
    for h in range(n_heads):
        sl = slice(h * w, (h + 1) * w)
        q = q_ref[0, :, sl]
        kn = kn_ref[0, :, sl]
        vn = vn_ref[0, :, sl]
        vp = [_cache_rows(vp_ref, h, c, n_heads, 2, past_len) for c in range(2)]
        outs = []
        for half in range(2):
            hs = slice(half * HEAD_DIM, (half + 1) * HEAD_DIM)
            sp = _dot_nt(q[:, hs], _cache_rows(kp_ref, h, half, n_heads, 2, past_len))
            sn = jnp.where(ok, _dot_nt(q[:, hs], kn[:, hs]), NEG_INF)
            pp, pn, l = _two_segment_softmax(sp, sn)
            o_past = jnp.concatenate([_dot(pp, vp[0]), _dot(pp, vp[1])], axis=1)
            outs.append((o_past + _dot(pn, vn)) / l)
        o = outs[0] - lam * outs[1]
        o_ref[0, :, sl] = (_head_rmsnorm(o, g_ref[...]) * (1.0 - lam_init)).astype(o_ref.dtype)


def _diff_sample_attention(q, kn, vn, kp, vp, g, lams, lam_init):
    b, t, n = q.shape
    past = kp.shape[1]
    kp, vp = _cache_view(kp), _cache_view(vp)
    new = pl.BlockSpec((1, t, n), lambda i: (i, 0, 0))
    old = pl.BlockSpec((1,) + kp.shape[1:], lambda i: (i, 0, 0))
    vec = pl.BlockSpec((1, HEAD_DIM), lambda i: (0, 0))
    kern = functools.partial(_diff_sample_kernel, past_len=past, lam_init=lam_init,
                             n_heads=n // (2 * HEAD_DIM))
    return pl.pallas_call(
        kern,
        out_shape=jax.ShapeDtypeStruct((b, t, n), BF16),
        grid=(b,),
        in_specs=[new, new, new, old, old,
                  pl.BlockSpec((1, 2 * HEAD_DIM), lambda i: (0, 0)), vec, vec, vec, vec],
        out_specs=new,
        compiler_params=_cparams(("parallel",)), name="diff_sample_attention",
    )(q, kn, vn, kp, vp, g, *lams)


def _fox_sample_kernel(q_ref, kn_ref, vn_ref, kp_ref, vp_ref, ckp_ref, ckn_ref, g_ref, o_ref,
                       *, n_heads):
    t = q_ref.shape[1]
    row = lax.broadcasted_iota(jnp.int32, (t, t), 0)
    col = lax.broadcasted_iota(jnp.int32, (t, t), 1)
    ok = col <= row
    for h in range(n_heads):
        sl = slice(h * HEAD_DIM, (h + 1) * HEAD_DIM)
        q = q_ref[0, :, sl]
        past = ckp_ref.shape[-1]
        kp = _cache_rows(kp_ref, h, 0, n_heads, 1, past)
        vp = _cache_rows(vp_ref, h, 0, n_heads, 1, past)
        sp = _dot_nt(q, kp) - ckp_ref[0, h] * LOG2E
        sn = jnp.where(ok, _dot_nt(q, kn_ref[0, :, sl]) - ckn_ref[0, h] * LOG2E, NEG_INF)
        pp, pn, l = _two_segment_softmax(sp, sn)
        o = (_dot(pp, vp) + _dot(pn, vn_ref[0, :, sl])) / l
        o_ref[0, :, sl] = _head_rmsnorm(o, g_ref[...]).astype(o_ref.dtype)


def _fox_sample_attention(q, kn, vn, kp, vp, ckp, ckn, g):
    b, t, n = q.shape
    past = kp.shape[1]
    nh = n // HEAD_DIM
    kp, vp = _cache_view(kp), _cache_view(vp)
    new = pl.BlockSpec((1, t, n), lambda i: (i, 0, 0))
    old = pl.BlockSpec((1,) + kp.shape[1:], lambda i: (i, 0, 0))
    kern = functools.partial(_fox_sample_kernel, n_heads=nh)
    return pl.pallas_call(
        kern,
        out_shape=jax.ShapeDtypeStruct((b, t, n), BF16),
        grid=(b,),
        in_specs=[new, new, new, old, old,
                  pl.BlockSpec((1, nh, 1, past), lambda i: (i, 0, 0, 0)),
                  pl.BlockSpec((1, nh, 1, t), lambda i: (i, 0, 0, 0)),
                  pl.BlockSpec((1, HEAD_DIM), lambda i: (0, 0))],
        out_specs=new,
        compiler_params=_cparams(("parallel",)), name="fox_sample_attention",
    )(q, kn, vn, kp, vp, ckp, ckn, g)


def _split_bf16(a):
    hi = a.astype(BF16)
    return hi, (a - hi.astype(F32)).astype(BF16)


def _outproj_router_kernel(md_ref, mf_ref, wd_ref, wf_ref, x_ref, g_ref, wr_ref, br_ref,
                           h_ref, hn_ref, idx_ref, gate_ref):
    h = x_ref[...] + _dot(md_ref[...], wd_ref[...]) + _dot(mf_ref[...], wf_ref[...])
    h_ref[...] = h
    hn = h * lax.rsqrt(jnp.mean(h * h, axis=-1, keepdims=True) + RMS_EPS) * g_ref[...]
    for c in range(hn_ref.shape[1]):
        hn_ref[:, c, :] = hn[:, c * LANES:(c + 1) * LANES]
    hn_hi, hn_lo = _split_bf16(hn)
    wr_hi, wr_lo = _split_bf16(wr_ref[...])
    logits = _dot(hn_hi, wr_hi) + (_dot(hn_lo, wr_hi) + _dot(hn_hi, wr_lo)) + br_ref[...]
    lane = lax.broadcasted_iota(jnp.int32, logits.shape, 1)
    work = logits
    idx_out = jnp.zeros(logits.shape, jnp.int32)
    val_out = jnp.zeros(logits.shape, F32)
    top = None
    denom = None
    for k in range(TOP_K):
        mx = jnp.max(work, axis=-1, keepdims=True)
        first = jnp.min(jnp.where(work == mx, lane, LANES), axis=-1, keepdims=True)
        work = jnp.where(lane == first, -jnp.inf, work)
        if k == 0:
            top = mx
        e = jnp.exp(mx - top)
        denom = e if k == 0 else denom + e
        idx_out = jnp.where(lane == k, first, idx_out)
        val_out = jnp.where(lane == k, e, val_out)
    idx_ref[...] = idx_out
    gate_ref[...] = val_out / denom


def _outproj_router(mix_d, mix_f, w_d, w_f, x, g, w_router_pad, b_router_pad):
    m, d = x.shape
    kd, kf = mix_d.shape[1], mix_f.shape[1]
    tm = _tile(m, 256)
    row = lambda w: pl.BlockSpec((tm, w), lambda i: (i, 0))
    full = lambda r, c: pl.BlockSpec((r, c), lambda i: (0, 0))
    return pl.pallas_call(
        _outproj_router_kernel,
        out_shape=[jax.ShapeDtypeStruct((m, d), F32),
                   jax.ShapeDtypeStruct((m, d // LANES, LANES), F32),
                   jax.ShapeDtypeStruct((m, LANES), jnp.int32),
                   jax.ShapeDtypeStruct((m, LANES), F32)],
        grid=(m // tm,),
        in_specs=[row(kd), row(kf), full(kd, d), full(kf, d), row(d), full(1, d),
                  full(d, LANES), full(1, LANES)],
        out_specs=[row(d), pl.BlockSpec((tm, d // LANES, LANES), lambda i: (i, 0, 0)),
                   row(LANES), row(LANES)],
        compiler_params=_cparams(("parallel",)), name="outproj_router",
    )(mix_d, mix_f, w_d, w_f, x, g, w_router_pad, b_router_pad)


def _rank_kernel(idx_ref, rank_ref, count_ref, carry_ref):
    @pl.when(pl.program_id(0) == 0)
    def _():
        carry_ref[...] = jnp.zeros_like(carry_ref)

    tm = idx_ref.shape[0]
    idx = idx_ref[...]
    lane = lax.broadcasted_iota(jnp.int32, (tm, LANES), 1)
    picks = [lane == idx[:, k:k + 1] for k in range(TOP_K)]
    onehot = jnp.zeros((tm, LANES), F32)
    for p in picks:
        onehot = onehot + jnp.where(p, 1.0, 0.0)
    row = lax.broadcasted_iota(jnp.int32, (tm, tm), 0)
    col = lax.broadcasted_iota(jnp.int32, (tm, tm), 1)
    strict = jnp.where(col < row, 1.0, 0.0).astype(BF16)
    before = _dot(strict, onehot.astype(BF16)) + carry_ref[...]
    rank = jnp.zeros((tm, LANES), F32)
    for k, p in enumerate(picks):
        r = jnp.sum(jnp.where(p, before, 0.0), axis=-1, keepdims=True)
        rank = jnp.where(lane == k, r, rank)
    rank_ref[...] = rank.astype(jnp.int32)
    total = before[tm - 1:tm, :] + onehot[tm - 1:tm, :]
    carry_ref[...] = total
    count_ref[...] = total.astype(jnp.int32)


def _expert_ranks(idx):
    n_tok = idx.shape[0]
    tm = _tile(n_tok, 512)
    return pl.pallas_call(
        _rank_kernel,
        out_shape=[jax.ShapeDtypeStruct((n_tok, LANES), jnp.int32),
                   jax.ShapeDtypeStruct((1, LANES), jnp.int32)],
        grid=(n_tok // tm,),
        in_specs=[pl.BlockSpec((tm, LANES), lambda i: (i, 0))],
        out_specs=[pl.BlockSpec((tm, LANES), lambda i: (i, 0)),
                   pl.BlockSpec((1, LANES), lambda i: (0, 0))],
        scratch_shapes=[pltpu.VMEM((1, LANES), F32)],
        compiler_params=_cparams(("arbitrary",)), name="expert_ranks",
    )(idx)


def _row_copy(src_ref, dst_ref, sem, src_row, dst_row):
    return pltpu.make_async_copy(src_ref.at[pl.ds(src_row, 1)], dst_ref.at[pl.ds(dst_row, 1)], sem)


def _gather_rows(idx_ref, src_ref, dst_ref, sem, n_rows):
    assert n_rows % GATHER_UNROLL == 0

    def start(g, carry):
        base = pl.multiple_of(g * GATHER_UNROLL, GATHER_UNROLL)
        for u in range(GATHER_UNROLL):
            _row_copy(src_ref, dst_ref, sem, idx_ref[0, 0, base + u], base + u).start()
        return carry

    def wait(g, carry):
        base = pl.multiple_of(g * GATHER_UNROLL, GATHER_UNROLL)
        for u in range(GATHER_UNROLL):
            _row_copy(src_ref, dst_ref, sem, 0, base + u).wait()
        return carry

    lax.fori_loop(0, n_rows // GATHER_UNROLL, start, 0)
    lax.fori_loop(0, n_rows // GATHER_UNROLL, wait, 0)


def _dispatch_kernel(pad_start_ref, pad_count_ref, dest_ref, hp_ref, hs_ref, xs_ref, zero_buf, sem,
                     *, ts, n_prompt_steps, n_pad):
    i = pl.program_id(0)

    @pl.when(i == 0)
    def _():
        zero_buf[...] = jnp.zeros_like(zero_buf)

        def fill(e, carry):
            def start(r, c):
                pltpu.make_async_copy(zero_buf, xs_ref.at[pad_start_ref[e] + r], sem).start()
                return c

            def wait(r, c):
                pltpu.make_async_copy(zero_buf, xs_ref.at[0], sem).wait()
                return c

            lax.fori_loop(0, pad_count_ref[e], start, 0)
            lax.fori_loop(0, pad_count_ref[e], wait, 0)
            return carry

        lax.fori_loop(0, n_pad, fill, 0)

    def copy_tokens(src_ref, first_tok):
        per_trip = GATHER_UNROLL // TOP_K

        def start(g, c):
            for u in range(per_trip):
                tok = g * per_trip + u
                for k in range(TOP_K):
                    pltpu.make_async_copy(src_ref.at[first_tok + tok],
                                          xs_ref.at[dest_ref[0, 0, tok * TOP_K + k]], sem).start()
            return c

        def wait(g, c):
            for _ in range(GATHER_UNROLL):
                pltpu.make_async_copy(src_ref.at[0], xs_ref.at[0], sem).wait()
            return c

        lax.fori_loop(0, ts // per_trip, start, 0)
        lax.fori_loop(0, ts // per_trip, wait, 0)

    @pl.when(i < n_prompt_steps)
    def _():
        copy_tokens(hp_ref, i * ts)

    @pl.when(i >= n_prompt_steps)
    def _():
        copy_tokens(hs_ref, (i - n_prompt_steps) * ts)


def _dispatch(pad_start, pad_count, dest, hn_prompt, hn_sample, n_rows):
    n_p, n_s = hn_prompt.shape[0], hn_sample.shape[0]
    ts = _tile(math.gcd(n_p, n_s), 512)
    assert ts % (GATHER_UNROLL // TOP_K) == 0
    n_steps = (n_p + n_s) // ts
    kern = functools.partial(_dispatch_kernel, ts=ts, n_prompt_steps=n_p // ts,
                             n_pad=pad_start.shape[0])
    anyspec = pl.BlockSpec(memory_space=pl.ANY)
    return pl.pallas_call(
        kern,
        out_shape=jax.ShapeDtypeStruct((n_rows,) + hn_prompt.shape[1:], hn_prompt.dtype),
        grid_spec=pltpu.PrefetchScalarGridSpec(
            num_scalar_prefetch=2,
            grid=(n_steps,),
            in_specs=[pl.BlockSpec((1, 1, ts * TOP_K), lambda i, ps, pc: (i, 0, 0),
                                   memory_space=pltpu.SMEM), anyspec, anyspec],
            out_specs=anyspec,
            scratch_shapes=[pltpu.VMEM(hn_prompt.shape[1:], hn_prompt.dtype),
                            pltpu.SemaphoreType.DMA]),
        compiler_params=_cparams(("arbitrary",)), name="moe_dispatch",
    )(pad_start, pad_count, dest.reshape(n_steps, 1, ts * TOP_K), hn_prompt, hn_sample)


def _pack_rows_kernel(x_ref, o_ref):
    rows, d = o_ref.shape
    n_chunks = d // LANES
    for c in range(n_chunks):
        o_ref[:, c * LANES:(c + 1) * LANES] = x_ref[pl.ds(c, rows, stride=n_chunks), :].astype(BF16)


def _pack_rows(xs_slabs):
    n_rows, n_chunks, _ = xs_slabs.shape
    rows = _tile(n_rows, DISPATCH_ROWS)
    return pl.pallas_call(
        _pack_rows_kernel,
        out_shape=jax.ShapeDtypeStruct((n_rows, n_chunks * LANES), BF16),
        grid=(n_rows // rows,),
        in_specs=[pl.BlockSpec((rows * n_chunks, LANES), lambda b: (b, 0))],
        out_specs=pl.BlockSpec((rows, n_chunks * LANES), lambda b: (b, 0)),
        compiler_params=_cparams(("parallel",)), name="moe_pack_rows",
    )(xs_slabs.reshape(n_rows * n_chunks, LANES))


def _combine_kernel(dest_ref, y_ref, h_ref, gate_ref, g_ref, o_ref, buf, sem, *, tm):
    _gather_rows(dest_ref, y_ref, buf, sem, TOP_K * tm)
    gates = gate_ref[...]
    out = h_ref[...]
    for k in range(TOP_K):
        out = out + buf[k * tm:(k + 1) * tm, :] * gates[:, k:k + 1]
    out = out * lax.rsqrt(jnp.mean(out * out, axis=-1, keepdims=True) + RMS_EPS)
    o_ref[...] = out * g_ref[...]


def _combine(dest, y_rows, h, gates, g):
    n_tiles, _, per = dest.shape
    tm = per // TOP_K
    m, d = h.shape
    kern = functools.partial(_combine_kernel, tm=tm)
    return pl.pallas_call(
        kern,
        out_shape=jax.ShapeDtypeStruct((m, d), F32),
        grid=(n_tiles,),
        in_specs=[pl.BlockSpec((1, 1, per), lambda i: (i, 0, 0), memory_space=pltpu.SMEM),
                  pl.BlockSpec(memory_space=pl.ANY),
                  pl.BlockSpec((tm, d), lambda i: (i, 0)),
                  pl.BlockSpec((tm, LANES), lambda i: (i, 0)),
                  pl.BlockSpec((1, d), lambda i: (0, 0))],
        out_specs=pl.BlockSpec((tm, d), lambda i: (i, 0)),
        scratch_shapes=[pltpu.VMEM((per, d), y_rows.dtype), pltpu.SemaphoreType.DMA],
        compiler_params=_cparams(("arbitrary",)), name="moe_combine",
    )(dest, y_rows, h, gates, g)


def _expert_changed(be_ref, rb):
    prev = be_ref[jnp.maximum(rb - 1, 0)]
    return jnp.logical_or(rb == 0, be_ref[rb] != prev)


def _moe_up_kernel(be_ref, used_ref, x_ref, wg_ref, wu_ref, bg_ref, bu_ref, o_ref, wg16, wu16):
    rb = pl.program_id(1)

    @pl.when(_expert_changed(be_ref, rb))
    def _():
        wg16[...] = wg_ref[0].astype(BF16)
        wu16[...] = wu_ref[0].astype(BF16)

    @pl.when(rb < used_ref[0])
    def _():
        x = x_ref[...]
        gate = jnp.minimum(_dot(x, wg16[...]) + bg_ref[0], SWIGLU_LIMIT)
        up = jnp.clip(_dot(x, wu16[...]) + bu_ref[0], -SWIGLU_LIMIT, SWIGLU_LIMIT)
        act = (up + 1.0) * (gate * jax.nn.sigmoid(SWIGLU_ALPHA * gate))
        o_ref[...] = act.astype(o_ref.dtype)

    @pl.when(rb >= used_ref[0])
    def _():
        o_ref[...] = jnp.zeros_like(o_ref)


def _moe_up(block_expert, n_used, xs, w_gate, w_up, b_gate, b_up, tn):
    n_rows, d = xs.shape
    n_exp, _, d_ff = w_gate.shape
    rows = MOE_ROWS
    n_blocks = n_rows // rows
    wspec = pl.BlockSpec((1, d, tn), lambda n, rb, be, used: (be[rb], 0, n))
    bspec = pl.BlockSpec((1, 1, tn), lambda n, rb, be, used: (be[rb], 0, n))
    return pl.pallas_call(
        _moe_up_kernel,
        out_shape=jax.ShapeDtypeStruct((n_rows, d_ff), BF16),
        grid_spec=pltpu.PrefetchScalarGridSpec(
            num_scalar_prefetch=2,
            grid=(d_ff // tn, n_blocks),
            in_specs=[pl.BlockSpec((rows, d), lambda n, rb, be, used: (rb, 0)),
                      wspec, wspec, bspec, bspec],
            out_specs=pl.BlockSpec((rows, tn), lambda n, rb, be, used: (rb, n)),
            scratch_shapes=[pltpu.VMEM((d, tn), BF16), pltpu.VMEM((d, tn), BF16)]),
        compiler_params=_cparams(("arbitrary", "arbitrary")), name="moe_gate_up",
    )(block_expert, n_used, xs, w_gate, w_up, b_gate.reshape(n_exp, 1, d_ff),
      b_up.reshape(n_exp, 1, d_ff))


def _moe_down_kernel(be_ref, used_ref, a_ref, w_ref, b_ref, o_ref, w16):
    rb = pl.program_id(1)

    @pl.when(_expert_changed(be_ref, rb))
    def _():
        w16[...] = w_ref[0].astype(BF16)

    @pl.when(rb < used_ref[0])
    def _():
        o_ref[...] = _dot(a_ref[...], w16[...]) + b_ref[0]

    @pl.when(rb >= used_ref[0])
    def _():
        o_ref[...] = jnp.zeros_like(o_ref)


def _moe_down(block_expert, n_used, act, w_down, b_down, tn):
    n_rows, d_ff = act.shape
    n_exp, _, d = w_down.shape
    rows = MOE_ROWS
    return pl.pallas_call(
        _moe_down_kernel,
        out_shape=jax.ShapeDtypeStruct((n_rows, d), F32),
        grid_spec=pltpu.PrefetchScalarGridSpec(
            num_scalar_prefetch=2,
            grid=(d // tn, n_rows // rows),
            in_specs=[pl.BlockSpec((rows, d_ff), lambda n, rb, be, used: (rb, 0)),
                      pl.BlockSpec((1, d_ff, tn), lambda n, rb, be, used: (be[rb], 0, n)),
                      pl.BlockSpec((1, 1, tn), lambda n, rb, be, used: (be[rb], 0, n))],
            out_specs=pl.BlockSpec((rows, tn), lambda n, rb, be, used: (rb, n)),
            scratch_shapes=[pltpu.VMEM((d_ff, tn), BF16)]),
        compiler_params=_cparams(("arbitrary", "arbitrary")), name="moe_down",
    )(block_expert, n_used, act, w_down, b_down.reshape(n_exp, 1, d))


def _pad_lanes(a, fill=0.0):
    return jnp.pad(a, ((0, 0), (0, LANES - a.shape[1])), constant_values=fill)


def _mixer_projections(x2d, pos, norm_g, w_in, b_forget, head_major):
    d = x2d.shape[1]
    dw = (w_in.shape[1] - 8) // 6
    q_scale = HEAD_DIM ** -0.5 * LOG2E
    xn = _rmsnorm_bf16(x2d, norm_g)
    w16 = w_in.astype(BF16)
    wcol = lambda j: w16[:, j * dw:(j + 1) * dw]
    tabs_q = _rope_tables(pos, q_scale)
    tabs_k = _rope_tables(pos, 1.0)
    hm = dict(head_major=head_major)
    (dq,) = _proj(xn, wcol(0), rope_tabs=tabs_q, head_w=2 * HEAD_DIM, **hm)
    dk32, dk = _proj(xn, wcol(1), rope_tabs=tabs_k, want_f32=True, head_w=2 * HEAD_DIM, **hm)
    dv32, dv = _proj(xn, wcol(2), want_f32=True, head_w=2 * HEAD_DIM, **hm)
    (fq,) = _proj(xn, wcol(3), scale=q_scale, **hm)
    fk32, fk = _proj(xn, wcol(4), want_f32=True, **hm)
    fv32, fv = _proj(xn, wcol(5), want_f32=True, **hm)
    logf = _logf(xn, _pad_lanes(w16[:, 6 * dw:]), _pad_lanes(b_forget.reshape(1, -1)))
    return (dq, dk, dv, fq, fk, fv), (dk32, dv32, fk32, fv32), logf


def _moe(hn_prompt, hn_sample, idx_all, w_gate, w_up, w_down, b_gate, b_up, b_down):
    n_tok = idx_all.shape[0]
    n_exp = w_gate.shape[0]
    rows = MOE_ROWS
    n_blocks = -(-(n_tok * TOP_K) // rows) + n_exp
    n_rows = n_blocks * rows

    rank, counts = _expert_ranks(idx_all)
    experts = idx_all[:, :TOP_K]
    counts = counts[0, :n_exp]
    padded = (counts + rows - 1) // rows * rows
    padded_end = jnp.cumsum(padded)
    padded_start = padded_end - padded
    dest = padded_start[experts] + rank[:, :TOP_K]
    pad_start = jnp.concatenate([padded_start + counts, padded_end[-1:]]).astype(jnp.int32)
    pad_count = jnp.concatenate([padded - counts, n_rows - padded_end[-1:]]).astype(jnp.int32)
    block_start = jnp.arange(n_blocks, dtype=jnp.int32) * rows
    block_expert = jnp.minimum(
        jnp.sum((padded_end[None, :] <= block_start[:, None]).astype(jnp.int32), axis=1), n_exp - 1)
    n_used = (padded_end[-1] // rows).astype(jnp.int32).reshape(1)

    xs = _pack_rows(_dispatch(pad_start, pad_count, dest, hn_prompt, hn_sample, n_rows))
    act = _moe_up(block_expert, n_used, xs, w_gate, w_up, b_gate, b_up, tn=512)
    y_rows = _moe_down(block_expert, n_used, act, w_down, b_down, tn=min(1024, w_down.shape[2]))
    return y_rows, dest


def _combine_tokens(dest, y_rows, h, gates, g):
    m = h.shape[0]
    tm = _tile(m, 256)
    dest_t = dest.reshape(m // tm, tm, TOP_K).transpose(0, 2, 1).reshape(m // tm, 1, TOP_K * tm)
    return _combine(dest_t, y_rows, h, gates, g.reshape(1, -1))


def kernel(x_prompt, x_sample, cache_diff_k, cache_diff_v, cache_fox_k, cache_fox_v, cache_fox_logf,
           attn_norm_g, w_in, b_forget, lambda_q1, lambda_k1, lambda_q2, lambda_k2,
           diff_out_norm_g, fox_out_norm_g, w_out, ffn_norm_g, w_router, b_router,
           w_gate, b_gate, w_up, b_up, w_down, b_down, final_norm_g):
    depth = w_in.shape[0]
    assert depth == 1, "single-layer stack"
    lam_init = 0.8 - 0.6 * math.exp(-0.3 * 0)
    drop = lambda a: a.reshape(a.shape[1:])
    (cache_diff_k, cache_diff_v, cache_fox_k, cache_fox_v, cache_fox_logf, attn_norm_g, w_in,
     b_forget, lambda_q1, lambda_k1, lambda_q2, lambda_k2, diff_out_norm_g, fox_out_norm_g, w_out,
     ffn_norm_g, w_router, b_router, w_gate, b_gate, w_up, b_up, w_down, b_down) = map(drop, (
         cache_diff_k, cache_diff_v, cache_fox_k, cache_fox_v, cache_fox_logf, attn_norm_g, w_in,
         b_forget, lambda_q1, lambda_k1, lambda_q2, lambda_k2, diff_out_norm_g, fox_out_norm_g,
         w_out, ffn_norm_g, w_router, b_router, w_gate, b_gate, w_up, b_up, w_down, b_down))
    bp, t, d = x_prompt.shape
    assert bp == 1
    bs, ts, _ = x_sample.shape
    past = cache_diff_k.shape[1]
    n_dh = cache_diff_k.shape[2]
    n_fh = cache_fox_k.shape[2]
    dw = n_dh * 2 * HEAD_DIM
    fw = n_fh * HEAD_DIM
    lams = [v.reshape(1, HEAD_DIM) for v in (lambda_q1, lambda_k1, lambda_q2, lambda_k2)]
    diff_g = diff_out_norm_g.reshape(1, -1)
    fox_g = fox_out_norm_g.reshape(1, -1)

    xp = x_prompt.reshape(t, d)
    pos_p = jnp.arange(t, dtype=jnp.int32)
    (dq, dk, dv, fq, fk, fv), cache_p, logf_p = _mixer_projections(
        xp, pos_p, attn_norm_g, w_in, b_forget, head_major=True)
    _, tk = _sweep_tiles(t)
    c_p = _cumsum_rows(logf_p, _tile(t, 512))
    ck_p = c_p[:, :n_fh].T.reshape(n_fh, t // tk, 1, tk)
    mix_d_p = _diff_prompt_attention(dq, dk, dv, diff_g, lams, lam_init)
    mix_f_p = _fox_prompt_attention(fq, fk, fv, ck_p, fox_g)

    xs = x_sample.reshape(bs * ts, d)
    pos_s = past + jnp.arange(ts, dtype=jnp.int32)
    (sdq, sdk, sdv, sfq, sfk, sfv), cache_s, logf_s = _mixer_projections(
        xs, jnp.tile(pos_s, bs), attn_norm_g, w_in, b_forget, head_major=False)
    logf_all = jnp.concatenate([cache_fox_logf.astype(F32),
                                logf_s[:, :n_fh].reshape(bs, ts, n_fh)], axis=1)
    tt = past + ts
    lf_cols = logf_all.transpose(1, 0, 2).reshape(tt, bs * n_fh)
    pad_c = (-lf_cols.shape[1]) % LANES
    lf_cols = jnp.pad(lf_cols, ((0, 0), (0, pad_c)))
    tb = max(tb_ for tb_ in range(8, 513, 8) if tt % tb_ == 0)
    c_s = _cumsum_rows(lf_cols, tb)[:, :bs * n_fh].reshape(tt, bs, n_fh).transpose(1, 2, 0)
    ckp_s = c_s[:, :, None, :past]
    ckn_s = c_s[:, :, None, past:]
    r3 = lambda a: a.reshape(bs, ts, -1)
    mix_d_s = _diff_sample_attention(
        r3(sdq), r3(sdk), r3(sdv), cache_diff_k, cache_diff_v,
        diff_g, lams, lam_init).reshape(bs * ts, dw)
    mix_f_s = _fox_sample_attention(
        r3(sfq), r3(sfk), r3(sfv), cache_fox_k, cache_fox_v,
        ckp_s, ckn_s, fox_g).reshape(bs * ts, fw)

    w_out16 = w_out.astype(BF16)
    w_router_pad = _pad_lanes(w_router)
    b_router_pad = _pad_lanes(b_router.reshape(1, -1), fill=NEG_INF)
    ffn_g = ffn_norm_g.reshape(1, -1)
    h_p, hn_p, idx_p, gate_p = _outproj_router(mix_d_p, mix_f_p, w_out16[:dw], w_out16[dw:],
                                               xp, ffn_g, w_router_pad, b_router_pad)
    h_s, hn_s, idx_s, gate_s = _outproj_router(mix_d_s, mix_f_s, w_out16[:dw], w_out16[dw:],
                                               xs, ffn_g, w_router_pad, b_router_pad)

    idx_all = jnp.concatenate([idx_p, idx_s], axis=0)
    y_rows, dest = _moe(hn_p, hn_s, idx_all, w_gate, w_up, w_down, b_gate, b_up, b_down)
    y_p = _combine_tokens(dest[:t], y_rows, h_p, gate_p, final_norm_g)
    y_s = _combine_tokens(dest[t:], y_rows, h_s, gate_s, final_norm_g)

    dk32, dv32, fk32, fv32 = cache_p
    sdk32, sdv32, sfk32, sfv32 = cache_s
    return (y_p.reshape(1, t, d), y_s.reshape(bs, ts, d),
            dk32.reshape(1, 1, t, n_dh, 2 * HEAD_DIM), dv32.reshape(1, 1, t, n_dh, 2 * HEAD_DIM),
            fk32.reshape(1, 1, t, n_fh, HEAD_DIM), fv32.reshape(1, 1, t, n_fh, HEAD_DIM),
            logf_p[:, :n_fh].reshape(1, 1, t, n_fh),
            sdk32.reshape(1, bs, ts, n_dh, 2 * HEAD_DIM), sdv32.reshape(1, bs, ts, n_dh, 2 * HEAD_DIM),
            sfk32.reshape(1, bs, ts, n_fh, HEAD_DIM), sfv32.reshape(1, bs, ts, n_fh, HEAD_DIM),
            logf_s[:, :n_fh].reshape(1, bs, ts, n_fh))
```

```python
import functools
import math

import jax
import jax.numpy as jnp
from jax import lax
from jax.experimental import pallas as pl
from jax.experimental.pallas import tpu as pltpu

HEAD_DIM = 128
CHUNK_SHIFT = 6
ROT_DIM = HEAD_DIM // 4
ROPE_THETA = 500000.0
TOP_K = 4
SWIGLU_LIMIT = 7.0
SWIGLU_ALPHA = 1.702
RMS_EPS = 1e-6
NEG_INF = -1e30
LOG2E = 1.4426950408889634
LANES = 128
MOE_ROWS = 256
GATHER_UNROLL = 8
DISPATCH_ROWS = 1024
VMEM_LIMIT = 56 * 1024 * 1024

F32 = jnp.float32
BF16 = jnp.bfloat16


def _cparams(sem, vmem=VMEM_LIMIT):
    return pltpu.CompilerParams(dimension_semantics=sem, vmem_limit_bytes=vmem)


def _tile(n, pref):
    if n <= pref:
        return n
    t = max(c for c in range(8, pref + 1, 8) if n % c == 0)
    return t


def _dot_nt(a, b):
    return lax.dot_general(a, b, (((1,), (1,)), ((), ())), preferred_element_type=F32)


def _dot(a, b):
    return jnp.dot(a, b, preferred_element_type=F32)


def _rmsnorm_kernel(x_ref, g_ref, o_ref):
    x = x_ref[...]
    y = x * lax.rsqrt(jnp.mean(x * x, axis=-1, keepdims=True) + RMS_EPS)
    o_ref[...] = (y * g_ref[...]).astype(o_ref.dtype)


def _rmsnorm_bf16(x, g):
    m, d = x.shape
    tm = _tile(m, 512)
    return pl.pallas_call(
        _rmsnorm_kernel,
        out_shape=jax.ShapeDtypeStruct((m, d), BF16),
        grid=(m // tm,),
        in_specs=[pl.BlockSpec((tm, d), lambda i: (i, 0)),
                  pl.BlockSpec((1, d), lambda i: (0, 0))],
        out_specs=pl.BlockSpec((tm, d), lambda i: (i, 0)),
        compiler_params=_cparams(("parallel",)),
        name="rmsnorm_bf16",
    )(x, g.reshape(1, d))


def _proj_kernel(*refs, rope, scale, want_f32, want_bf16, head_major, head_w):
    it = iter(refs)
    x_ref = next(it)
    w_ref = next(it)
    if rope:
        c_ref, sa_ref, sb_ref = next(it), next(it), next(it)
    o32_ref = next(it) if want_f32 else None
    o16_ref = next(it) if want_bf16 else None

    n = w_ref.shape[1]
    y = _dot(x_ref[...], w_ref[...])
    for h in range(n // HEAD_DIM):
        sl = slice(h * HEAD_DIM, (h + 1) * HEAD_DIM)
        yh = y[:, sl]
        if rope:
            yh = (yh * c_ref[...]
                  + pltpu.roll(yh, HEAD_DIM - ROT_DIM // 2, 1) * sa_ref[...]
                  + pltpu.roll(yh, ROT_DIM // 2, 1) * sb_ref[...])
        elif scale != 1.0:
            yh = yh * scale
        per = head_w // HEAD_DIM
        if want_f32:
            o32_ref[:, h // per, (h % per) * HEAD_DIM:(h % per + 1) * HEAD_DIM] = yh
        if want_bf16:
            if head_major:
                o16_ref[h // per, :, (h % per) * HEAD_DIM:(h % per + 1) * HEAD_DIM] = yh.astype(BF16)
            else:
                o16_ref[:, sl] = yh.astype(BF16)


def _proj(xn, w, rope_tabs=None, scale=1.0, want_f32=False, want_bf16=True,
          head_major=False, head_w=HEAD_DIM):
    m, d = xn.shape
    n = w.shape[1]
    tm = _tile(m, 512)
    rope = rope_tabs is not None
    in_specs = [pl.BlockSpec((tm, d), lambda i: (i, 0)),
                pl.BlockSpec((d, n), lambda i: (0, 0))]
    args = [xn, w]
    if rope:
        in_specs += [pl.BlockSpec((tm, HEAD_DIM), lambda i: (i, 0))] * 3
        args += list(rope_tabs)
    out_shape, out_specs = [], []
    if want_f32:
        out_shape.append(jax.ShapeDtypeStruct((m, n // head_w, head_w), F32))
        out_specs.append(pl.BlockSpec((tm, n // head_w, head_w), lambda i: (i, 0, 0)))
    if want_bf16:
        if head_major:
            nh = n // head_w
            out_shape.append(jax.ShapeDtypeStruct((nh, m, head_w), BF16))
            out_specs.append(pl.BlockSpec((nh, tm, head_w), lambda i: (0, i, 0)))
        else:
            out_shape.append(jax.ShapeDtypeStruct((m, n), BF16))
            out_specs.append(pl.BlockSpec((tm, n), lambda i: (i, 0)))
    kern = functools.partial(_proj_kernel, rope=rope, scale=scale, want_f32=want_f32,
                             want_bf16=want_bf16, head_major=head_major, head_w=head_w)
    return pl.pallas_call(
        kern, out_shape=out_shape, grid=(m // tm,), in_specs=in_specs, out_specs=out_specs,
        compiler_params=_cparams(("parallel",)), name="in_proj",
    )(*args)


def _rope_tables(pos, scale):
    half = ROT_DIM // 2
    t = pos.shape[0]
    inv_freq = jnp.power(ROPE_THETA, -jnp.arange(half, dtype=F32) * (2.0 / ROT_DIM))
    ang = pos.astype(F32)[:, None] * inv_freq[None, :]
    cos, sin = jnp.cos(ang), jnp.sin(ang)
    zeros = lambda w: jnp.zeros((t, w), F32)
    c = jnp.concatenate([cos, cos, jnp.ones((t, HEAD_DIM - ROT_DIM), F32)], axis=1)
    sa = jnp.concatenate([-sin, zeros(HEAD_DIM - half)], axis=1)
    sb = jnp.concatenate([zeros(half), sin, zeros(HEAD_DIM - ROT_DIM)], axis=1)
    return c * scale, sa * scale, sb * scale


def _logf_kernel(x_ref, w_ref, b_ref, o_ref):
    z = _dot(x_ref[...], w_ref[...]) + b_ref[...]
    o_ref[...] = jnp.minimum(z, 0.0) - jnp.log(1.0 + jnp.exp(-jnp.abs(z)))


def _logf(xn, w_pad, b_pad):
    m, d = xn.shape
    tm = _tile(m, 512)
    return pl.pallas_call(
        _logf_kernel,
        out_shape=jax.ShapeDtypeStruct((m, LANES), F32),
        grid=(m // tm,),
        in_specs=[pl.BlockSpec((tm, d), lambda i: (i, 0)),
                  pl.BlockSpec((d, LANES), lambda i: (0, 0)),
                  pl.BlockSpec((1, LANES), lambda i: (0, 0))],
        out_specs=pl.BlockSpec((tm, LANES), lambda i: (i, 0)),
        compiler_params=_cparams(("parallel",)), name="forget_logits",
    )(xn, w_pad, b_pad)


def _cumsum_kernel(x_ref, o_ref, carry_ref):
    @pl.when(pl.program_id(0) == 0)
    def _():
        carry_ref[...] = jnp.zeros_like(carry_ref)

    tb = x_ref.shape[0]
    row = lax.broadcasted_iota(jnp.int32, (tb, tb), 0)
    col = lax.broadcasted_iota(jnp.int32, (tb, tb), 1)
    tri = jnp.where(col <= row, 1.0, 0.0).astype(F32)
    c = jnp.dot(tri, x_ref[...], preferred_element_type=F32,
                precision=lax.Precision.HIGHEST) + carry_ref[...]
    o_ref[...] = c
    carry_ref[...] = c[tb - 1:tb, :]


def _cumsum_rows(x, tb):
    t, c = x.shape
    assert t % tb == 0 and c % LANES == 0
    return pl.pallas_call(
        _cumsum_kernel,
        out_shape=jax.ShapeDtypeStruct((t, c), F32),
        grid=(t // tb,),
        in_specs=[pl.BlockSpec((tb, c), lambda i: (i, 0))],
        out_specs=pl.BlockSpec((tb, c), lambda i: (i, 0)),
        scratch_shapes=[pltpu.VMEM((1, c), F32)],
        compiler_params=_cparams(("arbitrary",)), name="cumsum_rows",
    )(x)


def _lambda_value(lq1, lk1, lq2, lk2, lam_init):
    a = jnp.sum(lq1[...] * lk1[...], axis=-1, keepdims=True)
    b = jnp.sum(lq2[...] * lk2[...], axis=-1, keepdims=True)
    return jnp.exp(a) - jnp.exp(b) + lam_init


def _head_rmsnorm(o, g):
    return o * lax.rsqrt(jnp.mean(o * o, axis=-1, keepdims=True) + RMS_EPS) * g


def _online_update(s, m, l, acc_ref, v):
    m_new = jnp.maximum(m, jnp.max(s, axis=-1, keepdims=True))
    alpha = jnp.exp2(m - m_new)
    p = jnp.exp2(s - m_new)
    l_new = alpha * l + jnp.sum(p, axis=-1, keepdims=True)
    acc_ref[...] = alpha * acc_ref[...] + _dot(p.astype(BF16), v)
    return m_new, l_new


def _causal_sweep(n_full, scores_fn, values_fn, mask_fn, s_a, s_b, m_ref, l_ref, acc_refs):
    n_maps = len(acc_refs)
    for n in range(n_maps):
        m_ref[n] = jnp.full(m_ref.shape[1:], NEG_INF, F32)
        l_ref[n] = jnp.zeros(l_ref.shape[1:], F32)
        acc_refs[n][...] = jnp.zeros_like(acc_refs[n])

    def produce(bufs, j):
        for ref, s in zip(bufs, scores_fn(j)):
            ref[...] = s

    def consume(tiles, j):
        v = values_fn(j)
        for n in range(n_maps):
            m_new, l_new = _online_update(tiles[n], m_ref[n], l_ref[n], acc_refs[n], v)
            m_ref[n] = m_new
            l_ref[n] = l_new

    def pair(p, carry):
        produce(s_b, 2 * p + 1)
        consume([r[...] for r in s_a], 2 * p)
        produce(s_a, 2 * p + 2)
        consume([r[...] for r in s_b], 2 * p + 1)
        return carry

    produce(s_a, 0)
    lax.fori_loop(0, n_full // 2, pair, 0)
    diag = [mask_fn(s) for s in scores_fn(n_full)]

    @pl.when(n_full % 2 == 1)
    def _():
        consume([r[...] for r in s_a], n_full - 1)

    consume(diag, n_full)


def _key_tile(ref, j, tk):
    return ref[0, pl.ds(pl.multiple_of(j * tk, tk), tk), :]


def _sweep_tiles(t):
    tq = _tile(t, 512)
    tk = 2 * tq if t % (2 * tq) == 0 else tq
    assert tq % (1 << CHUNK_SHIFT) == 0
    return tq, tk


def _diag_position(i, tq, tk):
    n_full = (i * tq) // tk
    return n_full, i * tq - n_full * tk


def _diff_prompt_kernel(q_ref, k_ref, v_ref, g_ref, lq1, lk1, lq2, lk2, o_ref,
                        acc1, acc2, a1, a2, b1, b2, m_ref, l_ref, *, tq, tk, lam_init):
    n_full, offset = _diag_position(pl.program_id(1), tq, tk)
    q = q_ref[0]
    q1, q2 = q[:, :HEAD_DIM], q[:, HEAD_DIM:]

    def scores(j):
        k = _key_tile(k_ref, j, tk)
        return _dot_nt(q1, k[:, :HEAD_DIM]), _dot_nt(q2, k[:, HEAD_DIM:])

    def mask(s):
        row = lax.broadcasted_iota(jnp.int32, (tq, tk), 0) + offset
        col = lax.broadcasted_iota(jnp.int32, (tq, tk), 1)
        return jnp.where((col >> CHUNK_SHIFT) <= (row >> CHUNK_SHIFT), s, NEG_INF)

    _causal_sweep(n_full, scores, lambda j: _key_tile(v_ref, j, tk), mask,
                  (a1, a2), (b1, b2), m_ref, l_ref, (acc1, acc2))
    lam = _lambda_value(lq1, lk1, lq2, lk2, lam_init)
    o = acc1[...] / l_ref[0] - lam * (acc2[...] / l_ref[1])
    o_ref[...] = (_head_rmsnorm(o, g_ref[...]) * (1.0 - lam_init)).astype(o_ref.dtype)


def _diff_prompt_attention(q, k, v, g, lams, lam_init):
    nh, t, w = q.shape
    tq, tk = _sweep_tiles(t)
    vec = pl.BlockSpec((1, HEAD_DIM), lambda h, i: (0, 0))
    kern = functools.partial(_diff_prompt_kernel, tq=tq, tk=tk, lam_init=lam_init)
    return pl.pallas_call(
        kern,
        out_shape=jax.ShapeDtypeStruct((t, nh * w), BF16),
        grid=(nh, t // tq),
        in_specs=[pl.BlockSpec((1, tq, w), lambda h, i: (h, i, 0)),
                  pl.BlockSpec((1, t, w), lambda h, i: (h, 0, 0)),
                  pl.BlockSpec((1, t, w), lambda h, i: (h, 0, 0)),
                  pl.BlockSpec((1, w), lambda h, i: (0, 0)),
                  vec, vec, vec, vec],
        out_specs=pl.BlockSpec((tq, w), lambda h, i: (i, h)),
        scratch_shapes=[pltpu.VMEM((tq, w), F32)] * 2 + [pltpu.VMEM((tq, tk), F32)] * 4
        + [pltpu.VMEM((2, tq, 1), F32)] * 2,
        compiler_params=_cparams(("parallel", "arbitrary")), name="diff_prompt_attention",
    )(q, k, v, g, *lams)


def _fox_prompt_kernel(q_ref, k_ref, v_ref, ck_ref, g_ref, o_ref, acc, s_a, s_b, m_ref, l_ref,
                       *, tq, tk):
    n_full, offset = _diag_position(pl.program_id(1), tq, tk)
    q = q_ref[0]

    def scores(j):
        return (_dot_nt(q, _key_tile(k_ref, j, tk)) - ck_ref[0, j] * LOG2E,)

    def mask(s):
        row = lax.broadcasted_iota(jnp.int32, (tq, tk), 0) + offset
        col = lax.broadcasted_iota(jnp.int32, (tq, tk), 1)
        return jnp.where(col <= row, s, NEG_INF)

    _causal_sweep(n_full, scores, lambda j: _key_tile(v_ref, j, tk), mask,
                  (s_a,), (s_b,), m_ref, l_ref, (acc,))
    o_ref[...] = _head_rmsnorm(acc[...] / l_ref[0], g_ref[...]).astype(o_ref.dtype)


def _fox_prompt_attention(q, k, v, ck, g):
    nh, t, w = q.shape
    tq, tk = _sweep_tiles(t)
    assert ck.shape == (nh, t // tk, 1, tk)
    kern = functools.partial(_fox_prompt_kernel, tq=tq, tk=tk)
    return pl.pallas_call(
        kern,
        out_shape=jax.ShapeDtypeStruct((t, nh * w), BF16),
        grid=(nh, t // tq),
        in_specs=[pl.BlockSpec((1, tq, w), lambda h, i: (h, i, 0)),
                  pl.BlockSpec((1, t, w), lambda h, i: (h, 0, 0)),
                  pl.BlockSpec((1, t, w), lambda h, i: (h, 0, 0)),
                  pl.BlockSpec((1, t // tk, 1, tk), lambda h, i: (h, 0, 0, 0)),
                  pl.BlockSpec((1, w), lambda h, i: (0, 0))],
        out_specs=pl.BlockSpec((tq, w), lambda h, i: (i, h)),
        scratch_shapes=[pltpu.VMEM((tq, w), F32)] + [pltpu.VMEM((tq, tk), F32)] * 2
        + [pltpu.VMEM((1, tq, 1), F32)] * 2,
        compiler_params=_cparams(("parallel", "arbitrary")), name="fox_prompt_attention",
    )(q, k, v, ck, g)


def _two_segment_softmax(sp, sn):
    m = jnp.maximum(jnp.max(sp, axis=-1, keepdims=True), jnp.max(sn, axis=-1, keepdims=True))
    pp = jnp.exp2(sp - m)
    pn = jnp.exp2(sn - m)
    l = jnp.sum(pp, axis=-1, keepdims=True) + jnp.sum(pn, axis=-1, keepdims=True)
    return pp.astype(BF16), pn.astype(BF16), l


def _cache_view(cache):
    b, t, nh, w = cache.shape
    nc = w // LANES
    return cache.reshape(b, t, nh, nc, LANES).transpose(0, 1, 3, 2, 4).reshape(b, t * nc * nh, LANES)


def _cache_rows(ref, head, lane_block, n_heads, n_lane_blocks, n_rows):
    stride = n_heads * n_lane_blocks
    return ref[0, pl.ds(lane_block * n_heads + head, n_rows, stride=stride), :].astype(BF16)


def _diff_sample_kernel(q_ref, kn_ref, vn_ref, kp_ref, vp_ref, g_ref, lq1, lk1, lq2, lk2,
                        o_ref, *, past_len, lam_init, n_heads):
    t = q_ref.shape[1]
    w = 2 * HEAD_DIM
    row = lax.broadcasted_iota(jnp.int32, (t, t), 0) + past_len
    col = lax.broadcasted_iota(jnp.int32, (t, t), 1) + past_len
    ok = (col >> CHUNK_SHIFT) <= (row >> CHUNK_SHIFT)
    lam = _lambda_value(lq1, lk1, lq2, lk2, lam_init)
    for h in range(n_heads):
        sl = slice(h * w, (h + 1) * w)
        q = q_ref[0, :, sl]
        kn = kn_ref[0, :, sl]
        vn = vn_ref[0, :, sl]
        vp = [_cache_rows(vp_ref, h, c, n_heads, 2, past_len) for c in range(2)]
        outs = []
        for half in range(2):
            hs = slice(half * HEAD_DIM, (half + 1) * HEAD_DIM)
            sp = _dot_nt(q[:, hs], _cache_rows(kp_ref, h, half, n_heads, 2, past_len))
            sn = jnp.where(ok, _dot_nt(q[:, hs], kn[:, hs]), NEG_INF)
            pp, pn, l = _two_segment_softmax(sp, sn)
            o_past = jnp.concatenate([_dot(pp, vp[0]), _dot(pp, vp[1])], axis=1)
            outs.append((o_past + _dot(pn, vn)) / l)
        o = outs[0] - lam * outs[1]
        o_ref[0, :, sl] = (_head_rmsnorm(o, g_ref[...]) * (1.0 - lam_init)).astype(o_ref.dtype)


def _diff_sample_attention(q, kn, vn, kp, vp, g, lams, lam_init):
    b, t, n = q.shape
    past = kp.shape[1]
    kp, vp = _cache_view(kp), _cache_view(vp)
    new = pl.BlockSpec((1, t, n), lambda i: (i, 0, 0))
    old = pl.BlockSpec((1,) + kp.shape[1:], lambda i: (i, 0, 0))
    vec = pl.BlockSpec((1, HEAD_DIM), lambda i: (0, 0))
    kern = functools.partial(_diff_sample_kernel, past_len=past, lam_init=lam_init,
                             n_heads=n // (2 * HEAD_DIM))
    return pl.pallas_call(
        kern,
        out_shape=jax.ShapeDtypeStruct((b, t, n), BF16),
        grid=(b,),
        in_specs=[new, new, new, old, old,
                  pl.BlockSpec((1, 2 * HEAD_DIM), lambda i: (0, 0)), vec, vec, vec, vec],
        out_specs=new,
        compiler_params=_cparams(("parallel",)), name="diff_sample_attention",
    )(q, kn, vn, kp, vp, g, *lams)


def _fox_sample_kernel(q_ref, kn_ref, vn_ref, kp_ref, vp_ref, ckp_ref, ckn_ref, g_ref, o_ref,
                       *, n_heads):
    t = q_ref.shape[1]
    row = lax.broadcasted_iota(jnp.int32, (t, t), 0)
    col = lax.broadcasted_iota(jnp.int32, (t, t), 1)
    ok = col <= row
    for h in range(n_heads):
        sl = slice(h * HEAD_DIM, (h + 1) * HEAD_DIM)
        q = q_ref[0, :, sl]
        past = ckp_ref.shape[-1]
        kp = _cache_rows(kp_ref, h, 0, n_heads, 1, past)
        vp = _cache_rows(vp_ref, h, 0, n_heads, 1, past)
        sp = _dot_nt(q, kp) - ckp_ref[0, h] * LOG2E
        sn = jnp.where(ok, _dot_nt(q, kn_ref[0, :, sl]) - ckn_ref[0, h] * LOG2E, NEG_INF)
        pp, pn, l = _two_segment_softmax(sp, sn)
        o = (_dot(pp, vp) + _dot(pn, vn_ref[0, :, sl])) / l
        o_ref[0, :, sl] = _head_rmsnorm(o, g_ref[...]).astype(o_ref.dtype)


def _fox_sample_attention(q, kn, vn, kp, vp, ckp, ckn, g):
    b, t, n = q.shape
    past = kp.shape[1]
    nh = n // HEAD_DIM
    kp, vp = _cache_view(kp), _cache_view(vp)
    new = pl.BlockSpec((1, t, n), lambda i: (i, 0, 0))
    old = pl.BlockSpec((1,) + kp.shape[1:], lambda i: (i, 0, 0))
    kern = functools.partial(_fox_sample_kernel, n_heads=nh)
    return pl.pallas_call(
        kern,
        out_shape=jax.ShapeDtypeStruct((b, t, n), BF16),
        grid=(b,),
        in_specs=[new, new, new, old, old,
                  pl.BlockSpec((1, nh, 1, past), lambda i: (i, 0, 0, 0)),
                  pl.BlockSpec((1, nh, 1, t), lambda i: (i, 0, 0, 0)),
                  pl.BlockSpec((1, HEAD_DIM), lambda i: (0, 0))],
        out_specs=new,
        compiler_params=_cparams(("parallel",)), name="fox_sample_attention",
    )(q, kn, vn, kp, vp, ckp, ckn, g)


def _split_bf16(a):
    hi = a.astype(BF16)
    return hi, (a - hi.astype(F32)).astype(BF16)


def _outproj_router_kernel(md_ref, mf_ref, wd_ref, wf_ref, x_ref, g_ref, wr_ref, br_ref,
                           h_ref, hn_ref, idx_ref, gate_ref):
    h = x_ref[...] + _dot(md_ref[...], wd_ref[...]) + _dot(mf_ref[...], wf_ref[...])
    h_ref[...] = h
    hn = h * lax.rsqrt(jnp.mean(h * h, axis=-1, keepdims=True) + RMS_EPS) * g_ref[...]
    for c in range(hn_ref.shape[1]):
        hn_ref[:, c, :] = hn[:, c * LANES:(c + 1) * LANES]
    hn_hi, hn_lo = _split_bf16(hn)
    wr_hi, wr_lo = _split_bf16(wr_ref[...])
    logits = _dot(hn_hi, wr_hi) + (_dot(hn_lo, wr_hi) + _dot(hn_hi, wr_lo)) + br_ref[...]
    lane = lax.broadcasted_iota(jnp.int32, logits.shape, 1)
    work = logits
    idx_out = jnp.zeros(logits.shape, jnp.int32)
    val_out = jnp.zeros(logits.shape, F32)
    top = None
    denom = None
    for k in range(TOP_K):
        mx = jnp.max(work, axis=-1, keepdims=True)
        first = jnp.min(jnp.where(work == mx, lane, LANES), axis=-1, keepdims=True)
        work = jnp.where(lane == first, -jnp.inf, work)
        if k == 0:
            top = mx
        e = jnp.exp(mx - top)
        denom = e if k == 0 else denom + e
        idx_out = jnp.where(lane == k, first, idx_out)
        val_out = jnp.where(lane == k, e, val_out)
    idx_ref[...] = idx_out
    gate_ref[...] = val_out / denom


def _outproj_router(mix_d, mix_f, w_d, w_f, x, g, w_router_pad, b_router_pad):
    m, d = x.shape
    kd, kf = mix_d.shape[1], mix_f.shape[1]
    tm = _tile(m, 256)
    row = lambda w: pl.BlockSpec((tm, w), lambda i: (i, 0))
    full = lambda r, c: pl.BlockSpec((r, c), lambda i: (0, 0))
    return pl.pallas_call(
        _outproj_router_kernel,
        out_shape=[jax.ShapeDtypeStruct((m, d), F32),
                   jax.ShapeDtypeStruct((m, d // LANES, LANES), F32),
                   jax.ShapeDtypeStruct((m, LANES), jnp.int32),
                   jax.ShapeDtypeStruct((m, LANES), F32)],
        grid=(m // tm,),
        in_specs=[row(kd), row(kf), full(kd, d), full(kf, d), row(d), full(1, d),
                  full(d, LANES), full(1, LANES)],
        out_specs=[row(d), pl.BlockSpec((tm, d // LANES, LANES), lambda i: (i, 0, 0)),
                   row(LANES), row(LANES)],
        compiler_params=_cparams(("parallel",)), name="outproj_router",
    )(mix_d, mix_f, w_d, w_f, x, g, w_router_pad, b_router_pad)


def _rank_kernel(idx_ref, rank_ref, count_ref, carry_ref):
    @pl.when(pl.program_id(0) == 0)
    def _():
        carry_ref[...] = jnp.zeros_like(carry_ref)

    tm = idx_ref.shape[0]
    idx = idx_ref[...]
    lane = lax.broadcasted_iota(jnp.int32, (tm, LANES), 1)
    picks = [lane == idx[:, k:k + 1] for k in range(TOP_K)]
    onehot = jnp.zeros((tm, LANES), F32)
    for p in picks:
        onehot = onehot + jnp.where(p, 1.0, 0.0)
    row = lax.broadcasted_iota(jnp.int32, (tm, tm), 0)
    col = lax.broadcasted_iota(jnp.int32, (tm, tm), 1)
    strict = jnp.where(col < row, 1.0, 0.0).astype(BF16)
    before = _dot(strict, onehot.astype(BF16)) + carry_ref[...]
    rank = jnp.zeros((tm, LANES), F32)
    for k, p in enumerate(picks):
        r = jnp.sum(jnp.where(p, before, 0.0), axis=-1, keepdims=True)
        rank = jnp.where(lane == k, r, rank)
    rank_ref[...] = rank.astype(jnp.int32)
    total = before[tm - 1:tm, :] + onehot[tm - 1:tm, :]
    carry_ref[...] = total
    count_ref[...] = total.astype(jnp.int32)


def _expert_ranks(idx):
    n_tok = idx.shape[0]
    tm = _tile(n_tok, 512)
    return pl.pallas_call(
        _rank_kernel,
        out_shape=[jax.ShapeDtypeStruct((n_tok, LANES), jnp.int32),
                   jax.ShapeDtypeStruct((1, LANES), jnp.int32)],
        grid=(n_tok // tm,),
        in_specs=[pl.BlockSpec((tm, LANES), lambda i: (i, 0))],
        out_specs=[pl.BlockSpec((tm, LANES), lambda i: (i, 0)),
                   pl.BlockSpec((1, LANES), lambda i: (0, 0))],
        scratch_shapes=[pltpu.VMEM((1, LANES), F32)],
        compiler_params=_cparams(("arbitrary",)), name="expert_ranks",
    )(idx)


def _row_copy(src_ref, dst_ref, sem, src_row, dst_row):
    return pltpu.make_async_copy(src_ref.at[pl.ds(src_row, 1)], dst_ref.at[pl.ds(dst_row, 1)], sem)


def _gather_rows(idx_ref, src_ref, dst_ref, sem, n_rows):
    assert n_rows % GATHER_UNROLL == 0

    def start(g, carry):
        base = pl.multiple_of(g * GATHER_UNROLL, GATHER_UNROLL)
        for u in range(GATHER_UNROLL):
            _row_copy(src_ref, dst_ref, sem, idx_ref[0, 0, base + u], base + u).start()
        return carry

    def wait(g, carry):
        base = pl.multiple_of(g * GATHER_UNROLL, GATHER_UNROLL)
        for u in range(GATHER_UNROLL):
            _row_copy(src_ref, dst_ref, sem, 0, base + u).wait()
        return carry

    lax.fori_loop(0, n_rows // GATHER_UNROLL, start, 0)
    lax.fori_loop(0, n_rows // GATHER_UNROLL, wait, 0)


def _dispatch_kernel(pad_start_ref, pad_count_ref, dest_ref, hp_ref, hs_ref, xs_ref, zero_buf, sem,
                     *, ts, n_prompt_steps, n_pad):
    i = pl.program_id(0)

    @pl.when(i == 0)
    def _():
        zero_buf[...] = jnp.zeros_like(zero_buf)

        def fill(e, carry):
            def start(r, c):
                pltpu.make_async_copy(zero_buf, xs_ref.at[pad_start_ref[e] + r], sem).start()
                return c

            def wait(r, c):
                pltpu.make_async_copy(zero_buf, xs_ref.at[0], sem).wait()
                return c

            lax.fori_loop(0, pad_count_ref[e], start, 0)
            lax.fori_loop(0, pad_count_ref[e], wait, 0)
            return carry

        lax.fori_loop(0, n_pad, fill, 0)

    def copy_tokens(src_ref):
        per_trip = GATHER_UNROLL // TOP_K

        def start(g, c):
            for u in range(per_trip):
                tok = g * per_trip + u
                for k in range(TOP_K):
                    pltpu.make_async_copy(src_ref.at[tok],
                                          xs_ref.at[dest_ref[0, 0, tok * TOP_K + k]], sem).start()
            return c

        def wait(g, c):
            for _ in range(GATHER_UNROLL):
                pltpu.make_async_copy(src_ref.at[0], xs_ref.at[0], sem).wait()
            return c

        lax.fori_loop(0, ts // per_trip, start, 0)
        lax.fori_loop(0, ts // per_trip, wait, 0)

    @pl.when(i < n_prompt_steps)
    def _():
        copy_tokens(hp_ref)

    @pl.when(i >= n_prompt_steps)
    def _():
        copy_tokens(hs_ref)


def _dispatch(pad_start, pad_count, dest, hn_prompt, hn_sample, n_rows):
    n_p, n_s = hn_prompt.shape[0], hn_sample.shape[0]
    ts = _tile(math.gcd(n_p, n_s), 512)
    assert ts % (GATHER_UNROLL // TOP_K) == 0
    n_steps = (n_p + n_s) // ts
    kern = functools.partial(_dispatch_kernel, ts=ts, n_prompt_steps=n_p // ts,
                             n_pad=pad_start.shape[0])
    slab = hn_prompt.shape[1:]
    last_p = n_p // ts - 1
    return pl.pallas_call(
        kern,
        out_shape=jax.ShapeDtypeStruct((n_rows,) + slab, hn_prompt.dtype),
        grid_spec=pltpu.PrefetchScalarGridSpec(
            num_scalar_prefetch=2,
            grid=(n_steps,),
            in_specs=[pl.BlockSpec((1, 1, ts * TOP_K), lambda i, ps, pc: (i, 0, 0),
                                   memory_space=pltpu.SMEM),
                      pl.BlockSpec((ts,) + slab, lambda i, ps, pc: (jnp.minimum(i, last_p), 0, 0)),
                      pl.BlockSpec((ts,) + slab,
                                   lambda i, ps, pc: (jnp.maximum(i - last_p - 1, 0), 0, 0))],
            out_specs=pl.BlockSpec(memory_space=pl.ANY),
            scratch_shapes=[pltpu.VMEM(hn_prompt.shape[1:], hn_prompt.dtype),
                            pltpu.SemaphoreType.DMA]),
        compiler_params=_cparams(("arbitrary",)), name="moe_dispatch",
    )(pad_start, pad_count, dest.reshape(n_steps, 1, ts * TOP_K), hn_prompt, hn_sample)


def _pack_rows_kernel(x_ref, o_ref):
    rows, d = o_ref.shape
    n_chunks = d // LANES
    for c in range(n_chunks):
        o_ref[:, c * LANES:(c + 1) * LANES] = x_ref[pl.ds(c, rows, stride=n_chunks), :].astype(BF16)


def _pack_rows(xs_slabs):
    n_rows, n_chunks, _ = xs_slabs.shape
    rows = _tile(n_rows, DISPATCH_ROWS)
    return pl.pallas_call(
        _pack_rows_kernel,
        out_shape=jax.ShapeDtypeStruct((n_rows, n_chunks * LANES), BF16),
        grid=(n_rows // rows,),
        in_specs=[pl.BlockSpec((rows * n_chunks, LANES), lambda b: (b, 0))],
        out_specs=pl.BlockSpec((rows, n_chunks * LANES), lambda b: (b, 0)),
        compiler_params=_cparams(("parallel",)), name="moe_pack_rows",
    )(xs_slabs.reshape(n_rows * n_chunks, LANES))


def _combine_kernel(dest_ref, y_ref, h_ref, gate_ref, g_ref, o_ref, buf, sem, *, tm):
    _gather_rows(dest_ref, y_ref, buf, sem, TOP_K * tm)
    gates = gate_ref[...]
    out = h_ref[...]
    for k in range(TOP_K):
        out = out + buf[k * tm:(k + 1) * tm, :] * gates[:, k:k + 1]
    out = out * lax.rsqrt(jnp.mean(out * out, axis=-1, keepdims=True) + RMS_EPS)
    o_ref[...] = out * g_ref[...]


def _combine(dest, y_rows, h, gates, g):
    n_tiles, _, per = dest.shape
    tm = per // TOP_K
    m, d = h.shape
    kern = functools.partial(_combine_kernel, tm=tm)
    return pl.pallas_call(
        kern,
        out_shape=jax.ShapeDtypeStruct((m, d), F32),
        grid=(n_tiles,),
        in_specs=[pl.BlockSpec((1, 1, per), lambda i: (i, 0, 0), memory_space=pltpu.SMEM),
                  pl.BlockSpec(memory_space=pl.ANY),
                  pl.BlockSpec((tm, d), lambda i: (i, 0)),
                  pl.BlockSpec((tm, LANES), lambda i: (i, 0)),
                  pl.BlockSpec((1, d), lambda i: (0, 0))],
        out_specs=pl.BlockSpec((tm, d), lambda i: (i, 0)),
        scratch_shapes=[pltpu.VMEM((per, d), y_rows.dtype), pltpu.SemaphoreType.DMA],
        compiler_params=_cparams(("arbitrary",)), name="moe_combine",
    )(dest, y_rows, h, gates, g)


def _expert_changed(be_ref, rb):
    prev = be_ref[jnp.maximum(rb - 1, 0)]
    return jnp.logical_or(rb == 0, be_ref[rb] != prev)


def _moe_up_kernel(be_ref, used_ref, x_ref, wg_ref, wu_ref, bg_ref, bu_ref, o_ref, wg16, wu16):
    rb = pl.program_id(1)

    @pl.when(_expert_changed(be_ref, rb))
    def _():
        wg16[...] = wg_ref[0].astype(BF16)
        wu16[...] = wu_ref[0].astype(BF16)

    @pl.when(rb < used_ref[0])
    def _():
        x = x_ref[...]
        gate = jnp.minimum(_dot(x, wg16[...]) + bg_ref[0], SWIGLU_LIMIT)
        up = jnp.clip(_dot(x, wu16[...]) + bu_ref[0], -SWIGLU_LIMIT, SWIGLU_LIMIT)
        act = (up + 1.0) * (gate * jax.nn.sigmoid(SWIGLU_ALPHA * gate))
        o_ref[...] = act.astype(o_ref.dtype)

    @pl.when(rb >= used_ref[0])
    def _():
        o_ref[...] = jnp.zeros_like(o_ref)


def _moe_up(block_expert, n_used, xs, w_gate, w_up, b_gate, b_up, tn):
    n_rows, d = xs.shape
    n_exp, _, d_ff = w_gate.shape
    rows = MOE_ROWS
    n_blocks = n_rows // rows
    wspec = pl.BlockSpec((1, d, tn), lambda n, rb, be, used: (be[rb], 0, n))
    bspec = pl.BlockSpec((1, 1, tn), lambda n, rb, be, used: (be[rb], 0, n))
    return pl.pallas_call(
        _moe_up_kernel,
        out_shape=jax.ShapeDtypeStruct((n_rows, d_ff), BF16),
        grid_spec=pltpu.PrefetchScalarGridSpec(
            num_scalar_prefetch=2,
            grid=(d_ff // tn, n_blocks),
            in_specs=[pl.BlockSpec((rows, d), lambda n, rb, be, used: (rb, 0)),
                      wspec, wspec, bspec, bspec],
            out_specs=pl.BlockSpec((rows, tn), lambda n, rb, be, used: (rb, n)),
            scratch_shapes=[pltpu.VMEM((d, tn), BF16), pltpu.VMEM((d, tn), BF16)]),
        compiler_params=_cparams(("arbitrary", "arbitrary")), name="moe_gate_up",
    )(block_expert, n_used, xs, w_gate, w_up, b_gate.reshape(n_exp, 1, d_ff),
      b_up.reshape(n_exp, 1, d_ff))


def _moe_down_kernel(be_ref, used_ref, a_ref, w_ref, b_ref, o_ref, w16):
    rb = pl.program_id(1)

    @pl.when(_expert_changed(be_ref, rb))
    def _():
        w16[...] = w_ref[0].astype(BF16)

    @pl.when(rb < used_ref[0])
    def _():
        o_ref[...] = _dot(a_ref[...], w16[...]) + b_ref[0]

    @pl.when(rb >= used_ref[0])
    def _():
        o_ref[...] = jnp.zeros_like(o_ref)


def _moe_down(block_expert, n_used, act, w_down, b_down, tn):
    n_rows, d_ff = act.shape
    n_exp, _, d = w_down.shape
    rows = MOE_ROWS
    return pl.pallas_call(
        _moe_down_kernel,
        out_shape=jax.ShapeDtypeStruct((n_rows, d), F32),
        grid_spec=pltpu.PrefetchScalarGridSpec(
            num_scalar_prefetch=2,
            grid=(d // tn, n_rows // rows),
            in_specs=[pl.BlockSpec((rows, d_ff), lambda n, rb, be, used: (rb, 0)),
                      pl.BlockSpec((1, d_ff, tn), lambda n, rb, be, used: (be[rb], 0, n)),
                      pl.BlockSpec((1, 1, tn), lambda n, rb, be, used: (be[rb], 0, n))],
            out_specs=pl.BlockSpec((rows, tn), lambda n, rb, be, used: (rb, n)),
            scratch_shapes=[pltpu.VMEM((d_ff, tn), BF16)]),
        compiler_params=_cparams(("arbitrary", "arbitrary")), name="moe_down",
    )(block_expert, n_used, act, w_down, b_down.reshape(n_exp, 1, d))


def _pad_lanes(a, fill=0.0):
    return jnp.pad(a, ((0, 0), (0, LANES - a.shape[1])), constant_values=fill)


def _mixer_projections(x2d, pos, norm_g, w_in, b_forget, head_major):
    d = x2d.shape[1]
    dw = (w_in.shape[1] - 8) // 6
    q_scale = HEAD_DIM ** -0.5 * LOG2E
    xn = _rmsnorm_bf16(x2d, norm_g)
    w16 = w_in.astype(BF16)
    wcol = lambda j: w16[:, j * dw:(j + 1) * dw]
    tabs_q = _rope_tables(pos, q_scale)
    tabs_k = _rope_tables(pos, 1.0)
    hm = dict(head_major=head_major)
    (dq,) = _proj(xn, wcol(0), rope_tabs=tabs_q, head_w=2 * HEAD_DIM, **hm)
    dk32, dk = _proj(xn, wcol(1), rope_tabs=tabs_k, want_f32=True, head_w=2 * HEAD_DIM, **hm)
    dv32, dv = _proj(xn, wcol(2), want_f32=True, head_w=2 * HEAD_DIM, **hm)
    (fq,) = _proj(xn, wcol(3), scale=q_scale, **hm)
    fk32, fk = _proj(xn, wcol(4), want_f32=True, **hm)
    fv32, fv = _proj(xn, wcol(5), want_f32=True, **hm)
    logf = _logf(xn, _pad_lanes(w16[:, 6 * dw:]), _pad_lanes(b_forget.reshape(1, -1)))
    return (dq, dk, dv, fq, fk, fv), (dk32, dv32, fk32, fv32), logf


def _moe(hn_prompt, hn_sample, idx_all, w_gate, w_up, w_down, b_gate, b_up, b_down):
    n_tok = idx_all.shape[0]
    n_exp = w_gate.shape[0]
    rows = MOE_ROWS
    n_blocks = -(-(n_tok * TOP_K) // rows) + n_exp
    n_rows = n_blocks * rows

    rank, counts = _expert_ranks(idx_all)
    experts = idx_all[:, :TOP_K]
    counts = counts[0, :n_exp]
    padded = (counts + rows - 1) // rows * rows
    padded_end = jnp.cumsum(padded)
    padded_start = padded_end - padded
    dest = padded_start[experts] + rank[:, :TOP_K]
    pad_start = jnp.concatenate([padded_start + counts, padded_end[-1:]]).astype(jnp.int32)
    pad_count = jnp.concatenate([padded - counts, n_rows - padded_end[-1:]]).astype(jnp.int32)
    block_start = jnp.arange(n_blocks, dtype=jnp.int32) * rows
    block_expert = jnp.minimum(
        jnp.sum((padded_end[None, :] <= block_start[:, None]).astype(jnp.int32), axis=1), n_exp - 1)
    n_used = (padded_end[-1] // rows).astype(jnp.int32).reshape(1)

    xs = _pack_rows(_dispatch(pad_start, pad_count, dest, hn_prompt, hn_sample, n_rows))
    act = _moe_up(block_expert, n_used, xs, w_gate, w_up, b_gate, b_up, tn=512)
    y_rows = _moe_down(block_expert, n_used, act, w_down, b_down, tn=min(1024, w_down.shape[2]))
    return y_rows, dest


def _combine_tokens(dest, y_rows, h, gates, g):
    m = h.shape[0]
    tm = _tile(m, 256)
    dest_t = dest.reshape(m // tm, tm, TOP_K).transpose(0, 2, 1).reshape(m // tm, 1, TOP_K * tm)
    return _combine(dest_t, y_rows, h, gates, g.reshape(1, -1))


def kernel(x_prompt, x_sample, cache_diff_k, cache_diff_v, cache_fox_k, cache_fox_v, cache_fox_logf,
           attn_norm_g, w_in, b_forget, lambda_q1, lambda_k1, lambda_q2, lambda_k2,
           diff_out_norm_g, fox_out_norm_g, w_out, ffn_norm_g, w_router, b_router,
           w_gate, b_gate, w_up, b_up, w_down, b_down, final_norm_g):
    depth = w_in.shape[0]
    assert depth == 1, "single-layer stack"
    lam_init = 0.8 - 0.6 * math.exp(-0.3 * 0)
    drop = lambda a: a.reshape(a.shape[1:])
    (cache_diff_k, cache_diff_v, cache_fox_k, cache_fox_v, cache_fox_logf, attn_norm_g, w_in,
     b_forget, lambda_q1, lambda_k1, lambda_q2, lambda_k2, diff_out_norm_g, fox_out_norm_g, w_out,
     ffn_norm_g, w_router, b_router, w_gate, b_gate, w_up, b_up, w_down, b_down) = map(drop, (
         cache_diff_k, cache_diff_v, cache_fox_k, cache_fox_v, cache_fox_logf, attn_norm_g, w_in,
         b_forget, lambda_q1, lambda_k1, lambda_q2, lambda_k2, diff_out_norm_g, fox_out_norm_g,
         w_out, ffn_norm_g, w_router, b_router, w_gate, b_gate, w_up, b_up, w_down, b_down))
    bp, t, d = x_prompt.shape
    assert bp == 1
    bs, ts, _ = x_sample.shape
    past = cache_diff_k.shape[1]
    n_dh = cache_diff_k.shape[2]
    n_fh = cache_fox_k.shape[2]
    dw = n_dh * 2 * HEAD_DIM
    fw = n_fh * HEAD_DIM
    lams = [v.reshape(1, HEAD_DIM) for v in (lambda_q1, lambda_k1, lambda_q2, lambda_k2)]
    diff_g = diff_out_norm_g.reshape(1, -1)
    fox_g = fox_out_norm_g.reshape(1, -1)

    xp = x_prompt.reshape(t, d)
    pos_p = jnp.arange(t, dtype=jnp.int32)
    (dq, dk, dv, fq, fk, fv), cache_p, logf_p = _mixer_projections(
        xp, pos_p, attn_norm_g, w_in, b_forget, head_major=True)
    _, tk = _sweep_tiles(t)
    c_p = _cumsum_rows(logf_p, _tile(t, 512))
    ck_p = c_p[:, :n_fh].T.reshape(n_fh, t // tk, 1, tk)
    mix_d_p = _diff_prompt_attention(dq, dk, dv, diff_g, lams, lam_init)
    mix_f_p = _fox_prompt_attention(fq, fk, fv, ck_p, fox_g)

    xs = x_sample.reshape(bs * ts, d)
    pos_s = past + jnp.arange(ts, dtype=jnp.int32)
    (sdq, sdk, sdv, sfq, sfk, sfv), cache_s, logf_s = _mixer_projections(
        xs, jnp.tile(pos_s, bs), attn_norm_g, w_in, b_forget, head_major=False)
    logf_all = jnp.concatenate([cache_fox_logf.astype(F32),
                                logf_s[:, :n_fh].reshape(bs, ts, n_fh)], axis=1)
    tt = past + ts
    lf_cols = logf_all.transpose(1, 0, 2).reshape(tt, bs * n_fh)
    pad_c = (-lf_cols.shape[1]) % LANES
    lf_cols = jnp.pad(lf_cols, ((0, 0), (0, pad_c)))
    tb = max(tb_ for tb_ in range(8, 513, 8) if tt % tb_ == 0)
    c_s = _cumsum_rows(lf_cols, tb)[:, :bs * n_fh].reshape(tt, bs, n_fh).transpose(1, 2, 0)
    ckp_s = c_s[:, :, None, :past]
    ckn_s = c_s[:, :, None, past:]
    r3 = lambda a: a.reshape(bs, ts, -1)
    mix_d_s = _diff_sample_attention(
        r3(sdq), r3(sdk), r3(sdv), cache_diff_k, cache_diff_v,
        diff_g, lams, lam_init).reshape(bs * ts, dw)
    mix_f_s = _fox_sample_attention(
        r3(sfq), r3(sfk), r3(sfv), cache_fox_k, cache_fox_v,
        ckp_s, ckn_s, fox_g).reshape(bs * ts, fw)

    w_out16 = w_out.astype(BF16)
    w_router_pad = _pad_lanes(w_router)
    b_router_pad = _pad_lanes(b_router.reshape(1, -1), fill=NEG_INF)
    ffn_g = ffn_norm_g.reshape(1, -1)
    h_p, hn_p, idx_p, gate_p = _outproj_router(mix_d_p, mix_f_p, w_out16[:dw], w_out16[dw:],
                                               xp, ffn_g, w_router_pad, b_router_pad)
    h_s, hn_s, idx_s, gate_s = _outproj_router(mix_d_s, mix_f_s, w_out16[:dw], w_out16[dw:],
                                               xs, ffn_g, w_router_pad, b_router_pad)

    idx_all = jnp.concatenate([idx_p, idx_s], axis=0)
    y_rows, dest = _moe(hn_p, hn_s, idx_all, w_gate, w_up, w_down, b_gate, b_up, b_down)
    y_p = _combine_tokens(dest[:t], y_rows, h_p, gate_p, final_norm_g)
    y_s = _combine_tokens(dest[t:], y_rows, h_s, gate_s, final_norm_g)

    dk32, dv32, fk32, fv32 = cache_p
    sdk32, sdv32, sfk32, sfv32 = cache_s
    return (y_p.reshape(1, t, d), y_s.reshape(bs, ts, d),
            dk32.reshape(1, 1, t, n_dh, 2 * HEAD_DIM), dv32.reshape(1, 1, t, n_dh, 2 * HEAD_DIM),
            fk32.reshape(1, 1, t, n_fh, HEAD_DIM), fv32.reshape(1, 1, t, n_fh, HEAD_DIM),
            logf_p[:, :n_fh].reshape(1, 1, t, n_fh),
            sdk32.reshape(1, bs, ts, n_dh, 2 * HEAD_DIM), sdv32.reshape(1, bs, ts, n_dh, 2 * HEAD_DIM),
            sfk32.reshape(1, bs, ts, n_fh, HEAD_DIM), sfv32.reshape(1, bs, ts, n_fh, HEAD_DIM),
            logf_s[:, :n_fh].reshape(1, bs, ts, n_fh))
```

```python
import functools
import math

import jax
import jax.numpy as jnp
from jax import lax
from jax.experimental import pallas as pl
from jax.experimental.pallas import tpu as pltpu

HEAD_DIM = 128
CHUNK_SHIFT = 6
ROT_DIM = HEAD_DIM // 4
ROPE_THETA = 500000.0
TOP_K = 4
SWIGLU_LIMIT = 7.0
SWIGLU_ALPHA = 1.702
RMS_EPS = 1e-6
NEG_INF = -1e30
LOG2E = 1.4426950408889634
LANES = 128
MOE_ROWS = 512
GATHER_UNROLL = 8
DISPATCH_ROWS = 1024
VMEM_LIMIT = 56 * 1024 * 1024

F32 = jnp.float32
BF16 = jnp.bfloat16


def _cparams(sem, vmem=VMEM_LIMIT):
    return pltpu.CompilerParams(dimension_semantics=sem, vmem_limit_bytes=vmem)


def _tile(n, pref):
    if n <= pref:
        return n
    t = max(c for c in range(8, pref + 1, 8) if n % c == 0)
    return t


def _dot_nt(a, b):
    return lax.dot_general(a, b, (((1,), (1,)), ((), ())), preferred_element_type=F32)


def _dot(a, b):
    return jnp.dot(a, b, preferred_element_type=F32)


def _rmsnorm_kernel(x_ref, g_ref, o_ref):
    x = x_ref[...]
    y = x * lax.rsqrt(jnp.mean(x * x, axis=-1, keepdims=True) + RMS_EPS)
    o_ref[...] = (y * g_ref[...]).astype(o_ref.dtype)


def _rmsnorm_bf16(x, g):
    m, d = x.shape
    tm = _tile(m, 512)
    return pl.pallas_call(
        _rmsnorm_kernel,
        out_shape=jax.ShapeDtypeStruct((m, d), BF16),
        grid=(m // tm,),
        in_specs=[pl.BlockSpec((tm, d), lambda i: (i, 0)),
                  pl.BlockSpec((1, d), lambda i: (0, 0))],
        out_specs=pl.BlockSpec((tm, d), lambda i: (i, 0)),
        compiler_params=_cparams(("parallel",)),
        name="rmsnorm_bf16",
    )(x, g.reshape(1, d))


def _proj_kernel(*refs, rope, scale, want_f32, want_bf16, head_major, head_w):
    it = iter(refs)
    x_ref = next(it)
    w_ref = next(it)
    if rope:
        c_ref, sa_ref, sb_ref = next(it), next(it), next(it)
    o32_ref = next(it) if want_f32 else None
    o16_ref = next(it) if want_bf16 else None

    n = w_ref.shape[1]
    y = _dot(x_ref[...], w_ref[...])
    for h in range(n // HEAD_DIM):
        sl = slice(h * HEAD_DIM, (h + 1) * HEAD_DIM)
        yh = y[:, sl]
        if rope:
            yh = (yh * c_ref[...]
                  + pltpu.roll(yh, HEAD_DIM - ROT_DIM // 2, 1) * sa_ref[...]
                  + pltpu.roll(yh, ROT_DIM // 2, 1) * sb_ref[...])
        elif scale != 1.0:
            yh = yh * scale
        per = head_w // HEAD_DIM
        if want_f32:
            o32_ref[:, h // per, (h % per) * HEAD_DIM:(h % per + 1) * HEAD_DIM] = yh
        if want_bf16:
            if head_major:
                o16_ref[h // per, :, (h % per) * HEAD_DIM:(h % per + 1) * HEAD_DIM] = yh.astype(BF16)
            else:
                o16_ref[:, sl] = yh.astype(BF16)


def _proj(xn, w, rope_tabs=None, scale=1.0, want_f32=False, want_bf16=True,
          head_major=False, head_w=HEAD_DIM):
    m, d = xn.shape
    n = w.shape[1]
    tm = _tile(m, 512)
    rope = rope_tabs is not None
    in_specs = [pl.BlockSpec((tm, d), lambda i: (i, 0)),
                pl.BlockSpec((d, n), lambda i: (0, 0))]
    args = [xn, w]
    if rope:
        in_specs += [pl.BlockSpec((tm, HEAD_DIM), lambda i: (i, 0))] * 3
        args += list(rope_tabs)
    out_shape, out_specs = [], []
    if want_f32:
        out_shape.append(jax.ShapeDtypeStruct((m, n // head_w, head_w), F32))
        out_specs.append(pl.BlockSpec((tm, n // head_w, head_w), lambda i: (i, 0, 0)))
    if want_bf16:
        if head_major:
            nh = n // head_w
            out_shape.append(jax.ShapeDtypeStruct((nh, m, head_w), BF16))
            out_specs.append(pl.BlockSpec((nh, tm, head_w), lambda i: (0, i, 0)))
        else:
            out_shape.append(jax.ShapeDtypeStruct((m, n), BF16))
            out_specs.append(pl.BlockSpec((tm, n), lambda i: (i, 0)))
    kern = functools.partial(_proj_kernel, rope=rope, scale=scale, want_f32=want_f32,
                             want_bf16=want_bf16, head_major=head_major, head_w=head_w)
    return pl.pallas_call(
        kern, out_shape=out_shape, grid=(m // tm,), in_specs=in_specs, out_specs=out_specs,
        compiler_params=_cparams(("parallel",)), name="in_proj",
    )(*args)


def _rope_tables(pos, scale):
    half = ROT_DIM // 2
    t = pos.shape[0]
    inv_freq = jnp.power(ROPE_THETA, -jnp.arange(half, dtype=F32) * (2.0 / ROT_DIM))
    ang = pos.astype(F32)[:, None] * inv_freq[None, :]
    cos, sin = jnp.cos(ang), jnp.sin(ang)
    zeros = lambda w: jnp.zeros((t, w), F32)
    c = jnp.concatenate([cos, cos, jnp.ones((t, HEAD_DIM - ROT_DIM), F32)], axis=1)
    sa = jnp.concatenate([-sin, zeros(HEAD_DIM - half)], axis=1)
    sb = jnp.concatenate([zeros(half), sin, zeros(HEAD_DIM - ROT_DIM)], axis=1)
    return c * scale, sa * scale, sb * scale


def _logf_kernel(x_ref, w_ref, b_ref, o_ref):
    z = _dot(x_ref[...], w_ref[...]) + b_ref[...]
    o_ref[...] = jnp.minimum(z, 0.0) - jnp.log(1.0 + jnp.exp(-jnp.abs(z)))


def _logf(xn, w_pad, b_pad):
    m, d = xn.shape
    tm = _tile(m, 512)
    return pl.pallas_call(
        _logf_kernel,
        out_shape=jax.ShapeDtypeStruct((m, LANES), F32),
        grid=(m // tm,),
        in_specs=[pl.BlockSpec((tm, d), lambda i: (i, 0)),
                  pl.BlockSpec((d, LANES), lambda i: (0, 0)),
                  pl.BlockSpec((1, LANES), lambda i: (0, 0))],
        out_specs=pl.BlockSpec((tm, LANES), lambda i: (i, 0)),
        compiler_params=_cparams(("parallel",)), name="forget_logits",
    )(xn, w_pad, b_pad)


def _cumsum_kernel(x_ref, o_ref, carry_ref):
    @pl.when(pl.program_id(0) == 0)
    def _():
        carry_ref[...] = jnp.zeros_like(carry_ref)

    tb = x_ref.shape[0]
    row = lax.broadcasted_iota(jnp.int32, (tb, tb), 0)
    col = lax.broadcasted_iota(jnp.int32, (tb, tb), 1)
    tri = jnp.where(col <= row, 1.0, 0.0).astype(F32)
    c = jnp.dot(tri, x_ref[...], preferred_element_type=F32,
                precision=lax.Precision.HIGHEST) + carry_ref[...]
    o_ref[...] = c
    carry_ref[...] = c[tb - 1:tb, :]


def _cumsum_rows(x, tb):
    t, c = x.shape
    assert t % tb == 0 and c % LANES == 0
    return pl.pallas_call(
        _cumsum_kernel,
        out_shape=jax.ShapeDtypeStruct((t, c), F32),
        grid=(t // tb,),
        in_specs=[pl.BlockSpec((tb, c), lambda i: (i, 0))],
        out_specs=pl.BlockSpec((tb, c), lambda i: (i, 0)),
        scratch_shapes=[pltpu.VMEM((1, c), F32)],
        compiler_params=_cparams(("arbitrary",)), name="cumsum_rows",
    )(x)


def _lambda_value(lq1, lk1, lq2, lk2, lam_init):
    a = jnp.sum(lq1[...] * lk1[...], axis=-1, keepdims=True)
    b = jnp.sum(lq2[...] * lk2[...], axis=-1, keepdims=True)
    return jnp.exp(a) - jnp.exp(b) + lam_init


def _head_rmsnorm(o, g):
    return o * lax.rsqrt(jnp.mean(o * o, axis=-1, keepdims=True) + RMS_EPS) * g


def _online_update(s, m, l, acc_ref, v):
    m_new = jnp.maximum(m, jnp.max(s, axis=-1, keepdims=True))
    alpha = jnp.exp2(m - m_new)
    p = jnp.exp2(s - m_new)
    l_new = alpha * l + jnp.sum(p, axis=-1, keepdims=True)
    acc_ref[...] = alpha * acc_ref[...] + _dot(p.astype(BF16), v)
    return m_new, l_new


def _causal_sweep(n_full, scores_fn, values_fn, mask_fn, s_a, s_b, m_ref, l_ref, acc_refs):
    n_maps = len(acc_refs)
    for n in range(n_maps):
        m_ref[n] = jnp.full(m_ref.shape[1:], NEG_INF, F32)
        l_ref[n] = jnp.zeros(l_ref.shape[1:], F32)
        acc_refs[n][...] = jnp.zeros_like(acc_refs[n])

    def produce(bufs, j):
        for ref, s in zip(bufs, scores_fn(j)):
            ref[...] = s

    def consume(tiles, j):
        v = values_fn(j)
        for n in range(n_maps):
            m_new, l_new = _online_update(tiles[n], m_ref[n], l_ref[n], acc_refs[n], v)
            m_ref[n] = m_new
            l_ref[n] = l_new

    def pair(p, carry):
        produce(s_b, 2 * p + 1)
        consume([r[...] for r in s_a], 2 * p)
        produce(s_a, 2 * p + 2)
        consume([r[...] for r in s_b], 2 * p + 1)
        return carry

    produce(s_a, 0)
    lax.fori_loop(0, n_full // 2, pair, 0)
    diag = [mask_fn(s) for s in scores_fn(n_full)]

    @pl.when(n_full % 2 == 1)
    def _():
        consume([r[...] for r in s_a], n_full - 1)

    consume(diag, n_full)


def _key_tile(ref, j, tk):
    return ref[0, pl.ds(pl.multiple_of(j * tk, tk), tk), :]


def _sweep_tiles(t):
    tq = _tile(t, 512)
    tk = 2 * tq if t % (2 * tq) == 0 else tq
    assert tq % (1 << CHUNK_SHIFT) == 0
    return tq, tk


def _diag_position(i, tq, tk):
    n_full = (i * tq) // tk
    return n_full, i * tq - n_full * tk


def _diff_prompt_kernel(q_ref, k_ref, v_ref, g_ref, lq1, lk1, lq2, lk2, o_ref,
                        acc1, acc2, a1, a2, b1, b2, m_ref, l_ref, *, tq, tk, lam_init):
    n_full, offset = _diag_position(pl.program_id(1), tq, tk)
    q = q_ref[0]
    q1, q2 = q[:, :HEAD_DIM], q[:, HEAD_DIM:]

    def scores(j):
        k = _key_tile(k_ref, j, tk)
        return _dot_nt(q1, k[:, :HEAD_DIM]), _dot_nt(q2, k[:, HEAD_DIM:])

    def mask(s):
        row = lax.broadcasted_iota(jnp.int32, (tq, tk), 0) + offset
        col = lax.broadcasted_iota(jnp.int32, (tq, tk), 1)
        return jnp.where((col >> CHUNK_SHIFT) <= (row >> CHUNK_SHIFT), s, NEG_INF)

    _causal_sweep(n_full, scores, lambda j: _key_tile(v_ref, j, tk), mask,
                  (a1, a2), (b1, b2), m_ref, l_ref, (acc1, acc2))
    lam = _lambda_value(lq1, lk1, lq2, lk2, lam_init)
    o = acc1[...] / l_ref[0] - lam * (acc2[...] / l_ref[1])
    o_ref[...] = (_head_rmsnorm(o, g_ref[...]) * (1.0 - lam_init)).astype(o_ref.dtype)


def _diff_prompt_attention(q, k, v, g, lams, lam_init):
    nh, t, w = q.shape
    tq, tk = _sweep_tiles(t)
    vec = pl.BlockSpec((1, HEAD_DIM), lambda h, i: (0, 0))
    kern = functools.partial(_diff_prompt_kernel, tq=tq, tk=tk, lam_init=lam_init)
    return pl.pallas_call(
        kern,
        out_shape=jax.ShapeDtypeStruct((t, nh * w), BF16),
        grid=(nh, t // tq),
        in_specs=[pl.BlockSpec((1, tq, w), lambda h, i: (h, i, 0)),
                  pl.BlockSpec((1, t, w), lambda h, i: (h, 0, 0)),
                  pl.BlockSpec((1, t, w), lambda h, i: (h, 0, 0)),
                  pl.BlockSpec((1, w), lambda h, i: (0, 0)),
                  vec, vec, vec, vec],
        out_specs=pl.BlockSpec((tq, w), lambda h, i: (i, h)),
        scratch_shapes=[pltpu.VMEM((tq, w), F32)] * 2 + [pltpu.VMEM((tq, tk), F32)] * 4
        + [pltpu.VMEM((2, tq, 1), F32)] * 2,
        compiler_params=_cparams(("parallel", "arbitrary")), name="diff_prompt_attention",
    )(q, k, v, g, *lams)


def _fox_prompt_kernel(q_ref, k_ref, v_ref, ck_ref, g_ref, o_ref, acc, s_a, s_b, m_ref, l_ref,
                       *, tq, tk):
    n_full, offset = _diag_position(pl.program_id(1), tq, tk)
    q = q_ref[0]

    def scores(j):
        return (_dot_nt(q, _key_tile(k_ref, j, tk)) - ck_ref[0, j] * LOG2E,)

    def mask(s):
        row = lax.broadcasted_iota(jnp.int32, (tq, tk), 0) + offset
        col = lax.broadcasted_iota(jnp.int32, (tq, tk), 1)
        return jnp.where(col <= row, s, NEG_INF)

    _causal_sweep(n_full, scores, lambda j: _key_tile(v_ref, j, tk), mask,
                  (s_a,), (s_b,), m_ref, l_ref, (acc,))
    o_ref[...] = _head_rmsnorm(acc[...] / l_ref[0], g_ref[...]).astype(o_ref.dtype)


def _fox_prompt_attention(q, k, v, ck, g):
    nh, t, w = q.shape
    tq, tk = _sweep_tiles(t)
    assert ck.shape == (nh, t // tk, 1, tk)
    kern = functools.partial(_fox_prompt_kernel, tq=tq, tk=tk)
    return pl.pallas_call(
        kern,
        out_shape=jax.ShapeDtypeStruct((t, nh * w), BF16),
        grid=(nh, t // tq),
        in_specs=[pl.BlockSpec((1, tq, w), lambda h, i: (h, i, 0)),
                  pl.BlockSpec((1, t, w), lambda h, i: (h, 0, 0)),
                  pl.BlockSpec((1, t, w), lambda h, i: (h, 0, 0)),
                  pl.BlockSpec((1, t // tk, 1, tk), lambda h, i: (h, 0, 0, 0)),
                  pl.BlockSpec((1, w), lambda h, i: (0, 0))],
        out_specs=pl.BlockSpec((tq, w), lambda h, i: (i, h)),
        scratch_shapes=[pltpu.VMEM((tq, w), F32)] + [pltpu.VMEM((tq, tk), F32)] * 2
        + [pltpu.VMEM((1, tq, 1), F32)] * 2,
        compiler_params=_cparams(("parallel", "arbitrary")), name="fox_prompt_attention",
    )(q, k, v, ck, g)


def _two_segment_softmax(sp, sn):
    m = jnp.maximum(jnp.max(sp, axis=-1, keepdims=True), jnp.max(sn, axis=-1, keepdims=True))
    pp = jnp.exp2(sp - m)
    pn = jnp.exp2(sn - m)
    l = jnp.sum(pp, axis=-1, keepdims=True) + jnp.sum(pn, axis=-1, keepdims=True)
    return pp.astype(BF16), pn.astype(BF16), l


def _cache_view(cache):
    b, t, nh, w = cache.shape
    nc = w // LANES
    return cache.reshape(b, t, nh, nc, LANES).transpose(0, 1, 3, 2, 4).reshape(b, t * nc * nh, LANES)


def _cache_rows(ref, head, lane_block, n_heads, n_lane_blocks, n_rows):
    stride = n_heads * n_lane_blocks
    return ref[0, pl.ds(lane_block * n_heads + head, n_rows, stride=stride), :].astype(BF16)


def _diff_sample_kernel(q_ref, kn_ref, vn_ref, kp_ref, vp_ref, g_ref, lq1, lk1, lq2, lk2,
                        o_ref, *, past_len, lam_init, n_heads):
    t = q_ref.shape[1]
    w = 2 * HEAD_DIM
    row = lax.broadcasted_iota(jnp.int32, (t, t), 0) + past_len
    col = lax.broadcasted_iota(jnp.int32, (t, t), 1) + past_len
    ok = (col >> CHUNK_SHIFT) <= (row >> CHUNK_SHIFT)
    lam = _lambda_value(lq1, lk1, lq2, lk2, lam_init)
    for h in range(n_heads):
        sl = slice(h * w, (h + 1) * w)
        q = q_ref[0, :, sl]
        kn = kn_ref[0, :, sl]
        vn = vn_ref[0, :, sl]
        vp = [_cache_rows(vp_ref, h, c, n_heads, 2, past_len) for c in range(2)]
        outs = []
        for half in range(2):
            hs = slice(half * HEAD_DIM, (half + 1) * HEAD_DIM)
            sp = _dot_nt(q[:, hs], _cache_rows(kp_ref, h, half, n_heads, 2, past_len))
            sn = jnp.where(ok, _dot_nt(q[:, hs], kn[:, hs]), NEG_INF)
            pp, pn, l = _two_segment_softmax(sp, sn)
            o_past = jnp.concatenate([_dot(pp, vp[0]), _dot(pp, vp[1])], axis=1)
            outs.append((o_past + _dot(pn, vn)) / l)
        o = outs[0] - lam * outs[1]
        o_ref[0, :, sl] = (_head_rmsnorm(o, g_ref[...]) * (1.0 - lam_init)).astype(o_ref.dtype)


def _diff_sample_attention(q, kn, vn, kp, vp, g, lams, lam_init):
    b, t, n = q.shape
    past = kp.shape[1]
    kp, vp = _cache_view(kp), _cache_view(vp)
    new = pl.BlockSpec((1, t, n), lambda i: (i, 0, 0))
    old = pl.BlockSpec((1,) + kp.shape[1:], lambda i: (i, 0, 0))
    vec = pl.BlockSpec((1, HEAD_DIM), lambda i: (0, 0))
    kern = functools.partial(_diff_sample_kernel, past_len=past, lam_init=lam_init,
                             n_heads=n // (2 * HEAD_DIM))
    return pl.pallas_call(
        kern,
        out_shape=jax.ShapeDtypeStruct((b, t, n), BF16),
        grid=(b,),
        in_specs=[new, new, new, old, old,
                  pl.BlockSpec((1, 2 * HEAD_DIM), lambda i: (0, 0)), vec, vec, vec, vec],
        out_specs=new,
        compiler_params=_cparams(("parallel",)), name="diff_sample_attention",
    )(q, kn, vn, kp, vp, g, *lams)


def _fox_sample_kernel(q_ref, kn_ref, vn_ref, kp_ref, vp_ref, ckp_ref, ckn_ref, g_ref, o_ref,
                       *, n_heads):
    t = q_ref.shape[1]
    row = lax.broadcasted_iota(jnp.int32, (t, t), 0)
    col = lax.broadcasted_iota(jnp.int32, (t, t), 1)
    ok = col <= row
    for h in range(n_heads):
        sl = slice(h * HEAD_DIM, (h + 1) * HEAD_DIM)
        q = q_ref[0, :, sl]
        past = ckp_ref.shape[-1]
        kp = _cache_rows(kp_ref, h, 0, n_heads, 1, past)
        vp = _cache_rows(vp_ref, h, 0, n_heads, 1, past)
        sp = _dot_nt(q, kp) - ckp_ref[0, h] * LOG2E
        sn = jnp.where(ok, _dot_nt(q, kn_ref[0, :, sl]) - ckn_ref[0, h] * LOG2E, NEG_INF)
        pp, pn, l = _two_segment_softmax(sp, sn)
        o = (_dot(pp, vp) + _dot(pn, vn_ref[0, :, sl])) / l
        o_ref[0, :, sl] = _head_rmsnorm(o, g_ref[...]).astype(o_ref.dtype)


def _fox_sample_attention(q, kn, vn, kp, vp, ckp, ckn, g):
    b, t, n = q.shape
    past = kp.shape[1]
    nh = n // HEAD_DIM
    kp, vp = _cache_view(kp), _cache_view(vp)
    new = pl.BlockSpec((1, t, n), lambda i: (i, 0, 0))
    old = pl.BlockSpec((1,) + kp.shape[1:], lambda i: (i, 0, 0))
    kern = functools.partial(_fox_sample_kernel, n_heads=nh)
    return pl.pallas_call(
        kern,
        out_shape=jax.ShapeDtypeStruct((b, t, n), BF16),
        grid=(b,),
        in_specs=[new, new, new, old, old,
                  pl.BlockSpec((1, nh, 1, past), lambda i: (i, 0, 0, 0)),
                  pl.BlockSpec((1, nh, 1, t), lambda i: (i, 0, 0, 0)),
                  pl.BlockSpec((1, HEAD_DIM), lambda i: (0, 0))],
        out_specs=new,
        compiler_params=_cparams(("parallel",)), name="fox_sample_attention",
    )(q, kn, vn, kp, vp, ckp, ckn, g)


def _split_bf16(a):
    hi = a.astype(BF16)
    return hi, (a - hi.astype(F32)).astype(BF16)


def _outproj_router_kernel(md_ref, mf_ref, wd_ref, wf_ref, x_ref, g_ref, wr_ref, br_ref,
                           h_ref, hn_ref, idx_ref, gate_ref):
    h = x_ref[...] + _dot(md_ref[...], wd_ref[...]) + _dot(mf_ref[...], wf_ref[...])
    h_ref[...] = h
    hn = h * lax.rsqrt(jnp.mean(h * h, axis=-1, keepdims=True) + RMS_EPS) * g_ref[...]
    for c in range(hn_ref.shape[1]):
        hn_ref[:, c, :] = hn[:, c * LANES:(c + 1) * LANES]
    hn_hi, hn_lo = _split_bf16(hn)
    wr_hi, wr_lo = _split_bf16(wr_ref[...])
    logits = _dot(hn_hi, wr_hi) + (_dot(hn_lo, wr_hi) + _dot(hn_hi, wr_lo)) + br_ref[...]
    lane = lax.broadcasted_iota(jnp.int32, logits.shape, 1)
    work = logits
    idx_out = jnp.zeros(logits.shape, jnp.int32)
    val_out = jnp.zeros(logits.shape, F32)
    top = None
    denom = None
    for k in range(TOP_K):
        mx = jnp.max(work, axis=-1, keepdims=True)
        first = jnp.min(jnp.where(work == mx, lane, LANES), axis=-1, keepdims=True)
        work = jnp.where(lane == first, -jnp.inf, work)
        if k == 0:
            top = mx
        e = jnp.exp(mx - top)
        denom = e if k == 0 else denom + e
        idx_out = jnp.where(lane == k, first, idx_out)
        val_out = jnp.where(lane == k, e, val_out)
    idx_ref[...] = idx_out
    gate_ref[...] = val_out / denom


def _outproj_router(mix_d, mix_f, w_d, w_f, x, g, w_router_pad, b_router_pad):
    m, d = x.shape
    kd, kf = mix_d.shape[1], mix_f.shape[1]
    tm = _tile(m, 256)
    row = lambda w: pl.BlockSpec((tm, w), lambda i: (i, 0))
    full = lambda r, c: pl.BlockSpec((r, c), lambda i: (0, 0))
    return pl.pallas_call(
        _outproj_router_kernel,
        out_shape=[jax.ShapeDtypeStruct((m, d), F32),
                   jax.ShapeDtypeStruct((m, d // LANES, LANES), F32),
                   jax.ShapeDtypeStruct((m, LANES), jnp.int32),
                   jax.ShapeDtypeStruct((m, LANES), F32)],
        grid=(m // tm,),
        in_specs=[row(kd), row(kf), full(kd, d), full(kf, d), row(d), full(1, d),
                  full(d, LANES), full(1, LANES)],
        out_specs=[row(d), pl.BlockSpec((tm, d // LANES, LANES), lambda i: (i, 0, 0)),
                   row(LANES), row(LANES)],
        compiler_params=_cparams(("parallel",)), name="outproj_router",
    )(mix_d, mix_f, w_d, w_f, x, g, w_router_pad, b_router_pad)


def _rank_kernel(idx_ref, rank_ref, count_ref, carry_ref):
    @pl.when(pl.program_id(0) == 0)
    def _():
        carry_ref[...] = jnp.zeros_like(carry_ref)

    tm = idx_ref.shape[0]
    idx = idx_ref[...]
    lane = lax.broadcasted_iota(jnp.int32, (tm, LANES), 1)
    picks = [lane == idx[:, k:k + 1] for k in range(TOP_K)]
    onehot = jnp.zeros((tm, LANES), F32)
    for p in picks:
        onehot = onehot + jnp.where(p, 1.0, 0.0)
    row = lax.broadcasted_iota(jnp.int32, (tm, tm), 0)
    col = lax.broadcasted_iota(jnp.int32, (tm, tm), 1)
    strict = jnp.where(col < row, 1.0, 0.0).astype(BF16)
    before = _dot(strict, onehot.astype(BF16)) + carry_ref[...]
    rank = jnp.zeros((tm, LANES), F32)
    for k, p in enumerate(picks):
        r = jnp.sum(jnp.where(p, before, 0.0), axis=-1, keepdims=True)
        rank = jnp.where(lane == k, r, rank)
    rank_ref[...] = rank.astype(jnp.int32)
    total = before[tm - 1:tm, :] + onehot[tm - 1:tm, :]
    carry_ref[...] = total
    count_ref[...] = total.astype(jnp.int32)


def _expert_ranks(idx):
    n_tok = idx.shape[0]
    tm = _tile(n_tok, 512)
    return pl.pallas_call(
        _rank_kernel,
        out_shape=[jax.ShapeDtypeStruct((n_tok, LANES), jnp.int32),
                   jax.ShapeDtypeStruct((1, LANES), jnp.int32)],
        grid=(n_tok // tm,),
        in_specs=[pl.BlockSpec((tm, LANES), lambda i: (i, 0))],
        out_specs=[pl.BlockSpec((tm, LANES), lambda i: (i, 0)),
                   pl.BlockSpec((1, LANES), lambda i: (0, 0))],
        scratch_shapes=[pltpu.VMEM((1, LANES), F32)],
        compiler_params=_cparams(("arbitrary",)), name="expert_ranks",
    )(idx)


def _row_copy(src_ref, dst_ref, sem, src_row, dst_row):
    return pltpu.make_async_copy(src_ref.at[pl.ds(src_row, 1)], dst_ref.at[pl.ds(dst_row, 1)], sem)


def _gather_rows(idx_ref, src_ref, dst_ref, sem, n_rows):
    assert n_rows % GATHER_UNROLL == 0

    def start(g, carry):
        base = pl.multiple_of(g * GATHER_UNROLL, GATHER_UNROLL)
        for u in range(GATHER_UNROLL):
            _row_copy(src_ref, dst_ref, sem, idx_ref[0, 0, base + u], base + u).start()
        return carry

    def wait(g, carry):
        base = pl.multiple_of(g * GATHER_UNROLL, GATHER_UNROLL)
        for u in range(GATHER_UNROLL):
            _row_copy(src_ref, dst_ref, sem, 0, base + u).wait()
        return carry

    lax.fori_loop(0, n_rows // GATHER_UNROLL, start, 0)
    lax.fori_loop(0, n_rows // GATHER_UNROLL, wait, 0)


def _dispatch_kernel(pad_start_ref, pad_count_ref, dest_ref, hp_ref, hs_ref, xs_ref, zero_buf, sem,
                     *, ts, n_prompt_steps, n_pad):
    i = pl.program_id(0)

    @pl.when(i == 0)
    def _():
        zero_buf[...] = jnp.zeros_like(zero_buf)

        def fill(e, carry):
            def start(r, c):
                pltpu.make_async_copy(zero_buf, xs_ref.at[pad_start_ref[e] + r], sem).start()
                return c

            def wait(r, c):
                pltpu.make_async_copy(zero_buf, xs_ref.at[0], sem).wait()
                return c

            lax.fori_loop(0, pad_count_ref[e], start, 0)
            lax.fori_loop(0, pad_count_ref[e], wait, 0)
            return carry

        lax.fori_loop(0, n_pad, fill, 0)

    def copy_tokens(src_ref):
        per_trip = GATHER_UNROLL // TOP_K

        def start(g, c):
            for u in range(per_trip):
                tok = g * per_trip + u
                for k in range(TOP_K):
                    pltpu.make_async_copy(src_ref.at[tok],
                                          xs_ref.at[dest_ref[0, 0, tok * TOP_K + k]], sem).start()
            return c

        def wait(g, c):
            for _ in range(GATHER_UNROLL):
                pltpu.make_async_copy(src_ref.at[0], xs_ref.at[0], sem).wait()
            return c

        lax.fori_loop(0, ts // per_trip, start, 0)
        lax.fori_loop(0, ts // per_trip, wait, 0)

    @pl.when(i < n_prompt_steps)
    def _():
        copy_tokens(hp_ref)

    @pl.when(i >= n_prompt_steps)
    def _():
        copy_tokens(hs_ref)


def _dispatch(pad_start, pad_count, dest, hn_prompt, hn_sample, n_rows):
    n_p, n_s = hn_prompt.shape[0], hn_sample.shape[0]
    ts = _tile(math.gcd(n_p, n_s), 512)
    assert ts % (GATHER_UNROLL // TOP_K) == 0
    n_steps = (n_p + n_s) // ts
    kern = functools.partial(_dispatch_kernel, ts=ts, n_prompt_steps=n_p // ts,
                             n_pad=pad_start.shape[0])
    slab = hn_prompt.shape[1:]
    last_p = n_p // ts - 1
    return pl.pallas_call(
        kern,
        out_shape=jax.ShapeDtypeStruct((n_rows,) + slab, hn_prompt.dtype),
        grid_spec=pltpu.PrefetchScalarGridSpec(
            num_scalar_prefetch=2,
            grid=(n_steps,),
            in_specs=[pl.BlockSpec((1, 1, ts * TOP_K), lambda i, ps, pc: (i, 0, 0),
                                   memory_space=pltpu.SMEM),
                      pl.BlockSpec((ts,) + slab, lambda i, ps, pc: (jnp.minimum(i, last_p), 0, 0)),
                      pl.BlockSpec((ts,) + slab,
                                   lambda i, ps, pc: (jnp.maximum(i - last_p - 1, 0), 0, 0))],
            out_specs=pl.BlockSpec(memory_space=pl.ANY),
            scratch_shapes=[pltpu.VMEM(hn_prompt.shape[1:], hn_prompt.dtype),
                            pltpu.SemaphoreType.DMA]),
        compiler_params=_cparams(("arbitrary",)), name="moe_dispatch",
    )(pad_start, pad_count, dest.reshape(n_steps, 1, ts * TOP_K), hn_prompt, hn_sample)


def _pack_rows_kernel(x_ref, o_ref):
    rows, d = o_ref.shape
    n_chunks = d // LANES
    for c in range(n_chunks):
        o_ref[:, c * LANES:(c + 1) * LANES] = x_ref[pl.ds(c, rows, stride=n_chunks), :].astype(BF16)


def _pack_rows(xs_slabs):
    n_rows, n_chunks, _ = xs_slabs.shape
    rows = _tile(n_rows, DISPATCH_ROWS)
    return pl.pallas_call(
        _pack_rows_kernel,
        out_shape=jax.ShapeDtypeStruct((n_rows, n_chunks * LANES), BF16),
        grid=(n_rows // rows,),
        in_specs=[pl.BlockSpec((rows * n_chunks, LANES), lambda b: (b, 0))],
        out_specs=pl.BlockSpec((rows, n_chunks * LANES), lambda b: (b, 0)),
        compiler_params=_cparams(("parallel",)), name="moe_pack_rows",
    )(xs_slabs.reshape(n_rows * n_chunks, LANES))


def _combine_kernel(dest_ref, y_ref, h_ref, gate_ref, g_ref, o_ref, buf, sem, *, tm):
    _gather_rows(dest_ref, y_ref, buf, sem, TOP_K * tm)
    gates = gate_ref[...]
    out = h_ref[...]
    for k in range(TOP_K):
        out = out + buf[k * tm:(k + 1) * tm, :] * gates[:, k:k + 1]
    out = out * lax.rsqrt(jnp.mean(out * out, axis=-1, keepdims=True) + RMS_EPS)
    o_ref[...] = out * g_ref[...]


def _combine(dest, y_rows, h, gates, g):
    n_tiles, _, per = dest.shape
    tm = per // TOP_K
    m, d = h.shape
    kern = functools.partial(_combine_kernel, tm=tm)
    return pl.pallas_call(
        kern,
        out_shape=jax.ShapeDtypeStruct((m, d), F32),
        grid=(n_tiles,),
        in_specs=[pl.BlockSpec((1, 1, per), lambda i: (i, 0, 0), memory_space=pltpu.SMEM),
                  pl.BlockSpec(memory_space=pl.ANY),
                  pl.BlockSpec((tm, d), lambda i: (i, 0)),
                  pl.BlockSpec((tm, LANES), lambda i: (i, 0)),
                  pl.BlockSpec((1, d), lambda i: (0, 0))],
        out_specs=pl.BlockSpec((tm, d), lambda i: (i, 0)),
        scratch_shapes=[pltpu.VMEM((per, d), y_rows.dtype), pltpu.SemaphoreType.DMA],
        compiler_params=_cparams(("arbitrary",)), name="moe_combine",
    )(dest, y_rows, h, gates, g)


def _expert_changed(be_ref, rb):
    prev = be_ref[jnp.maximum(rb - 1, 0)]
    return jnp.logical_or(rb == 0, be_ref[rb] != prev)


def _moe_up_kernel(be_ref, used_ref, x_ref, wg_ref, wu_ref, bg_ref, bu_ref, o_ref, wg16, wu16):
    rb = pl.program_id(1)

    @pl.when(_expert_changed(be_ref, rb))
    def _():
        wg16[...] = wg_ref[0].astype(BF16)
        wu16[...] = wu_ref[0].astype(BF16)

    @pl.when(rb < used_ref[0])
    def _():
        x = x_ref[...]
        gate = jnp.minimum(_dot(x, wg16[...]) + bg_ref[0], SWIGLU_LIMIT)
        up = jnp.clip(_dot(x, wu16[...]) + bu_ref[0], -SWIGLU_LIMIT, SWIGLU_LIMIT)
        act = (up + 1.0) * (gate * jax.nn.sigmoid(SWIGLU_ALPHA * gate))
        o_ref[...] = act.astype(o_ref.dtype)

    @pl.when(rb >= used_ref[0])
    def _():
        o_ref[...] = jnp.zeros_like(o_ref)


def _moe_up(block_expert, n_used, xs, w_gate, w_up, b_gate, b_up, tn):
    n_rows, d = xs.shape
    n_exp, _, d_ff = w_gate.shape
    rows = MOE_ROWS
    n_blocks = n_rows // rows
    wspec = pl.BlockSpec((1, d, tn), lambda n, rb, be, used: (be[rb], 0, n))
    bspec = pl.BlockSpec((1, 1, tn), lambda n, rb, be, used: (be[rb], 0, n))
    return pl.pallas_call(
        _moe_up_kernel,
        out_shape=jax.ShapeDtypeStruct((n_rows, d_ff), BF16),
        grid_spec=pltpu.PrefetchScalarGridSpec(
            num_scalar_prefetch=2,
            grid=(d_ff // tn, n_blocks),
            in_specs=[pl.BlockSpec((rows, d), lambda n, rb, be, used: (rb, 0)),
                      wspec, wspec, bspec, bspec],
            out_specs=pl.BlockSpec((rows, tn), lambda n, rb, be, used: (rb, n)),
            scratch_shapes=[pltpu.VMEM((d, tn), BF16), pltpu.VMEM((d, tn), BF16)]),
        compiler_params=_cparams(("arbitrary", "arbitrary")), name="moe_gate_up",
    )(block_expert, n_used, xs, w_gate, w_up, b_gate.reshape(n_exp, 1, d_ff),
      b_up.reshape(n_exp, 1, d_ff))


def _moe_down_kernel(be_ref, used_ref, a_ref, w_ref, b_ref, o_ref, w16):
    rb = pl.program_id(1)

    @pl.when(_expert_changed(be_ref, rb))
    def _():
        w16[...] = w_ref[0].astype(BF16)

    @pl.when(rb < used_ref[0])
    def _():
        o_ref[...] = _dot(a_ref[...], w16[...]) + b_ref[0]

    @pl.when(rb >= used_ref[0])
    def _():
        o_ref[...] = jnp.zeros_like(o_ref)


def _moe_down(block_expert, n_used, act, w_down, b_down, tn):
    n_rows, d_ff = act.shape
    n_exp, _, d = w_down.shape
    rows = MOE_ROWS
    return pl.pallas_call(
        _moe_down_kernel,
        out_shape=jax.ShapeDtypeStruct((n_rows, d), F32),
        grid_spec=pltpu.PrefetchScalarGridSpec(
            num_scalar_prefetch=2,
            grid=(d // tn, n_rows // rows),
            in_specs=[pl.BlockSpec((rows, d_ff), lambda n, rb, be, used: (rb, 0)),
                      pl.BlockSpec((1, d_ff, tn), lambda n, rb, be, used: (be[rb], 0, n)),
                      pl.BlockSpec((1, 1, tn), lambda n, rb, be, used: (be[rb], 0, n))],
            out_specs=pl.BlockSpec((rows, tn), lambda n, rb, be, used: (rb, n)),
            scratch_shapes=[pltpu.VMEM((d_ff, tn), BF16)]),
        compiler_params=_cparams(("arbitrary", "arbitrary")), name="moe_down",
    )(block_expert, n_used, act, w_down, b_down.reshape(n_exp, 1, d))


def _pad_lanes(a, fill=0.0):
    return jnp.pad(a, ((0, 0), (0, LANES - a.shape[1])), constant_values=fill)


def _mixer_projections(x2d, pos, norm_g, w_in, b_forget, head_major):
    d = x2d.shape[1]
    dw = (w_in.shape[1] - 8) // 6
    q_scale = HEAD_DIM ** -0.5 * LOG2E
    xn = _rmsnorm_bf16(x2d, norm_g)
    w16 = w_in.astype(BF16)
    wcol = lambda j: w16[:, j * dw:(j + 1) * dw]
    tabs_q = _rope_tables(pos, q_scale)
    tabs_k = _rope_tables(pos, 1.0)
    hm = dict(head_major=head_major)
    (dq,) = _proj(xn, wcol(0), rope_tabs=tabs_q, head_w=2 * HEAD_DIM, **hm)
    dk32, dk = _proj(xn, wcol(1), rope_tabs=tabs_k, want_f32=True, head_w=2 * HEAD_DIM, **hm)
    dv32, dv = _proj(xn, wcol(2), want_f32=True, head_w=2 * HEAD_DIM, **hm)
    (fq,) = _proj(xn, wcol(3), scale=q_scale, **hm)
    fk32, fk = _proj(xn, wcol(4), want_f32=True, **hm)
    fv32, fv = _proj(xn, wcol(5), want_f32=True, **hm)
    logf = _logf(xn, _pad_lanes(w16[:, 6 * dw:]), _pad_lanes(b_forget.reshape(1, -1)))
    return (dq, dk, dv, fq, fk, fv), (dk32, dv32, fk32, fv32), logf


def _moe(hn_prompt, hn_sample, idx_all, w_gate, w_up, w_down, b_gate, b_up, b_down):
    n_tok = idx_all.shape[0]
    n_exp = w_gate.shape[0]
    rows = MOE_ROWS
    n_blocks = -(-(n_tok * TOP_K) // rows) + n_exp
    n_rows = n_blocks * rows

    rank, counts = _expert_ranks(idx_all)
    experts = idx_all[:, :TOP_K]
    counts = counts[0, :n_exp]
    padded = (counts + rows - 1) // rows * rows
    padded_end = jnp.cumsum(padded)
    padded_start = padded_end - padded
    dest = padded_start[experts] + rank[:, :TOP_K]
    pad_start = jnp.concatenate([padded_start + counts, padded_end[-1:]]).astype(jnp.int32)
    pad_count = jnp.concatenate([padded - counts, n_rows - padded_end[-1:]]).astype(jnp.int32)
    block_start = jnp.arange(n_blocks, dtype=jnp.int32) * rows
    block_expert = jnp.minimum(
        jnp.sum((padded_end[None, :] <= block_start[:, None]).astype(jnp.int32), axis=1), n_exp - 1)
    n_used = (padded_end[-1] // rows).astype(jnp.int32).reshape(1)

    xs = _pack_rows(_dispatch(pad_start, pad_count, dest, hn_prompt, hn_sample, n_rows))
    act = _moe_up(block_expert, n_used, xs, w_gate, w_up, b_gate, b_up,
                  tn=min(1024, w_gate.shape[2]))
    y_rows = _moe_down(block_expert, n_used, act, w_down, b_down, tn=min(1024, w_down.shape[2]))
    return y_rows, dest


def _combine_tokens(dest, y_rows, h, gates, g):
    m = h.shape[0]
    tm = _tile(m, 256)
    dest_t = dest.reshape(m // tm, tm, TOP_K).transpose(0, 2, 1).reshape(m // tm, 1, TOP_K * tm)
    return _combine(dest_t, y_rows, h, gates, g.reshape(1, -1))


def kernel(x_prompt, x_sample, cache_diff_k, cache_diff_v, cache_fox_k, cache_fox_v, cache_fox_logf,
           attn_norm_g, w_in, b_forget, lambda_q1, lambda_k1, lambda_q2, lambda_k2,
           diff_out_norm_g, fox_out_norm_g, w_out, ffn_norm_g, w_router, b_router,
           w_gate, b_gate, w_up, b_up, w_down, b_down, final_norm_g):
    depth = w_in.shape[0]
    assert depth == 1, "single-layer stack"
    lam_init = 0.8 - 0.6 * math.exp(-0.3 * 0)
    drop = lambda a: a.reshape(a.shape[1:])
    (cache_diff_k, cache_diff_v, cache_fox_k, cache_fox_v, cache_fox_logf, attn_norm_g, w_in,
     b_forget, lambda_q1, lambda_k1, lambda_q2, lambda_k2, diff_out_norm_g, fox_out_norm_g, w_out,
     ffn_norm_g, w_router, b_router, w_gate, b_gate, w_up, b_up, w_down, b_down) = map(drop, (
         cache_diff_k, cache_diff_v, cache_fox_k, cache_fox_v, cache_fox_logf, attn_norm_g, w_in,
         b_forget, lambda_q1, lambda_k1, lambda_q2, lambda_k2, diff_out_norm_g, fox_out_norm_g,
         w_out, ffn_norm_g, w_router, b_router, w_gate, b_gate, w_up, b_up, w_down, b_down))
    bp, t, d = x_prompt.shape
    assert bp == 1
    bs, ts, _ = x_sample.shape
    past = cache_diff_k.shape[1]
    n_dh = cache_diff_k.shape[2]
    n_fh = cache_fox_k.shape[2]
    dw = n_dh * 2 * HEAD_DIM
    fw = n_fh * HEAD_DIM
    lams = [v.reshape(1, HEAD_DIM) for v in (lambda_q1, lambda_k1, lambda_q2, lambda_k2)]
    diff_g = diff_out_norm_g.reshape(1, -1)
    fox_g = fox_out_norm_g.reshape(1, -1)

    xp = x_prompt.reshape(t, d)
    pos_p = jnp.arange(t, dtype=jnp.int32)
    (dq, dk, dv, fq, fk, fv), cache_p, logf_p = _mixer_projections(
        xp, pos_p, attn_norm_g, w_in, b_forget, head_major=True)
    _, tk = _sweep_tiles(t)
    c_p = _cumsum_rows(logf_p, _tile(t, 512))
    ck_p = c_p[:, :n_fh].T.reshape(n_fh, t // tk, 1, tk)
    mix_d_p = _diff_prompt_attention(dq, dk, dv, diff_g, lams, lam_init)
    mix_f_p = _fox_prompt_attention(fq, fk, fv, ck_p, fox_g)

    xs = x_sample.reshape(bs * ts, d)
    pos_s = past + jnp.arange(ts, dtype=jnp.int32)
    (sdq, sdk, sdv, sfq, sfk, sfv), cache_s, logf_s = _mixer_projections(
        xs, jnp.tile(pos_s, bs), attn_norm_g, w_in, b_forget, head_major=False)
    logf_all = jnp.concatenate([cache_fox_logf.astype(F32),
                                logf_s[:, :n_fh].reshape(bs, ts, n_fh)], axis=1)
    tt = past + ts
    lf_cols = logf_all.transpose(1, 0, 2).reshape(tt, bs * n_fh)
    pad_c = (-lf_cols.shape[1]) % LANES
    lf_cols = jnp.pad(lf_cols, ((0, 0), (0, pad_c)))
    tb = max(tb_ for tb_ in range(8, 513, 8) if tt % tb_ == 0)
    c_s = _cumsum_rows(lf_cols, tb)[:, :bs * n_fh].reshape(tt, bs, n_fh).transpose(1, 2, 0)
    ckp_s = c_s[:, :, None, :past]
    ckn_s = c_s[:, :, None, past:]
    r3 = lambda a: a.reshape(bs, ts, -1)
    mix_d_s = _diff_sample_attention(
        r3(sdq), r3(sdk), r3(sdv), cache_diff_k, cache_diff_v,
        diff_g, lams, lam_init).reshape(bs * ts, dw)
    mix_f_s = _fox_sample_attention(
        r3(sfq), r3(sfk), r3(sfv), cache_fox_k, cache_fox_v,
        ckp_s, ckn_s, fox_g).reshape(bs * ts, fw)

    w_out16 = w_out.astype(BF16)
    w_router_pad = _pad_lanes(w_router)
    b_router_pad = _pad_lanes(b_router.reshape(1, -1), fill=NEG_INF)
    ffn_g = ffn_norm_g.reshape(1, -1)
    h_p, hn_p, idx_p, gate_p = _outproj_router(mix_d_p, mix_f_p, w_out16[:dw], w_out16[dw:],
                                               xp, ffn_g, w_router_pad, b_router_pad)
    h_s, hn_s, idx_s, gate_s = _outproj_router(mix_d_s, mix_f_s, w_out16[:dw], w_out16[dw:],
                                               xs, ffn_g, w_router_pad, b_router_pad)

    idx_all = jnp.concatenate([idx_p, idx_s], axis=0)
    y_rows, dest = _moe(hn_p, hn_s, idx_all, w_gate, w_up, w_down, b_gate, b_up, b_down)
    y_p = _combine_tokens(dest[:t], y_rows, h_p, gate_p, final_norm_g)
    y_s = _combine_tokens(dest[t:], y_rows, h_s, gate_s, final_norm_g)

    dk32, dv32, fk32, fv32 = cache_p
    sdk32, sdv32, sfk32, sfv32 = cache_s
    return (y_p.reshape(1, t, d), y_s.reshape(bs, ts, d),
            dk32.reshape(1, 1, t, n_dh, 2 * HEAD_DIM), dv32.reshape(1, 1, t, n_dh, 2 * HEAD_DIM),
            fk32.reshape(1, 1, t, n_fh, HEAD_DIM), fv32.reshape(1, 1, t, n_fh, HEAD_DIM),
            logf_p[:, :n_fh].reshape(1, 1, t, n_fh),
            sdk32.reshape(1, bs, ts, n_dh, 2 * HEAD_DIM), sdv32.reshape(1, bs, ts, n_dh, 2 * HEAD_DIM),
            sfk32.reshape(1, bs, ts, n_fh, HEAD_DIM), sfv32.reshape(1, bs, ts, n_fh, HEAD_DIM),
            logf_s[:, :n_fh].reshape(1, bs, ts, n_fh))
```

```python
import functools
import math

import jax
import jax.numpy as jnp
from jax import lax
from jax.experimental import pallas as pl
from jax.experimental.pallas import tpu as pltpu

HEAD_DIM = 128
CHUNK_SHIFT = 6
ROT_DIM = HEAD_DIM // 4
ROPE_THETA = 500000.0
TOP_K = 4
SWIGLU_LIMIT = 7.0
SWIGLU_ALPHA = 1.702
RMS_EPS = 1e-6
NEG_INF = -1e30
LOG2E = 1.4426950408889634
LANES = 128
MOE_ROWS = 512
GATHER_UNROLL = 8
VMEM_LIMIT = 56 * 1024 * 1024

F32 = jnp.float32
BF16 = jnp.bfloat16


def _cparams(sem, vmem=VMEM_LIMIT):
    return pltpu.CompilerParams(dimension_semantics=sem, vmem_limit_bytes=vmem)


def _tile(n, pref):
    if n <= pref:
        return n
    t = max(c for c in range(8, pref + 1, 8) if n % c == 0)
    return t


def _dot_nt(a, b):
    return lax.dot_general(a, b, (((1,), (1,)), ((), ())), preferred_element_type=F32)


def _dot(a, b):
    return jnp.dot(a, b, preferred_element_type=F32)


def _rmsnorm_kernel(x_ref, g_ref, o_ref):
    x = x_ref[...]
    y = x * lax.rsqrt(jnp.mean(x * x, axis=-1, keepdims=True) + RMS_EPS)
    o_ref[...] = (y * g_ref[...]).astype(o_ref.dtype)


def _rmsnorm_bf16(x, g):
    m, d = x.shape
    tm = _tile(m, 512)
    return pl.pallas_call(
        _rmsnorm_kernel,
        out_shape=jax.ShapeDtypeStruct((m, d), BF16),
        grid=(m // tm,),
        in_specs=[pl.BlockSpec((tm, d), lambda i: (i, 0)),
                  pl.BlockSpec((1, d), lambda i: (0, 0))],
        out_specs=pl.BlockSpec((tm, d), lambda i: (i, 0)),
        compiler_params=_cparams(("parallel",)),
        name="rmsnorm_bf16",
    )(x, g.reshape(1, d))


def _proj_kernel(*refs, rope, scale, want_f32, want_bf16, head_major, head_w):
    it = iter(refs)
    x_ref = next(it)
    w_ref = next(it)
    if rope:
        c_ref, sa_ref, sb_ref = next(it), next(it), next(it)
    o32_ref = next(it) if want_f32 else None
    o16_ref = next(it) if want_bf16 else None

    n = w_ref.shape[1]
    y = _dot(x_ref[...], w_ref[...])
    for h in range(n // HEAD_DIM):
        sl = slice(h * HEAD_DIM, (h + 1) * HEAD_DIM)
        yh = y[:, sl]
        if rope:
            yh = (yh * c_ref[...]
                  + pltpu.roll(yh, HEAD_DIM - ROT_DIM // 2, 1) * sa_ref[...]
                  + pltpu.roll(yh, ROT_DIM // 2, 1) * sb_ref[...])
        elif scale != 1.0:
            yh = yh * scale
        per = head_w // HEAD_DIM
        if want_f32:
            o32_ref[:, h // per, (h % per) * HEAD_DIM:(h % per + 1) * HEAD_DIM] = yh
        if want_bf16:
            if head_major:
                o16_ref[h // per, :, (h % per) * HEAD_DIM:(h % per + 1) * HEAD_DIM] = yh.astype(BF16)
            else:
                o16_ref[:, sl] = yh.astype(BF16)


def _proj(xn, w, rope_tabs=None, scale=1.0, want_f32=False, want_bf16=True,
          head_major=False, head_w=HEAD_DIM):
    m, d = xn.shape
    n = w.shape[1]
    tm = _tile(m, 512)
    rope = rope_tabs is not None
    in_specs = [pl.BlockSpec((tm, d), lambda i: (i, 0)),
                pl.BlockSpec((d, n), lambda i: (0, 0))]
    args = [xn, w]
    if rope:
        in_specs += [pl.BlockSpec((tm, HEAD_DIM), lambda i: (i, 0))] * 3
        args += list(rope_tabs)
    out_shape, out_specs = [], []
    if want_f32:
        out_shape.append(jax.ShapeDtypeStruct((m, n // head_w, head_w), F32))
        out_specs.append(pl.BlockSpec((tm, n // head_w, head_w), lambda i: (i, 0, 0)))
    if want_bf16:
        if head_major:
            nh = n // head_w
            out_shape.append(jax.ShapeDtypeStruct((nh, m, head_w), BF16))
            out_specs.append(pl.BlockSpec((nh, tm, head_w), lambda i: (0, i, 0)))
        else:
            out_shape.append(jax.ShapeDtypeStruct((m, n), BF16))
            out_specs.append(pl.BlockSpec((tm, n), lambda i: (i, 0)))
    kern = functools.partial(_proj_kernel, rope=rope, scale=scale, want_f32=want_f32,
                             want_bf16=want_bf16, head_major=head_major, head_w=head_w)
    return pl.pallas_call(
        kern, out_shape=out_shape, grid=(m // tm,), in_specs=in_specs, out_specs=out_specs,
        compiler_params=_cparams(("parallel",)), name="in_proj",
    )(*args)


def _rope_tables(pos, scale):
    half = ROT_DIM // 2
    t = pos.shape[0]
    inv_freq = jnp.power(ROPE_THETA, -jnp.arange(half, dtype=F32) * (2.0 / ROT_DIM))
    ang = pos.astype(F32)[:, None] * inv_freq[None, :]
    cos, sin = jnp.cos(ang), jnp.sin(ang)
    zeros = lambda w: jnp.zeros((t, w), F32)
    c = jnp.concatenate([cos, cos, jnp.ones((t, HEAD_DIM - ROT_DIM), F32)], axis=1)
    sa = jnp.concatenate([-sin, zeros(HEAD_DIM - half)], axis=1)
    sb = jnp.concatenate([zeros(half), sin, zeros(HEAD_DIM - ROT_DIM)], axis=1)
    return c * scale, sa * scale, sb * scale


def _logf_kernel(x_ref, w_ref, b_ref, o_ref):
    z = _dot(x_ref[...], w_ref[...]) + b_ref[...]
    o_ref[...] = jnp.minimum(z, 0.0) - jnp.log(1.0 + jnp.exp(-jnp.abs(z)))


def _logf(xn, w_pad, b_pad):
    m, d = xn.shape
    tm = _tile(m, 512)
    return pl.pallas_call(
        _logf_kernel,
        out_shape=jax.ShapeDtypeStruct((m, LANES), F32),
        grid=(m // tm,),
        in_specs=[pl.BlockSpec((tm, d), lambda i: (i, 0)),
                  pl.BlockSpec((d, LANES), lambda i: (0, 0)),
                  pl.BlockSpec((1, LANES), lambda i: (0, 0))],
        out_specs=pl.BlockSpec((tm, LANES), lambda i: (i, 0)),
        compiler_params=_cparams(("parallel",)), name="forget_logits",
    )(xn, w_pad, b_pad)


def _cumsum_kernel(x_ref, o_ref, carry_ref):
    @pl.when(pl.program_id(0) == 0)
    def _():
        carry_ref[...] = jnp.zeros_like(carry_ref)

    tb = x_ref.shape[0]
    row = lax.broadcasted_iota(jnp.int32, (tb, tb), 0)
    col = lax.broadcasted_iota(jnp.int32, (tb, tb), 1)
    tri = jnp.where(col <= row, 1.0, 0.0).astype(F32)
    c = jnp.dot(tri, x_ref[...], preferred_element_type=F32,
                precision=lax.Precision.HIGHEST) + carry_ref[...]
    o_ref[...] = c
    carry_ref[...] = c[tb - 1:tb, :]


def _cumsum_rows(x, tb):
    t, c = x.shape
    assert t % tb == 0 and c % LANES == 0
    return pl.pallas_call(
        _cumsum_kernel,
        out_shape=jax.ShapeDtypeStruct((t, c), F32),
        grid=(t // tb,),
        in_specs=[pl.BlockSpec((tb, c), lambda i: (i, 0))],
        out_specs=pl.BlockSpec((tb, c), lambda i: (i, 0)),
        scratch_shapes=[pltpu.VMEM((1, c), F32)],
        compiler_params=_cparams(("arbitrary",)), name="cumsum_rows",
    )(x)


def _lambda_value(lq1, lk1, lq2, lk2, lam_init):
    a = jnp.sum(lq1[...] * lk1[...], axis=-1, keepdims=True)
    b = jnp.sum(lq2[...] * lk2[...], axis=-1, keepdims=True)
    return jnp.exp(a) - jnp.exp(b) + lam_init


def _head_rmsnorm(o, g):
    return o * lax.rsqrt(jnp.mean(o * o, axis=-1, keepdims=True) + RMS_EPS) * g


def _online_update(s, m, l, acc_ref, v):
    m_new = jnp.maximum(m, jnp.max(s, axis=-1, keepdims=True))
    alpha = jnp.exp2(m - m_new)
    p = jnp.exp2(s - m_new)
    l_new = alpha * l + jnp.sum(p, axis=-1, keepdims=True)
    acc_ref[...] = alpha * acc_ref[...] + _dot(p.astype(BF16), v)
    return m_new, l_new


def _causal_sweep(n_full, scores_fn, values_fn, mask_fn, s_a, s_b, m_ref, l_ref, acc_refs):
    n_maps = len(acc_refs)
    for n in range(n_maps):
        m_ref[n] = jnp.full(m_ref.shape[1:], NEG_INF, F32)
        l_ref[n] = jnp.zeros(l_ref.shape[1:], F32)
        acc_refs[n][...] = jnp.zeros_like(acc_refs[n])

    def produce(bufs, j):
        for ref, s in zip(bufs, scores_fn(j)):
            ref[...] = s

    def consume(tiles, j):
        v = values_fn(j)
        for n in range(n_maps):
            m_new, l_new = _online_update(tiles[n], m_ref[n], l_ref[n], acc_refs[n], v)
            m_ref[n] = m_new
            l_ref[n] = l_new

    def pair(p, carry):
        produce(s_b, 2 * p + 1)
        consume([r[...] for r in s_a], 2 * p)
        produce(s_a, 2 * p + 2)
        consume([r[...] for r in s_b], 2 * p + 1)
        return carry

    produce(s_a, 0)
    lax.fori_loop(0, n_full // 2, pair, 0)
    diag = [mask_fn(s) for s in scores_fn(n_full)]

    @pl.when(n_full % 2 == 1)
    def _():
        consume([r[...] for r in s_a], n_full - 1)

    consume(diag, n_full)


def _key_tile(ref, j, tk):
    return ref[0, pl.ds(pl.multiple_of(j * tk, tk), tk), :]


def _sweep_tiles(t):
    tq = _tile(t, 512)
    tk = 2 * tq if t % (2 * tq) == 0 else tq
    assert tq % (1 << CHUNK_SHIFT) == 0
    return tq, tk


def _diag_position(i, tq, tk):
    n_full = (i * tq) // tk
    return n_full, i * tq - n_full * tk


def _diff_prompt_kernel(q_ref, k_ref, v_ref, g_ref, lq1, lk1, lq2, lk2, o_ref,
                        acc1, acc2, a1, a2, b1, b2, m_ref, l_ref, *, tq, tk, lam_init):
    n_full, offset = _diag_position(pl.program_id(1), tq, tk)
    q = q_ref[0]
    q1, q2 = q[:, :HEAD_DIM], q[:, HEAD_DIM:]

    def scores(j):
        k = _key_tile(k_ref, j, tk)
        return _dot_nt(q1, k[:, :HEAD_DIM]), _dot_nt(q2, k[:, HEAD_DIM:])

    def mask(s):
        row = lax.broadcasted_iota(jnp.int32, (tq, tk), 0) + offset
        col = lax.broadcasted_iota(jnp.int32, (tq, tk), 1)
        return jnp.where((col >> CHUNK_SHIFT) <= (row >> CHUNK_SHIFT), s, NEG_INF)

    _causal_sweep(n_full, scores, lambda j: _key_tile(v_ref, j, tk), mask,
                  (a1, a2), (b1, b2), m_ref, l_ref, (acc1, acc2))
    lam = _lambda_value(lq1, lk1, lq2, lk2, lam_init)
    o = acc1[...] / l_ref[0] - lam * (acc2[...] / l_ref[1])
    o_ref[...] = (_head_rmsnorm(o, g_ref[...]) * (1.0 - lam_init)).astype(o_ref.dtype)


def _diff_prompt_attention(q, k, v, g, lams, lam_init):
    nh, t, w = q.shape
    tq, tk = _sweep_tiles(t)
    vec = pl.BlockSpec((1, HEAD_DIM), lambda h, i: (0, 0))
    kern = functools.partial(_diff_prompt_kernel, tq=tq, tk=tk, lam_init=lam_init)
    return pl.pallas_call(
        kern,
        out_shape=jax.ShapeDtypeStruct((t, nh * w), BF16),
        grid=(nh, t // tq),
        in_specs=[pl.BlockSpec((1, tq, w), lambda h, i: (h, i, 0)),
                  pl.BlockSpec((1, t, w), lambda h, i: (h, 0, 0)),
                  pl.BlockSpec((1, t, w), lambda h, i: (h, 0, 0)),
                  pl.BlockSpec((1, w), lambda h, i: (0, 0)),
                  vec, vec, vec, vec],
        out_specs=pl.BlockSpec((tq, w), lambda h, i: (i, h)),
        scratch_shapes=[pltpu.VMEM((tq, w), F32)] * 2 + [pltpu.VMEM((tq, tk), F32)] * 4
        + [pltpu.VMEM((2, tq, 1), F32)] * 2,
        compiler_params=_cparams(("parallel", "arbitrary")), name="diff_prompt_attention",
    )(q, k, v, g, *lams)


def _fox_prompt_kernel(q_ref, k_ref, v_ref, ck_ref, g_ref, o_ref, acc, s_a, s_b, m_ref, l_ref,
                       *, tq, tk):
    n_full, offset = _diag_position(pl.program_id(1), tq, tk)
    q = q_ref[0]

    def scores(j):
        return (_dot_nt(q, _key_tile(k_ref, j, tk)) - ck_ref[0, j] * LOG2E,)

    def mask(s):
        row = lax.broadcasted_iota(jnp.int32, (tq, tk), 0) + offset
        col = lax.broadcasted_iota(jnp.int32, (tq, tk), 1)
        return jnp.where(col <= row, s, NEG_INF)

    _causal_sweep(n_full, scores, lambda j: _key_tile(v_ref, j, tk), mask,
                  (s_a,), (s_b,), m_ref, l_ref, (acc,))
    o_ref[...] = _head_rmsnorm(acc[...] / l_ref[0], g_ref[...]).astype(o_ref.dtype)


def _fox_prompt_attention(q, k, v, ck, g):
    nh, t, w = q.shape
    tq, tk = _sweep_tiles(t)
    assert ck.shape == (nh, t // tk, 1, tk)
    kern = functools.partial(_fox_prompt_kernel, tq=tq, tk=tk)
    return pl.pallas_call(
        kern,
        out_shape=jax.ShapeDtypeStruct((t, nh * w), BF16),
        grid=(nh, t // tq),
        in_specs=[pl.BlockSpec((1, tq, w), lambda h, i: (h, i, 0)),
                  pl.BlockSpec((1, t, w), lambda h, i: (h, 0, 0)),
                  pl.BlockSpec((1, t, w), lambda h, i: (h, 0, 0)),
                  pl.BlockSpec((1, t // tk, 1, tk), lambda h, i: (h, 0, 0, 0)),
                  pl.BlockSpec((1, w), lambda h, i: (0, 0))],
        out_specs=pl.BlockSpec((tq, w), lambda h, i: (i, h)),
        scratch_shapes=[pltpu.VMEM((tq, w), F32)] + [pltpu.VMEM((tq, tk), F32)] * 2
        + [pltpu.VMEM((1, tq, 1), F32)] * 2,
        compiler_params=_cparams(("parallel", "arbitrary")), name="fox_prompt_attention",
    )(q, k, v, ck, g)


def _two_segment_softmax(sp, sn):
    m = jnp.maximum(jnp.max(sp, axis=-1, keepdims=True), jnp.max(sn, axis=-1, keepdims=True))
    pp = jnp.exp2(sp - m)
    pn = jnp.exp2(sn - m)
    l = jnp.sum(pp, axis=-1, keepdims=True) + jnp.sum(pn, axis=-1, keepdims=True)
    return pp.astype(BF16), pn.astype(BF16), l


def _cache_view(cache):
    b, t, nh, w = cache.shape
    nc = w // LANES
    return cache.reshape(b, t, nh, nc, LANES).transpose(0, 1, 3, 2, 4).reshape(b, t * nc * nh, LANES)


def _cache_rows(ref, head, lane_block, n_heads, n_lane_blocks, n_rows):
    stride = n_heads * n_lane_blocks
    return ref[0, pl.ds(lane_block * n_heads + head, n_rows, stride=stride), :].astype(BF16)


def _diff_sample_kernel(q_ref, kn_ref, vn_ref, kp_ref, vp_ref, g_ref, lq1, lk1, lq2, lk2,
                        o_ref, *, past_len, lam_init, n_heads):
    t = q_ref.shape[1]
    w = 2 * HEAD_DIM
    row = lax.broadcasted_iota(jnp.int32, (t, t), 0) + past_len
    col = lax.broadcasted_iota(jnp.int32, (t, t), 1) + past_len
    ok = (col >> CHUNK_SHIFT) <= (row >> CHUNK_SHIFT)
    lam = _lambda_value(lq1, lk1, lq2, lk2, lam_init)
    for h in range(n_heads):
        sl = slice(h * w, (h + 1) * w)
        q = q_ref[0, :, sl]
        kn = kn_ref[0, :, sl]
        vn = vn_ref[0, :, sl]
        vp = [_cache_rows(vp_ref, h, c, n_heads, 2, past_len) for c in range(2)]
        outs = []
        for half in range(2):
            hs = slice(half * HEAD_DIM, (half + 1) * HEAD_DIM)
            sp = _dot_nt(q[:, hs], _cache_rows(kp_ref, h, half, n_heads, 2, past_len))
            sn = jnp.where(ok, _dot_nt(q[:, hs], kn[:, hs]), NEG_INF)
            pp, pn, l = _two_segment_softmax(sp, sn)
            o_past = jnp.concatenate([_dot(pp, vp[0]), _dot(pp, vp[1])], axis=1)
            outs.append((o_past + _dot(pn, vn)) / l)
        o = outs[0] - lam * outs[1]
        o_ref[0, :, sl] = (_head_rmsnorm(o, g_ref[...]) * (1.0 - lam_init)).astype(o_ref.dtype)


def _diff_sample_attention(q, kn, vn, kp, vp, g, lams, lam_init):
    b, t, n = q.shape
    past = kp.shape[1]
    kp, vp = _cache_view(kp), _cache_view(vp)
    new = pl.BlockSpec((1, t, n), lambda i: (i, 0, 0))
    old = pl.BlockSpec((1,) + kp.shape[1:], lambda i: (i, 0, 0))
    vec = pl.BlockSpec((1, HEAD_DIM), lambda i: (0, 0))
    kern = functools.partial(_diff_sample_kernel, past_len=past, lam_init=lam_init,
                             n_heads=n // (2 * HEAD_DIM))
    return pl.pallas_call(
        kern,
        out_shape=jax.ShapeDtypeStruct((b, t, n), BF16),
        grid=(b,),
        in_specs=[new, new, new, old, old,
                  pl.BlockSpec((1, 2 * HEAD_DIM), lambda i: (0, 0)), vec, vec, vec, vec],
        out_specs=new,
        compiler_params=_cparams(("parallel",)), name="diff_sample_attention",
    )(q, kn, vn, kp, vp, g, *lams)


def _fox_sample_kernel(q_ref, kn_ref, vn_ref, kp_ref, vp_ref, ckp_ref, ckn_ref, g_ref, o_ref,
                       *, n_heads):
    t = q_ref.shape[1]
    row = lax.broadcasted_iota(jnp.int32, (t, t), 0)
    col = lax.broadcasted_iota(jnp.int32, (t, t), 1)
    ok = col <= row
    for h in range(n_heads):
        sl = slice(h * HEAD_DIM, (h + 1) * HEAD_DIM)
        q = q_ref[0, :, sl]
        past = ckp_ref.shape[-1]
        kp = _cache_rows(kp_ref, h, 0, n_heads, 1, past)
        vp = _cache_rows(vp_ref, h, 0, n_heads, 1, past)
        sp = _dot_nt(q, kp) - ckp_ref[0, h] * LOG2E
        sn = jnp.where(ok, _dot_nt(q, kn_ref[0, :, sl]) - ckn_ref[0, h] * LOG2E, NEG_INF)
        pp, pn, l = _two_segment_softmax(sp, sn)
        o = (_dot(pp, vp) + _dot(pn, vn_ref[0, :, sl])) / l
        o_ref[0, :, sl] = _head_rmsnorm(o, g_ref[...]).astype(o_ref.dtype)


def _fox_sample_attention(q, kn, vn, kp, vp, ckp, ckn, g):
    b, t, n = q.shape
    past = kp.shape[1]
    nh = n // HEAD_DIM
    kp, vp = _cache_view(kp), _cache_view(vp)
    new = pl.BlockSpec((1, t, n), lambda i: (i, 0, 0))
    old = pl.BlockSpec((1,) + kp.shape[1:], lambda i: (i, 0, 0))
    kern = functools.partial(_fox_sample_kernel, n_heads=nh)
    return pl.pallas_call(
        kern,
        out_shape=jax.ShapeDtypeStruct((b, t, n), BF16),
        grid=(b,),
        in_specs=[new, new, new, old, old,
                  pl.BlockSpec((1, nh, 1, past), lambda i: (i, 0, 0, 0)),
                  pl.BlockSpec((1, nh, 1, t), lambda i: (i, 0, 0, 0)),
                  pl.BlockSpec((1, HEAD_DIM), lambda i: (0, 0))],
        out_specs=new,
        compiler_params=_cparams(("parallel",)), name="fox_sample_attention",
    )(q, kn, vn, kp, vp, ckp, ckn, g)


def _split_bf16(a):
    hi = a.astype(BF16)
    return hi, (a - hi.astype(F32)).astype(BF16)


def _outproj_router_kernel(md_ref, mf_ref, wd_ref, wf_ref, x_ref, g_ref, wr_ref, br_ref,
                           h_ref, hn_ref, idx_ref, gate_ref):
    h = x_ref[...] + _dot(md_ref[...], wd_ref[...]) + _dot(mf_ref[...], wf_ref[...])
    h_ref[...] = h
    hn = h * lax.rsqrt(jnp.mean(h * h, axis=-1, keepdims=True) + RMS_EPS) * g_ref[...]
    for c in range(hn_ref.shape[1]):
        hn_ref[:, c, :] = hn[:, c * LANES:(c + 1) * LANES]
    hn_hi, hn_lo = _split_bf16(hn)
    wr_hi, wr_lo = _split_bf16(wr_ref[...])
    logits = _dot(hn_hi, wr_hi) + (_dot(hn_lo, wr_hi) + _dot(hn_hi, wr_lo)) + br_ref[...]
    lane = lax.broadcasted_iota(jnp.int32, logits.shape, 1)
    work = logits
    idx_out = jnp.zeros(logits.shape, jnp.int32)
    val_out = jnp.zeros(logits.shape, F32)
    top = None
    denom = None
    for k in range(TOP_K):
        mx = jnp.max(work, axis=-1, keepdims=True)
        first = jnp.min(jnp.where(work == mx, lane, LANES), axis=-1, keepdims=True)
        work = jnp.where(lane == first, -jnp.inf, work)
        if k == 0:
            top = mx
        e = jnp.exp(mx - top)
        denom = e if k == 0 else denom + e
        idx_out = jnp.where(lane == k, first, idx_out)
        val_out = jnp.where(lane == k, e, val_out)
    idx_ref[...] = idx_out
    gate_ref[...] = val_out / denom


def _outproj_router(mix_d, mix_f, w_d, w_f, x, g, w_router_pad, b_router_pad):
    m, d = x.shape
    kd, kf = mix_d.shape[1], mix_f.shape[1]
    tm = _tile(m, 256)
    row = lambda w: pl.BlockSpec((tm, w), lambda i: (i, 0))
    full = lambda r, c: pl.BlockSpec((r, c), lambda i: (0, 0))
    return pl.pallas_call(
        _outproj_router_kernel,
        out_shape=[jax.ShapeDtypeStruct((m, d), F32),
                   jax.ShapeDtypeStruct((m, d // LANES, LANES), F32),
                   jax.ShapeDtypeStruct((m, LANES), jnp.int32),
                   jax.ShapeDtypeStruct((m, LANES), F32)],
        grid=(m // tm,),
        in_specs=[row(kd), row(kf), full(kd, d), full(kf, d), row(d), full(1, d),
                  full(d, LANES), full(1, LANES)],
        out_specs=[row(d), pl.BlockSpec((tm, d // LANES, LANES), lambda i: (i, 0, 0)),
                   row(LANES), row(LANES)],
        compiler_params=_cparams(("parallel",)), name="outproj_router",
    )(mix_d, mix_f, w_d, w_f, x, g, w_router_pad, b_router_pad)


def _rank_kernel(idx_ref, rank_ref, count_ref, carry_ref):
    @pl.when(pl.program_id(0) == 0)
    def _():
        carry_ref[...] = jnp.zeros_like(carry_ref)

    tm = idx_ref.shape[0]
    idx = idx_ref[...]
    lane = lax.broadcasted_iota(jnp.int32, (tm, LANES), 1)
    picks = [lane == idx[:, k:k + 1] for k in range(TOP_K)]
    onehot = jnp.zeros((tm, LANES), F32)
    for p in picks:
        onehot = onehot + jnp.where(p, 1.0, 0.0)
    row = lax.broadcasted_iota(jnp.int32, (tm, tm), 0)
    col = lax.broadcasted_iota(jnp.int32, (tm, tm), 1)
    strict = jnp.where(col < row, 1.0, 0.0).astype(BF16)
    before = _dot(strict, onehot.astype(BF16)) + carry_ref[...]
    rank = jnp.zeros((tm, LANES), F32)
    for k, p in enumerate(picks):
        r = jnp.sum(jnp.where(p, before, 0.0), axis=-1, keepdims=True)
        rank = jnp.where(lane == k, r, rank)
    rank_ref[...] = rank.astype(jnp.int32)
    total = before[tm - 1:tm, :] + onehot[tm - 1:tm, :]
    carry_ref[...] = total
    count_ref[...] = total.astype(jnp.int32)


def _expert_ranks(idx):
    n_tok = idx.shape[0]
    tm = _tile(n_tok, 512)
    return pl.pallas_call(
        _rank_kernel,
        out_shape=[jax.ShapeDtypeStruct((n_tok, LANES), jnp.int32),
                   jax.ShapeDtypeStruct((1, LANES), jnp.int32)],
        grid=(n_tok // tm,),
        in_specs=[pl.BlockSpec((tm, LANES), lambda i: (i, 0))],
        out_specs=[pl.BlockSpec((tm, LANES), lambda i: (i, 0)),
                   pl.BlockSpec((1, LANES), lambda i: (0, 0))],
        scratch_shapes=[pltpu.VMEM((1, LANES), F32)],
        compiler_params=_cparams(("arbitrary",)), name="expert_ranks",
    )(idx)


def _row_copy(src_ref, dst_ref, sem, src_row, dst_row):
    return pltpu.make_async_copy(src_ref.at[pl.ds(src_row, 1)], dst_ref.at[pl.ds(dst_row, 1)], sem)


def _gather_rows(idx_ref, src_ref, dst_ref, sem, n_rows):
    assert n_rows % GATHER_UNROLL == 0

    def start(g, carry):
        base = pl.multiple_of(g * GATHER_UNROLL, GATHER_UNROLL)
        for u in range(GATHER_UNROLL):
            _row_copy(src_ref, dst_ref, sem, idx_ref[0, 0, base + u], base + u).start()
        return carry

    def wait(g, carry):
        base = pl.multiple_of(g * GATHER_UNROLL, GATHER_UNROLL)
        for u in range(GATHER_UNROLL):
            _row_copy(src_ref, dst_ref, sem, 0, base + u).wait()
        return carry

    lax.fori_loop(0, n_rows // GATHER_UNROLL, start, 0)
    lax.fori_loop(0, n_rows // GATHER_UNROLL, wait, 0)


def _dispatch_kernel(pad_start_ref, pad_count_ref, dest_ref, hp_ref, hs_ref, xs_ref, zero_buf, sem,
                     *, ts, n_prompt_steps, n_pad):
    i = pl.program_id(0)

    @pl.when(i == 0)
    def _():
        zero_buf[...] = jnp.zeros_like(zero_buf)

        def fill(e, carry):
            def start(r, c):
                pltpu.make_async_copy(zero_buf, xs_ref.at[pad_start_ref[e] + r], sem).start()
                return c

            def wait(r, c):
                pltpu.make_async_copy(zero_buf, xs_ref.at[0], sem).wait()
                return c

            lax.fori_loop(0, pad_count_ref[e], start, 0)
            lax.fori_loop(0, pad_count_ref[e], wait, 0)
            return carry

        lax.fori_loop(0, n_pad, fill, 0)

    def copy_tokens(src_ref):
        per_trip = GATHER_UNROLL // TOP_K

        def start(g, c):
            for u in range(per_trip):
                tok = g * per_trip + u
                for k in range(TOP_K):
                    pltpu.make_async_copy(src_ref.at[tok],
                                          xs_ref.at[dest_ref[0, 0, tok * TOP_K + k]], sem).start()
            return c

        def wait(g, c):
            for _ in range(GATHER_UNROLL):
                pltpu.make_async_copy(src_ref.at[0], xs_ref.at[0], sem).wait()
            return c

        lax.fori_loop(0, ts // per_trip, start, 0)
        lax.fori_loop(0, ts // per_trip, wait, 0)

    @pl.when(i < n_prompt_steps)
    def _():
        copy_tokens(hp_ref)

    @pl.when(i >= n_prompt_steps)
    def _():
        copy_tokens(hs_ref)


def _dispatch(pad_start, pad_count, dest, hn_prompt, hn_sample, n_rows):
    n_p, n_s = hn_prompt.shape[0], hn_sample.shape[0]
    ts = _tile(math.gcd(n_p, n_s), 512)
    assert ts % (GATHER_UNROLL // TOP_K) == 0
    n_steps = (n_p + n_s) // ts
    kern = functools.partial(_dispatch_kernel, ts=ts, n_prompt_steps=n_p // ts,
                             n_pad=pad_start.shape[0])
    slab = hn_prompt.shape[1:]
    last_p = n_p // ts - 1
    return pl.pallas_call(
        kern,
        out_shape=jax.ShapeDtypeStruct((n_rows,) + slab, hn_prompt.dtype),
        grid_spec=pltpu.PrefetchScalarGridSpec(
            num_scalar_prefetch=2,
            grid=(n_steps,),
            in_specs=[pl.BlockSpec((1, 1, ts * TOP_K), lambda i, ps, pc: (i, 0, 0),
                                   memory_space=pltpu.SMEM),
                      pl.BlockSpec((ts,) + slab, lambda i, ps, pc: (jnp.minimum(i, last_p), 0, 0)),
                      pl.BlockSpec((ts,) + slab,
                                   lambda i, ps, pc: (jnp.maximum(i - last_p - 1, 0), 0, 0))],
            out_specs=pl.BlockSpec(memory_space=pl.ANY),
            scratch_shapes=[pltpu.VMEM(hn_prompt.shape[1:], hn_prompt.dtype),
                            pltpu.SemaphoreType.DMA]),
        compiler_params=_cparams(("arbitrary",)), name="moe_dispatch",
    )(pad_start, pad_count, dest.reshape(n_steps, 1, ts * TOP_K), hn_prompt, hn_sample)


def _combine_kernel(dest_ref, y_ref, h_ref, gate_ref, g_ref, o_ref, buf, sem, *, tm):
    _gather_rows(dest_ref, y_ref, buf, sem, TOP_K * tm)
    gates = gate_ref[...]
    out = h_ref[...]
    for k in range(TOP_K):
        out = out + buf[k * tm:(k + 1) * tm, :] * gates[:, k:k + 1]
    out = out * lax.rsqrt(jnp.mean(out * out, axis=-1, keepdims=True) + RMS_EPS)
    o_ref[...] = out * g_ref[...]


def _combine(dest, y_rows, h, gates, g):
    n_tiles, _, per = dest.shape
    tm = per // TOP_K
    m, d = h.shape
    kern = functools.partial(_combine_kernel, tm=tm)
    return pl.pallas_call(
        kern,
        out_shape=jax.ShapeDtypeStruct((m, d), F32),
        grid=(n_tiles,),
        in_specs=[pl.BlockSpec((1, 1, per), lambda i: (i, 0, 0), memory_space=pltpu.SMEM),
                  pl.BlockSpec(memory_space=pl.ANY),
                  pl.BlockSpec((tm, d), lambda i: (i, 0)),
                  pl.BlockSpec((tm, LANES), lambda i: (i, 0)),
                  pl.BlockSpec((1, d), lambda i: (0, 0))],
        out_specs=pl.BlockSpec((tm, d), lambda i: (i, 0)),
        scratch_shapes=[pltpu.VMEM((per, d), y_rows.dtype), pltpu.SemaphoreType.DMA],
        compiler_params=_cparams(("arbitrary",)), name="moe_combine",
    )(dest, y_rows, h, gates, g)


def _expert_changed(be_ref, rb):
    prev = be_ref[jnp.maximum(rb - 1, 0)]
    return jnp.logical_or(rb == 0, be_ref[rb] != prev)


def _moe_up_kernel(be_ref, used_ref, x_ref, wg_ref, wu_ref, bg_ref, bu_ref, o_ref, wg16, wu16):
    rb = pl.program_id(1)

    @pl.when(_expert_changed(be_ref, rb))
    def _():
        wg16[...] = wg_ref[0].astype(BF16)
        wu16[...] = wu_ref[0].astype(BF16)

    @pl.when(rb < used_ref[0])
    def _():
        rows, n_chunks = o_ref.shape[0], wg16.shape[0] // LANES
        x = jnp.concatenate([x_ref[pl.ds(c, rows, stride=n_chunks), :].astype(BF16)
                             for c in range(n_chunks)], axis=1)
        gate = jnp.minimum(_dot(x, wg16[...]) + bg_ref[0], SWIGLU_LIMIT)
        up = jnp.clip(_dot(x, wu16[...]) + bu_ref[0], -SWIGLU_LIMIT, SWIGLU_LIMIT)
        act = (up + 1.0) * (gate * jax.nn.sigmoid(SWIGLU_ALPHA * gate))
        o_ref[...] = act.astype(o_ref.dtype)

    @pl.when(rb >= used_ref[0])
    def _():
        o_ref[...] = jnp.zeros_like(o_ref)


def _moe_up(block_expert, n_used, xs_slabs, w_gate, w_up, b_gate, b_up, tn):
    n_rows, n_chunks, _ = xs_slabs.shape
    xs = xs_slabs.reshape(n_rows * n_chunks, LANES)
    n_exp, d, d_ff = w_gate.shape
    rows = MOE_ROWS
    n_blocks = n_rows // rows
    wspec = pl.BlockSpec((1, d, tn), lambda n, rb, be, used: (be[rb], 0, n))
    bspec = pl.BlockSpec((1, 1, tn), lambda n, rb, be, used: (be[rb], 0, n))
    return pl.pallas_call(
        _moe_up_kernel,
        out_shape=jax.ShapeDtypeStruct((n_rows, d_ff), BF16),
        grid_spec=pltpu.PrefetchScalarGridSpec(
            num_scalar_prefetch=2,
            grid=(d_ff // tn, n_blocks),
            in_specs=[pl.BlockSpec((rows * n_chunks, LANES), lambda n, rb, be, used: (rb, 0)),
                      wspec, wspec, bspec, bspec],
            out_specs=pl.BlockSpec((rows, tn), lambda n, rb, be, used: (rb, n)),
            scratch_shapes=[pltpu.VMEM((d, tn), BF16), pltpu.VMEM((d, tn), BF16)]),
        compiler_params=_cparams(("arbitrary", "arbitrary")), name="moe_gate_up",
    )(block_expert, n_used, xs, w_gate, w_up, b_gate.reshape(n_exp, 1, d_ff),
      b_up.reshape(n_exp, 1, d_ff))


def _moe_down_kernel(be_ref, used_ref, a_ref, w_ref, b_ref, o_ref, w16):
    rb = pl.program_id(1)

    @pl.when(_expert_changed(be_ref, rb))
    def _():
        w16[...] = w_ref[0].astype(BF16)

    @pl.when(rb < used_ref[0])
    def _():
        o_ref[...] = _dot(a_ref[...], w16[...]) + b_ref[0]

    @pl.when(rb >= used_ref[0])
    def _():
        o_ref[...] = jnp.zeros_like(o_ref)


def _moe_down(block_expert, n_used, act, w_down, b_down, tn):
    n_rows, d_ff = act.shape
    n_exp, _, d = w_down.shape
    rows = MOE_ROWS
    return pl.pallas_call(
        _moe_down_kernel,
        out_shape=jax.ShapeDtypeStruct((n_rows, d), F32),
        grid_spec=pltpu.PrefetchScalarGridSpec(
            num_scalar_prefetch=2,
            grid=(d // tn, n_rows // rows),
            in_specs=[pl.BlockSpec((rows, d_ff), lambda n, rb, be, used: (rb, 0)),
                      pl.BlockSpec((1, d_ff, tn), lambda n, rb, be, used: (be[rb], 0, n)),
                      pl.BlockSpec((1, 1, tn), lambda n, rb, be, used: (be[rb], 0, n))],
            out_specs=pl.BlockSpec((rows, tn), lambda n, rb, be, used: (rb, n)),
            scratch_shapes=[pltpu.VMEM((d_ff, tn), BF16)]),
        compiler_params=_cparams(("arbitrary", "arbitrary")), name="moe_down",
    )(block_expert, n_used, act, w_down, b_down.reshape(n_exp, 1, d))


def _pad_lanes(a, fill=0.0):
    return jnp.pad(a, ((0, 0), (0, LANES - a.shape[1])), constant_values=fill)


def _mixer_projections(x2d, pos, norm_g, w_in, b_forget, head_major):
    d = x2d.shape[1]
    dw = (w_in.shape[1] - 8) // 6
    q_scale = HEAD_DIM ** -0.5 * LOG2E
    xn = _rmsnorm_bf16(x2d, norm_g)
    w16 = w_in.astype(BF16)
    wcol = lambda j: w16[:, j * dw:(j + 1) * dw]
    tabs_q = _rope_tables(pos, q_scale)
    tabs_k = _rope_tables(pos, 1.0)
    hm = dict(head_major=head_major)
    (dq,) = _proj(xn, wcol(0), rope_tabs=tabs_q, head_w=2 * HEAD_DIM, **hm)
    dk32, dk = _proj(xn, wcol(1), rope_tabs=tabs_k, want_f32=True, head_w=2 * HEAD_DIM, **hm)
    dv32, dv = _proj(xn, wcol(2), want_f32=True, head_w=2 * HEAD_DIM, **hm)
    (fq,) = _proj(xn, wcol(3), scale=q_scale, **hm)
    fk32, fk = _proj(xn, wcol(4), want_f32=True, **hm)
    fv32, fv = _proj(xn, wcol(5), want_f32=True, **hm)
    logf = _logf(xn, _pad_lanes(w16[:, 6 * dw:]), _pad_lanes(b_forget.reshape(1, -1)))
    return (dq, dk, dv, fq, fk, fv), (dk32, dv32, fk32, fv32), logf


def _moe(hn_prompt, hn_sample, idx_all, w_gate, w_up, w_down, b_gate, b_up, b_down):
    n_tok = idx_all.shape[0]
    n_exp = w_gate.shape[0]
    rows = MOE_ROWS
    n_blocks = -(-(n_tok * TOP_K) // rows) + n_exp
    n_rows = n_blocks * rows

    rank, counts = _expert_ranks(idx_all)
    experts = idx_all[:, :TOP_K]
    counts = counts[0, :n_exp]
    padded = (counts + rows - 1) // rows * rows
    padded_end = jnp.cumsum(padded)
    padded_start = padded_end - padded
    dest = padded_start[experts] + rank[:, :TOP_K]
    pad_start = jnp.concatenate([padded_start + counts, padded_end[-1:]]).astype(jnp.int32)
    pad_count = jnp.concatenate([padded - counts, n_rows - padded_end[-1:]]).astype(jnp.int32)
    block_start = jnp.arange(n_blocks, dtype=jnp.int32) * rows
    block_expert = jnp.minimum(
        jnp.sum((padded_end[None, :] <= block_start[:, None]).astype(jnp.int32), axis=1), n_exp - 1)
    n_used = (padded_end[-1] // rows).astype(jnp.int32).reshape(1)

    xs = _dispatch(pad_start, pad_count, dest, hn_prompt, hn_sample, n_rows)
    act = _moe_up(block_expert, n_used, xs, w_gate, w_up, b_gate, b_up,
                  tn=min(1024, w_gate.shape[2]))
    y_rows = _moe_down(block_expert, n_used, act, w_down, b_down, tn=min(1024, w_down.shape[2]))
    return y_rows, dest


def _combine_tokens(dest, y_rows, h, gates, g):
    m = h.shape[0]
    tm = _tile(m, 256)
    dest_t = dest.reshape(m // tm, tm, TOP_K).transpose(0, 2, 1).reshape(m // tm, 1, TOP_K * tm)
    return _combine(dest_t, y_rows, h, gates, g.reshape(1, -1))


def kernel(x_prompt, x_sample, cache_diff_k, cache_diff_v, cache_fox_k, cache_fox_v, cache_fox_logf,
           attn_norm_g, w_in, b_forget, lambda_q1, lambda_k1, lambda_q2, lambda_k2,
           diff_out_norm_g, fox_out_norm_g, w_out, ffn_norm_g, w_router, b_router,
           w_gate, b_gate, w_up, b_up, w_down, b_down, final_norm_g):
    depth = w_in.shape[0]
    assert depth == 1, "single-layer stack"
    lam_init = 0.8 - 0.6 * math.exp(-0.3 * 0)
    drop = lambda a: a.reshape(a.shape[1:])
    (cache_diff_k, cache_diff_v, cache_fox_k, cache_fox_v, cache_fox_logf, attn_norm_g, w_in,
     b_forget, lambda_q1, lambda_k1, lambda_q2, lambda_k2, diff_out_norm_g, fox_out_norm_g, w_out,
     ffn_norm_g, w_router, b_router, w_gate, b_gate, w_up, b_up, w_down, b_down) = map(drop, (
         cache_diff_k, cache_diff_v, cache_fox_k, cache_fox_v, cache_fox_logf, attn_norm_g, w_in,
         b_forget, lambda_q1, lambda_k1, lambda_q2, lambda_k2, diff_out_norm_g, fox_out_norm_g,
         w_out, ffn_norm_g, w_router, b_router, w_gate, b_gate, w_up, b_up, w_down, b_down))
    bp, t, d = x_prompt.shape
    assert bp == 1
    bs, ts, _ = x_sample.shape
    past = cache_diff_k.shape[1]
    n_dh = cache_diff_k.shape[2]
    n_fh = cache_fox_k.shape[2]
    dw = n_dh * 2 * HEAD_DIM
    fw = n_fh * HEAD_DIM
    lams = [v.reshape(1, HEAD_DIM) for v in (lambda_q1, lambda_k1, lambda_q2, lambda_k2)]
    diff_g = diff_out_norm_g.reshape(1, -1)
    fox_g = fox_out_norm_g.reshape(1, -1)

    xp = x_prompt.reshape(t, d)
    pos_p = jnp.arange(t, dtype=jnp.int32)
    (dq, dk, dv, fq, fk, fv), cache_p, logf_p = _mixer_projections(
        xp, pos_p, attn_norm_g, w_in, b_forget, head_major=True)
    _, tk = _sweep_tiles(t)
    c_p = _cumsum_rows(logf_p, _tile(t, 512))
    ck_p = c_p[:, :n_fh].T.reshape(n_fh, t // tk, 1, tk)
    mix_d_p = _diff_prompt_attention(dq, dk, dv, diff_g, lams, lam_init)
    mix_f_p = _fox_prompt_attention(fq, fk, fv, ck_p, fox_g)

    xs = x_sample.reshape(bs * ts, d)
    pos_s = past + jnp.arange(ts, dtype=jnp.int32)
    (sdq, sdk, sdv, sfq, sfk, sfv), cache_s, logf_s = _mixer_projections(
        xs, jnp.tile(pos_s, bs), attn_norm_g, w_in, b_forget, head_major=False)
    logf_all = jnp.concatenate([cache_fox_logf.astype(F32),
                                logf_s[:, :n_fh].reshape(bs, ts, n_fh)], axis=1)
    tt = past + ts
    lf_cols = logf_all.transpose(1, 0, 2).reshape(tt, bs * n_fh)
    pad_c = (-lf_cols.shape[1]) % LANES
    lf_cols = jnp.pad(lf_cols, ((0, 0), (0, pad_c)))
    tb = max(tb_ for tb_ in range(8, 513, 8) if tt % tb_ == 0)
    c_s = _cumsum_rows(lf_cols, tb)[:, :bs * n_fh].reshape(tt, bs, n_fh).transpose(1, 2, 0)
    ckp_s = c_s[:, :, None, :past]
    ckn_s = c_s[:, :, None, past:]
    r3 = lambda a: a.reshape(bs, ts, -1)
    mix_d_s = _diff_sample_attention(
        r3(sdq), r3(sdk), r3(sdv), cache_diff_k, cache_diff_v,
        diff_g, lams, lam_init).reshape(bs * ts, dw)
    mix_f_s = _fox_sample_attention(
        r3(sfq), r3(sfk), r3(sfv), cache_fox_k, cache_fox_v,
        ckp_s, ckn_s, fox_g).reshape(bs * ts, fw)

    w_out16 = w_out.astype(BF16)
    w_router_pad = _pad_lanes(w_router)
    b_router_pad = _pad_lanes(b_router.reshape(1, -1), fill=NEG_INF)
    ffn_g = ffn_norm_g.reshape(1, -1)
    h_p, hn_p, idx_p, gate_p = _outproj_router(mix_d_p, mix_f_p, w_out16[:dw], w_out16[dw:],
                                               xp, ffn_g, w_router_pad, b_router_pad)
    h_s, hn_s, idx_s, gate_s = _outproj_router(mix_d_s, mix_f_s, w_out16[:dw], w_out16[dw:],
                                               xs, ffn_g, w_router_pad, b_router_pad)

    idx_all = jnp.concatenate([idx_p, idx_s], axis=0)
    y_rows, dest = _moe(hn_p, hn_s, idx_all, w_gate, w_up, w_down, b_gate, b_up, b_down)
    y_p = _combine_tokens(dest[:t], y_rows, h_p, gate_p, final_norm_g)
    y_s = _combine_tokens(dest[t:], y_rows, h_s, gate_s, final_norm_g)

    dk32, dv32, fk32, fv32 = cache_p
    sdk32, sdv32, sfk32, sfv32 = cache_s
    return (y_p.reshape(1, t, d), y_s.reshape(bs, ts, d),
            dk32.reshape(1, 1, t, n_dh, 2 * HEAD_DIM), dv32.reshape(1, 1, t, n_dh, 2 * HEAD_DIM),
            fk32.reshape(1, 1, t, n_fh, HEAD_DIM), fv32.reshape(1, 1, t, n_fh, HEAD_DIM),
            logf_p[:, :n_fh].reshape(1, 1, t, n_fh),
            sdk32.reshape(1, bs, ts, n_dh, 2 * HEAD_DIM), sdv32.reshape(1, bs, ts, n_dh, 2 * HEAD_DIM),
            sfk32.reshape(1, bs, ts, n_fh, HEAD_DIM), sfv32.reshape(1, bs, ts, n_fh, HEAD_DIM),
            logf_s[:, :n_fh].reshape(1, bs, ts, n_fh))
```

```python
import functools
import math

import jax
import jax.numpy as jnp
from jax import lax
from jax.experimental import pallas as pl
from jax.experimental.pallas import tpu as pltpu

HEAD_DIM = 128
CHUNK_SHIFT = 6
ROT_DIM = HEAD_DIM // 4
ROPE_THETA = 500000.0
TOP_K = 4
SWIGLU_LIMIT = 7.0
SWIGLU_ALPHA = 1.702
RMS_EPS = 1e-6
NEG_INF = -1e30
LOG2E = 1.4426950408889634
LANES = 128
MOE_ROWS = 512
GATHER_UNROLL = 8
VMEM_LIMIT = 56 * 1024 * 1024

F32 = jnp.float32
BF16 = jnp.bfloat16


def _cparams(sem, vmem=VMEM_LIMIT):
    return pltpu.CompilerParams(dimension_semantics=sem, vmem_limit_bytes=vmem)


def _tile(n, pref):
    if n <= pref:
        return n
    t = max(c for c in range(8, pref + 1, 8) if n % c == 0)
    return t


def _dot_nt(a, b):
    return lax.dot_general(a, b, (((1,), (1,)), ((), ())), preferred_element_type=F32)


def _dot(a, b):
    return jnp.dot(a, b, preferred_element_type=F32)


def _rmsnorm_kernel(x_ref, g_ref, o_ref):
    x = x_ref[...]
    y = x * lax.rsqrt(jnp.mean(x * x, axis=-1, keepdims=True) + RMS_EPS)
    o_ref[...] = (y * g_ref[...]).astype(o_ref.dtype)


def _rmsnorm_bf16(x, g):
    m, d = x.shape
    tm = _tile(m, 512)
    return pl.pallas_call(
        _rmsnorm_kernel,
        out_shape=jax.ShapeDtypeStruct((m, d), BF16),
        grid=(m // tm,),
        in_specs=[pl.BlockSpec((tm, d), lambda i: (i, 0)),
                  pl.BlockSpec((1, d), lambda i: (0, 0))],
        out_specs=pl.BlockSpec((tm, d), lambda i: (i, 0)),
        compiler_params=_cparams(("parallel",)),
        name="rmsnorm_bf16",
    )(x, g.reshape(1, d))


def _proj_kernel(*refs, rope, scale, want_f32, want_bf16, head_major, head_w):
    it = iter(refs)
    x_ref = next(it)
    w_ref = next(it)
    if rope:
        c_ref, sa_ref, sb_ref = next(it), next(it), next(it)
    o32_ref = next(it) if want_f32 else None
    o16_ref = next(it) if want_bf16 else None

    n = w_ref.shape[1]
    y = _dot(x_ref[...], w_ref[...])
    for h in range(n // HEAD_DIM):
        sl = slice(h * HEAD_DIM, (h + 1) * HEAD_DIM)
        yh = y[:, sl]
        if rope:
            yh = (yh * c_ref[...]
                  + pltpu.roll(yh, HEAD_DIM - ROT_DIM // 2, 1) * sa_ref[...]
                  + pltpu.roll(yh, ROT_DIM // 2, 1) * sb_ref[...])
        elif scale != 1.0:
            yh = yh * scale
        per = head_w // HEAD_DIM
        if want_f32:
            o32_ref[:, h // per, (h % per) * HEAD_DIM:(h % per + 1) * HEAD_DIM] = yh
        if want_bf16:
            if head_major:
                o16_ref[h // per, :, (h % per) * HEAD_DIM:(h % per + 1) * HEAD_DIM] = yh.astype(BF16)
            else:
                o16_ref[:, sl] = yh.astype(BF16)


def _proj(xn, w, rope_tabs=None, scale=1.0, want_f32=False, want_bf16=True,
          head_major=False, head_w=HEAD_DIM):
    m, d = xn.shape
    n = w.shape[1]
    tm = _tile(m, 512)
    rope = rope_tabs is not None
    in_specs = [pl.BlockSpec((tm, d), lambda i: (i, 0)),
                pl.BlockSpec((d, n), lambda i: (0, 0))]
    args = [xn, w]
    if rope:
        in_specs += [pl.BlockSpec((tm, HEAD_DIM), lambda i: (i, 0))] * 3
        args += list(rope_tabs)
    out_shape, out_specs = [], []
    if want_f32:
        out_shape.append(jax.ShapeDtypeStruct((m, n // head_w, head_w), F32))
        out_specs.append(pl.BlockSpec((tm, n // head_w, head_w), lambda i: (i, 0, 0)))
    if want_bf16:
        if head_major:
            nh = n // head_w
            out_shape.append(jax.ShapeDtypeStruct((nh, m, head_w), BF16))
            out_specs.append(pl.BlockSpec((nh, tm, head_w), lambda i: (0, i, 0)))
        else:
            out_shape.append(jax.ShapeDtypeStruct((m, n), BF16))
            out_specs.append(pl.BlockSpec((tm, n), lambda i: (i, 0)))
    kern = functools.partial(_proj_kernel, rope=rope, scale=scale, want_f32=want_f32,
                             want_bf16=want_bf16, head_major=head_major, head_w=head_w)
    return pl.pallas_call(
        kern, out_shape=out_shape, grid=(m // tm,), in_specs=in_specs, out_specs=out_specs,
        compiler_params=_cparams(("parallel",)), name="in_proj",
    )(*args)


def _rope_tables(pos, scale):
    half = ROT_DIM // 2
    t = pos.shape[0]
    inv_freq = jnp.power(ROPE_THETA, -jnp.arange(half, dtype=F32) * (2.0 / ROT_DIM))
    ang = pos.astype(F32)[:, None] * inv_freq[None, :]
    cos, sin = jnp.cos(ang), jnp.sin(ang)
    zeros = lambda w: jnp.zeros((t, w), F32)
    c = jnp.concatenate([cos, cos, jnp.ones((t, HEAD_DIM - ROT_DIM), F32)], axis=1)
    sa = jnp.concatenate([-sin, zeros(HEAD_DIM - half)], axis=1)
    sb = jnp.concatenate([zeros(half), sin, zeros(HEAD_DIM - ROT_DIM)], axis=1)
    return c * scale, sa * scale, sb * scale


def _logf_kernel(x_ref, w_ref, b_ref, o_ref):
    z = _dot(x_ref[...], w_ref[...]) + b_ref[...]
    o_ref[...] = jnp.minimum(z, 0.0) - jnp.log(1.0 + jnp.exp(-jnp.abs(z)))


def _logf(xn, w_pad, b_pad):
    m, d = xn.shape
    tm = _tile(m, 512)
    return pl.pallas_call(
        _logf_kernel,
        out_shape=jax.ShapeDtypeStruct((m, LANES), F32),
        grid=(m // tm,),
        in_specs=[pl.BlockSpec((tm, d), lambda i: (i, 0)),
                  pl.BlockSpec((d, LANES), lambda i: (0, 0)),
                  pl.BlockSpec((1, LANES), lambda i: (0, 0))],
        out_specs=pl.BlockSpec((tm, LANES), lambda i: (i, 0)),
        compiler_params=_cparams(("parallel",)), name="forget_logits",
    )(xn, w_pad, b_pad)


def _cumsum_kernel(x_ref, o_ref, carry_ref):
    @pl.when(pl.program_id(0) == 0)
    def _():
        carry_ref[...] = jnp.zeros_like(carry_ref)

    tb = x_ref.shape[0]
    row = lax.broadcasted_iota(jnp.int32, (tb, tb), 0)
    col = lax.broadcasted_iota(jnp.int32, (tb, tb), 1)
    tri = jnp.where(col <= row, 1.0, 0.0).astype(F32)
    c = jnp.dot(tri, x_ref[...], preferred_element_type=F32,
                precision=lax.Precision.HIGHEST) + carry_ref[...]
    o_ref[...] = c
    carry_ref[...] = c[tb - 1:tb, :]


def _cumsum_rows(x, tb):
    t, c = x.shape
    assert t % tb == 0 and c % LANES == 0
    return pl.pallas_call(
        _cumsum_kernel,
        out_shape=jax.ShapeDtypeStruct((t, c), F32),
        grid=(t // tb,),
        in_specs=[pl.BlockSpec((tb, c), lambda i: (i, 0))],
        out_specs=pl.BlockSpec((tb, c), lambda i: (i, 0)),
        scratch_shapes=[pltpu.VMEM((1, c), F32)],
        compiler_params=_cparams(("arbitrary",)), name="cumsum_rows",
    )(x)


def _lambda_value(lq1, lk1, lq2, lk2, lam_init):
    a = jnp.sum(lq1[...] * lk1[...], axis=-1, keepdims=True)
    b = jnp.sum(lq2[...] * lk2[...], axis=-1, keepdims=True)
    return jnp.exp(a) - jnp.exp(b) + lam_init


def _head_rmsnorm(o, g):
    return o * lax.rsqrt(jnp.mean(o * o, axis=-1, keepdims=True) + RMS_EPS) * g


def _online_update(s, m, l, acc_ref, v):
    m_new = jnp.maximum(m, jnp.max(s, axis=-1, keepdims=True))
    alpha = jnp.exp2(m - m_new)
    p = jnp.exp2(s - m_new)
    l_new = alpha * l + jnp.sum(p, axis=-1, keepdims=True)
    acc_ref[...] = alpha * acc_ref[...] + _dot(p.astype(BF16), v)
    return m_new, l_new


def _causal_sweep(n_full, offset, tq, tk, scores_fn, values_fn, mask_fn, s_a, s_b, m_ref, l_ref,
                  acc_refs):
    assert tk in (tq, 2 * tq)
    n_maps = len(acc_refs)
    for n in range(n_maps):
        m_ref[n] = jnp.full(m_ref.shape[1:], NEG_INF, F32)
        l_ref[n] = jnp.zeros(l_ref.shape[1:], F32)
        acc_refs[n][...] = jnp.zeros_like(acc_refs[n])

    def produce(bufs, j):
        for ref, s in zip(bufs, scores_fn(j * tk, tk)):
            ref[...] = s

    def consume(tiles, start, width):
        v = values_fn(start, width)
        for n in range(n_maps):
            m_new, l_new = _online_update(tiles[n], m_ref[n], l_ref[n], acc_refs[n], v)
            m_ref[n] = m_new
            l_ref[n] = l_new

    def pair(p, carry):
        produce(s_b, 2 * p + 1)
        consume([r[...] for r in s_a], 2 * p * tk, tk)
        produce(s_a, 2 * p + 2)
        consume([r[...] for r in s_b], (2 * p + 1) * tk, tk)
        return carry

    produce(s_a, 0)
    lax.fori_loop(0, n_full // 2, pair, 0)

    @pl.when(n_full % 2 == 1)
    def _():
        consume([r[...] for r in s_a], (n_full - 1) * tk, tk)

    base = n_full * tk

    @pl.when(offset > 0)
    def _():
        consume(scores_fn(base, tq), base, tq)

    consume([mask_fn(s) for s in scores_fn(base + offset, tq)], base + offset, tq)


def _key_rows(ref, start, width, align):
    return ref[0, pl.ds(pl.multiple_of(start, align), width), :]


def _sweep_tiles(t):
    tq = _tile(t, 512)
    tk = 2 * tq if t % (2 * tq) == 0 else tq
    assert tq % (1 << CHUNK_SHIFT) == 0
    return tq, tk


def _diag_position(i, tq, tk):
    n_full = (i * tq) // tk
    return n_full, i * tq - n_full * tk


def _diff_prompt_kernel(q_ref, k_ref, v_ref, g_ref, lq1, lk1, lq2, lk2, o_ref,
                        acc1, acc2, a1, a2, b1, b2, m_ref, l_ref, *, tq, tk, lam_init):
    n_full, offset = _diag_position(pl.program_id(1), tq, tk)
    q = q_ref[0]
    q1, q2 = q[:, :HEAD_DIM], q[:, HEAD_DIM:]

    def scores(start, width):
        k = _key_rows(k_ref, start, width, tq)
        return _dot_nt(q1, k[:, :HEAD_DIM]), _dot_nt(q2, k[:, HEAD_DIM:])

    def mask(s):
        row = lax.broadcasted_iota(jnp.int32, (tq, tq), 0)
        col = lax.broadcasted_iota(jnp.int32, (tq, tq), 1)
        return jnp.where((col >> CHUNK_SHIFT) <= (row >> CHUNK_SHIFT), s, NEG_INF)

    _causal_sweep(n_full, offset, tq, tk, scores,
                  lambda start, width: _key_rows(v_ref, start, width, tq), mask,
                  (a1, a2), (b1, b2), m_ref, l_ref, (acc1, acc2))
    lam = _lambda_value(lq1, lk1, lq2, lk2, lam_init)
    o = acc1[...] / l_ref[0] - lam * (acc2[...] / l_ref[1])
    o_ref[...] = (_head_rmsnorm(o, g_ref[...]) * (1.0 - lam_init)).astype(o_ref.dtype)


def _diff_prompt_attention(q, k, v, g, lams, lam_init):
    nh, t, w = q.shape
    tq, tk = _sweep_tiles(t)
    vec = pl.BlockSpec((1, HEAD_DIM), lambda h, i: (0, 0))
    kern = functools.partial(_diff_prompt_kernel, tq=tq, tk=tk, lam_init=lam_init)
    return pl.pallas_call(
        kern,
        out_shape=jax.ShapeDtypeStruct((t, nh * w), BF16),
        grid=(nh, t // tq),
        in_specs=[pl.BlockSpec((1, tq, w), lambda h, i: (h, i, 0)),
                  pl.BlockSpec((1, t, w), lambda h, i: (h, 0, 0)),
                  pl.BlockSpec((1, t, w), lambda h, i: (h, 0, 0)),
                  pl.BlockSpec((1, w), lambda h, i: (0, 0)),
                  vec, vec, vec, vec],
        out_specs=pl.BlockSpec((tq, w), lambda h, i: (i, h)),
        scratch_shapes=[pltpu.VMEM((tq, w), F32)] * 2 + [pltpu.VMEM((tq, tk), F32)] * 4
        + [pltpu.VMEM((2, tq, 1), F32)] * 2,
        compiler_params=_cparams(("parallel", "arbitrary")), name="diff_prompt_attention",
    )(q, k, v, g, *lams)


def _fox_prompt_kernel(q_ref, k_ref, v_ref, ck_ref, g_ref, o_ref, acc, s_a, s_b, m_ref, l_ref,
                       *, tq, tk):
    n_full, offset = _diag_position(pl.program_id(1), tq, tk)
    q = q_ref[0]

    def scores(start, width):
        first = start // tq
        ck = jnp.concatenate([ck_ref[0, first + r] for r in range(width // tq)], axis=1)
        return (_dot_nt(q, _key_rows(k_ref, start, width, tq)) - ck * LOG2E,)

    def mask(s):
        row = lax.broadcasted_iota(jnp.int32, (tq, tq), 0)
        col = lax.broadcasted_iota(jnp.int32, (tq, tq), 1)
        return jnp.where(col <= row, s, NEG_INF)

    _causal_sweep(n_full, offset, tq, tk, scores,
                  lambda start, width: _key_rows(v_ref, start, width, tq), mask,
                  (s_a,), (s_b,), m_ref, l_ref, (acc,))
    o_ref[...] = _head_rmsnorm(acc[...] / l_ref[0], g_ref[...]).astype(o_ref.dtype)


def _fox_prompt_attention(q, k, v, ck, g):
    nh, t, w = q.shape
    tq, tk = _sweep_tiles(t)
    assert ck.shape == (nh, t // tq, 1, tq)
    kern = functools.partial(_fox_prompt_kernel, tq=tq, tk=tk)
    return pl.pallas_call(
        kern,
        out_shape=jax.ShapeDtypeStruct((t, nh * w), BF16),
        grid=(nh, t // tq),
        in_specs=[pl.BlockSpec((1, tq, w), lambda h, i: (h, i, 0)),
                  pl.BlockSpec((1, t, w), lambda h, i: (h, 0, 0)),
                  pl.BlockSpec((1, t, w), lambda h, i: (h, 0, 0)),
                  pl.BlockSpec((1, t // tq, 1, tq), lambda h, i: (h, 0, 0, 0)),
                  pl.BlockSpec((1, w), lambda h, i: (0, 0))],
        out_specs=pl.BlockSpec((tq, w), lambda h, i: (i, h)),
        scratch_shapes=[pltpu.VMEM((tq, w), F32)] + [pltpu.VMEM((tq, tk), F32)] * 2
        + [pltpu.VMEM((1, tq, 1), F32)] * 2,
        compiler_params=_cparams(("parallel", "arbitrary")), name="fox_prompt_attention",
    )(q, k, v, ck, g)


def _two_segment_softmax(sp, sn):
    m = jnp.maximum(jnp.max(sp, axis=-1, keepdims=True), jnp.max(sn, axis=-1, keepdims=True))
    pp = jnp.exp2(sp - m)
    pn = jnp.exp2(sn - m)
    l = jnp.sum(pp, axis=-1, keepdims=True) + jnp.sum(pn, axis=-1, keepdims=True)
    return pp.astype(BF16), pn.astype(BF16), l


def _cache_view(cache):
    b, t, nh, w = cache.shape
    nc = w // LANES
    return cache.reshape(b, t, nh, nc, LANES).transpose(0, 1, 3, 2, 4).reshape(b, t * nc * nh, LANES)


def _cache_rows(ref, head, lane_block, n_heads, n_lane_blocks, n_rows):
    stride = n_heads * n_lane_blocks
    return ref[0, pl.ds(lane_block * n_heads + head, n_rows, stride=stride), :].astype(BF16)


def _diff_sample_kernel(q_ref, kn_ref, vn_ref, kp_ref, vp_ref, g_ref, lq1, lk1, lq2, lk2,
                        o_ref, *, past_len, lam_init, n_heads):
    t = q_ref.shape[1]
    w = 2 * HEAD_DIM
    row = lax.broadcasted_iota(jnp.int32, (t, t), 0) + past_len
    col = lax.broadcasted_iota(jnp.int32, (t, t), 1) + past_len
    ok = (col >> CHUNK_SHIFT) <= (row >> CHUNK_SHIFT)
    lam = _lambda_value(lq1, lk1, lq2, lk2, lam_init)
    for h in range(n_heads):
        sl = slice(h * w, (h + 1) * w)
        q = q_ref[0, :, sl]
        kn = kn_ref[0, :, sl]
        vn = vn_ref[0, :, sl]
        vp = [_cache_rows(vp_ref, h, c, n_heads, 2, past_len) for c in range(2)]
        outs = []
        for half in range(2):
            hs = slice(half * HEAD_DIM, (half + 1) * HEAD_DIM)
            sp = _dot_nt(q[:, hs], _cache_rows(kp_ref, h, half, n_heads, 2, past_len))
            sn = jnp.where(ok, _dot_nt(q[:, hs], kn[:, hs]), NEG_INF)
            pp, pn, l = _two_segment_softmax(sp, sn)
            o_past = jnp.concatenate([_dot(pp, vp[0]), _dot(pp, vp[1])], axis=1)
            outs.append((o_past + _dot(pn, vn)) / l)
        o = outs[0] - lam * outs[1]
        o_ref[0, :, sl] = (_head_rmsnorm(o, g_ref[...]) * (1.0 - lam_init)).astype(o_ref.dtype)


def _diff_sample_attention(q, kn, vn, kp, vp, g, lams, lam_init):
    b, t, n = q.shape
    past = kp.shape[1]
    kp, vp = _cache_view(kp), _cache_view(vp)
    new = pl.BlockSpec((1, t, n), lambda i: (i, 0, 0))
    old = pl.BlockSpec((1,) + kp.shape[1:], lambda i: (i, 0, 0))
    vec = pl.BlockSpec((1, HEAD_DIM), lambda i: (0, 0))
    kern = functools.partial(_diff_sample_kernel, past_len=past, lam_init=lam_init,
                             n_heads=n // (2 * HEAD_DIM))
    return pl.pallas_call(
        kern,
        out_shape=jax.ShapeDtypeStruct((b, t, n), BF16),
        grid=(b,),
        in_specs=[new, new, new, old, old,
                  pl.BlockSpec((1, 2 * HEAD_DIM), lambda i: (0, 0)), vec, vec, vec, vec],
        out_specs=new,
        compiler_params=_cparams(("parallel",)), name="diff_sample_attention",
    )(q, kn, vn, kp, vp, g, *lams)


def _fox_sample_kernel(q_ref, kn_ref, vn_ref, kp_ref, vp_ref, ckp_ref, ckn_ref, g_ref, o_ref,
                       *, n_heads):
    t = q_ref.shape[1]
    row = lax.broadcasted_iota(jnp.int32, (t, t), 0)
    col = lax.broadcasted_iota(jnp.int32, (t, t), 1)
    ok = col <= row
    for h in range(n_heads):
        sl = slice(h * HEAD_DIM, (h + 1) * HEAD_DIM)
        q = q_ref[0, :, sl]
        past = ckp_ref.shape[-1]
        kp = _cache_rows(kp_ref, h, 0, n_heads, 1, past)
        vp = _cache_rows(vp_ref, h, 0, n_heads, 1, past)
        sp = _dot_nt(q, kp) - ckp_ref[0, h] * LOG2E
        sn = jnp.where(ok, _dot_nt(q, kn_ref[0, :, sl]) - ckn_ref[0, h] * LOG2E, NEG_INF)
        pp, pn, l = _two_segment_softmax(sp, sn)
        o = (_dot(pp, vp) + _dot(pn, vn_ref[0, :, sl])) / l
        o_ref[0, :, sl] = _head_rmsnorm(o, g_ref[...]).astype(o_ref.dtype)


def _fox_sample_attention(q, kn, vn, kp, vp, ckp, ckn, g):
    b, t, n = q.shape
    past = kp.shape[1]
    nh = n // HEAD_DIM
    kp, vp = _cache_view(kp), _cache_view(vp)
    new = pl.BlockSpec((1, t, n), lambda i: (i, 0, 0))
    old = pl.BlockSpec((1,) + kp.shape[1:], lambda i: (i, 0, 0))
    kern = functools.partial(_fox_sample_kernel, n_heads=nh)
    return pl.pallas_call(
        kern,
        out_shape=jax.ShapeDtypeStruct((b, t, n), BF16),
        grid=(b,),
        in_specs=[new, new, new, old, old,
                  pl.BlockSpec((1, nh, 1, past), lambda i: (i, 0, 0, 0)),
                  pl.BlockSpec((1, nh, 1, t), lambda i: (i, 0, 0, 0)),
                  pl.BlockSpec((1, HEAD_DIM), lambda i: (0, 0))],
        out_specs=new,
        compiler_params=_cparams(("parallel",)), name="fox_sample_attention",
    )(q, kn, vn, kp, vp, ckp, ckn, g)


def _split_bf16(a):
    hi = a.astype(BF16)
    return hi, (a - hi.astype(F32)).astype(BF16)


def _outproj_router_kernel(md_ref, mf_ref, wd_ref, wf_ref, x_ref, g_ref, wr_ref, br_ref,
                           h_ref, hn_ref, idx_ref, gate_ref):
    h = x_ref[...] + _dot(md_ref[...], wd_ref[...]) + _dot(mf_ref[...], wf_ref[...])
    h_ref[...] = h
    hn = h * lax.rsqrt(jnp.mean(h * h, axis=-1, keepdims=True) + RMS_EPS) * g_ref[...]
    for c in range(hn_ref.shape[1]):
        hn_ref[:, c, :] = hn[:, c * LANES:(c + 1) * LANES]
    hn_hi, hn_lo = _split_bf16(hn)
    wr_hi, wr_lo = _split_bf16(wr_ref[...])
    logits = _dot(hn_hi, wr_hi) + (_dot(hn_lo, wr_hi) + _dot(hn_hi, wr_lo)) + br_ref[...]
    lane = lax.broadcasted_iota(jnp.int32, logits.shape, 1)
    work = logits
    idx_out = jnp.zeros(logits.shape, jnp.int32)
    val_out = jnp.zeros(logits.shape, F32)
    top = None
    denom = None
    for k in range(TOP_K):
        mx = jnp.max(work, axis=-1, keepdims=True)
        first = jnp.min(jnp.where(work == mx, lane, LANES), axis=-1, keepdims=True)
        work = jnp.where(lane == first, -jnp.inf, work)
        if k == 0:
            top = mx
        e = jnp.exp(mx - top)
        denom = e if k == 0 else denom + e
        idx_out = jnp.where(lane == k, first, idx_out)
        val_out = jnp.where(lane == k, e, val_out)
    idx_ref[...] = idx_out
    gate_ref[...] = val_out / denom


def _outproj_router(mix_d, mix_f, w_d, w_f, x, g, w_router_pad, b_router_pad):
    m, d = x.shape
    kd, kf = mix_d.shape[1], mix_f.shape[1]
    tm = _tile(m, 256)
    row = lambda w: pl.BlockSpec((tm, w), lambda i: (i, 0))
    full = lambda r, c: pl.BlockSpec((r, c), lambda i: (0, 0))
    return pl.pallas_call(
        _outproj_router_kernel,
        out_shape=[jax.ShapeDtypeStruct((m, d), F32),
                   jax.ShapeDtypeStruct((m, d // LANES, LANES), F32),
                   jax.ShapeDtypeStruct((m, LANES), jnp.int32),
                   jax.ShapeDtypeStruct((m, LANES), F32)],
        grid=(m // tm,),
        in_specs=[row(kd), row(kf), full(kd, d), full(kf, d), row(d), full(1, d),
                  full(d, LANES), full(1, LANES)],
        out_specs=[row(d), pl.BlockSpec((tm, d // LANES, LANES), lambda i: (i, 0, 0)),
                   row(LANES), row(LANES)],
        compiler_params=_cparams(("parallel",)), name="outproj_router",
    )(mix_d, mix_f, w_d, w_f, x, g, w_router_pad, b_router_pad)


def _rank_kernel(idx_ref, rank_ref, count_ref, carry_ref):
    @pl.when(pl.program_id(0) == 0)
    def _():
        carry_ref[...] = jnp.zeros_like(carry_ref)

    tm = idx_ref.shape[0]
    idx = idx_ref[...]
    lane = lax.broadcasted_iota(jnp.int32, (tm, LANES), 1)
    picks = [lane == idx[:, k:k + 1] for k in range(TOP_K)]
    onehot = jnp.zeros((tm, LANES), F32)
    for p in picks:
        onehot = onehot + jnp.where(p, 1.0, 0.0)
    row = lax.broadcasted_iota(jnp.int32, (tm, tm), 0)
    col = lax.broadcasted_iota(jnp.int32, (tm, tm), 1)
    strict = jnp.where(col < row, 1.0, 0.0).astype(BF16)
    before = _dot(strict, onehot.astype(BF16)) + carry_ref[...]
    rank = jnp.zeros((tm, LANES), F32)
    for k, p in enumerate(picks):
        r = jnp.sum(jnp.where(p, before, 0.0), axis=-1, keepdims=True)
        rank = jnp.where(lane == k, r, rank)
    rank_ref[...] = rank.astype(jnp.int32)
    total = before[tm - 1:tm, :] + onehot[tm - 1:tm, :]
    carry_ref[...] = total
    count_ref[...] = total.astype(jnp.int32)


def _expert_ranks(idx):
    n_tok = idx.shape[0]
    tm = _tile(n_tok, 512)
    return pl.pallas_call(
        _rank_kernel,
        out_shape=[jax.ShapeDtypeStruct((n_tok, LANES), jnp.int32),
                   jax.ShapeDtypeStruct((1, LANES), jnp.int32)],
        grid=(n_tok // tm,),
        in_specs=[pl.BlockSpec((tm, LANES), lambda i: (i, 0))],
        out_specs=[pl.BlockSpec((tm, LANES), lambda i: (i, 0)),
                   pl.BlockSpec((1, LANES), lambda i: (0, 0))],
        scratch_shapes=[pltpu.VMEM((1, LANES), F32)],
        compiler_params=_cparams(("arbitrary",)), name="expert_ranks",
    )(idx)


def _row_copy(src_ref, dst_ref, sem, src_row, dst_row):
    return pltpu.make_async_copy(src_ref.at[pl.ds(src_row, 1)], dst_ref.at[pl.ds(dst_row, 1)], sem)


def _gather_rows(idx_ref, src_ref, dst_ref, sem, n_rows):
    assert n_rows % GATHER_UNROLL == 0

    def start(g, carry):
        base = pl.multiple_of(g * GATHER_UNROLL, GATHER_UNROLL)
        for u in range(GATHER_UNROLL):
            _row_copy(src_ref, dst_ref, sem, idx_ref[0, 0, base + u], base + u).start()
        return carry

    def wait(g, carry):
        base = pl.multiple_of(g * GATHER_UNROLL, GATHER_UNROLL)
        for u in range(GATHER_UNROLL):
            _row_copy(src_ref, dst_ref, sem, 0, base + u).wait()
        return carry

    lax.fori_loop(0, n_rows // GATHER_UNROLL, start, 0)
    lax.fori_loop(0, n_rows // GATHER_UNROLL, wait, 0)


def _dispatch_kernel(pad_start_ref, pad_count_ref, dest_ref, hp_ref, hs_ref, xs_ref, zero_buf, sem,
                     *, ts, n_prompt_steps, n_pad):
    i = pl.program_id(0)

    @pl.when(i == 0)
    def _():
        zero_buf[...] = jnp.zeros_like(zero_buf)

        def fill(e, carry):
            def start(r, c):
                pltpu.make_async_copy(zero_buf, xs_ref.at[pad_start_ref[e] + r], sem).start()
                return c

            lax.fori_loop(0, pad_count_ref[e], start, 0)
            return carry

        def drain(e, carry):
            def wait(r, c):
                pltpu.make_async_copy(zero_buf, xs_ref.at[0], sem).wait()
                return c

            lax.fori_loop(0, pad_count_ref[e], wait, 0)
            return carry

        lax.fori_loop(0, n_pad, fill, 0)
        lax.fori_loop(0, n_pad, drain, 0)

    def copy_tokens(src_ref):
        per_trip = GATHER_UNROLL // TOP_K

        def start(g, c):
            for u in range(per_trip):
                tok = g * per_trip + u
                for k in range(TOP_K):
                    pltpu.make_async_copy(src_ref.at[tok],
                                          xs_ref.at[dest_ref[0, 0, tok * TOP_K + k]], sem).start()
            return c

        def wait(g, c):
            for _ in range(GATHER_UNROLL):
                pltpu.make_async_copy(src_ref.at[0], xs_ref.at[0], sem).wait()
            return c

        lax.fori_loop(0, ts // per_trip, start, 0)
        lax.fori_loop(0, ts // per_trip, wait, 0)

    @pl.when(i < n_prompt_steps)
    def _():
        copy_tokens(hp_ref)

    @pl.when(i >= n_prompt_steps)
    def _():
        copy_tokens(hs_ref)


def _dispatch(pad_start, pad_count, dest, hn_prompt, hn_sample, n_rows):
    n_p, n_s = hn_prompt.shape[0], hn_sample.shape[0]
    ts = _tile(math.gcd(n_p, n_s), 512)
    assert ts % (GATHER_UNROLL // TOP_K) == 0
    n_steps = (n_p + n_s) // ts
    kern = functools.partial(_dispatch_kernel, ts=ts, n_prompt_steps=n_p // ts,
                             n_pad=pad_start.shape[0])
    slab = hn_prompt.shape[1:]
    last_p = n_p // ts - 1
    return pl.pallas_call(
        kern,
        out_shape=jax.ShapeDtypeStruct((n_rows,) + slab, hn_prompt.dtype),
        grid_spec=pltpu.PrefetchScalarGridSpec(
            num_scalar_prefetch=2,
            grid=(n_steps,),
            in_specs=[pl.BlockSpec((1, 1, ts * TOP_K), lambda i, ps, pc: (i, 0, 0),
                                   memory_space=pltpu.SMEM),
                      pl.BlockSpec((ts,) + slab, lambda i, ps, pc: (jnp.minimum(i, last_p), 0, 0)),
                      pl.BlockSpec((ts,) + slab,
                                   lambda i, ps, pc: (jnp.maximum(i - last_p - 1, 0), 0, 0))],
            out_specs=pl.BlockSpec(memory_space=pl.ANY),
            scratch_shapes=[pltpu.VMEM(hn_prompt.shape[1:], hn_prompt.dtype),
                            pltpu.SemaphoreType.DMA]),
        compiler_params=_cparams(("arbitrary",)), name="moe_dispatch",
    )(pad_start, pad_count, dest.reshape(n_steps, 1, ts * TOP_K), hn_prompt, hn_sample)


def _combine_kernel(dest_ref, y_ref, h_ref, gate_ref, g_ref, o_ref, buf, sem, *, tm):
    _gather_rows(dest_ref, y_ref, buf, sem, TOP_K * tm)
    gates = gate_ref[...]
    out = h_ref[...]
    for k in range(TOP_K):
        out = out + buf[k * tm:(k + 1) * tm, :] * gates[:, k:k + 1]
    out = out * lax.rsqrt(jnp.mean(out * out, axis=-1, keepdims=True) + RMS_EPS)
    o_ref[...] = out * g_ref[...]


def _combine(dest, y_rows, h, gates, g):
    n_tiles, _, per = dest.shape
    tm = per // TOP_K
    m, d = h.shape
    kern = functools.partial(_combine_kernel, tm=tm)
    return pl.pallas_call(
        kern,
        out_shape=jax.ShapeDtypeStruct((m, d), F32),
        grid=(n_tiles,),
        in_specs=[pl.BlockSpec((1, 1, per), lambda i: (i, 0, 0), memory_space=pltpu.SMEM),
                  pl.BlockSpec(memory_space=pl.ANY),
                  pl.BlockSpec((tm, d), lambda i: (i, 0)),
                  pl.BlockSpec((tm, LANES), lambda i: (i, 0)),
                  pl.BlockSpec((1, d), lambda i: (0, 0))],
        out_specs=pl.BlockSpec((tm, d), lambda i: (i, 0)),
        scratch_shapes=[pltpu.VMEM((per, d), y_rows.dtype), pltpu.SemaphoreType.DMA],
        compiler_params=_cparams(("arbitrary",)), name="moe_combine",
    )(dest, y_rows, h, gates, g)


def _expert_changed(be_ref, rb):
    prev = be_ref[jnp.maximum(rb - 1, 0)]
    return jnp.logical_or(rb == 0, be_ref[rb] != prev)


def _moe_up_kernel(be_ref, used_ref, x_ref, wg_ref, wu_ref, bg_ref, bu_ref, o_ref, wg16, wu16):
    rb = pl.program_id(1)

    @pl.when(_expert_changed(be_ref, rb))
    def _():
        wg16[...] = wg_ref[0].astype(BF16)
        wu16[...] = wu_ref[0].astype(BF16)

    @pl.when(rb < used_ref[0])
    def _():
        rows, n_chunks = o_ref.shape[0], wg16.shape[0] // LANES
        x = jnp.concatenate([x_ref[pl.ds(c, rows, stride=n_chunks), :].astype(BF16)
                             for c in range(n_chunks)], axis=1)
        gate = jnp.minimum(_dot(x, wg16[...]) + bg_ref[0], SWIGLU_LIMIT)
        up = jnp.clip(_dot(x, wu16[...]) + bu_ref[0], -SWIGLU_LIMIT, SWIGLU_LIMIT)
        act = (up + 1.0) * (gate * jax.nn.sigmoid(SWIGLU_ALPHA * gate))
        o_ref[...] = act.astype(o_ref.dtype)

    @pl.when(rb >= used_ref[0])
    def _():
        o_ref[...] = jnp.zeros_like(o_ref)


def _moe_up(block_expert, n_used, xs_slabs, w_gate, w_up, b_gate, b_up, tn):
    n_rows, n_chunks, _ = xs_slabs.shape
    xs = xs_slabs.reshape(n_rows * n_chunks, LANES)
    n_exp, d, d_ff = w_gate.shape
    rows = MOE_ROWS
    n_blocks = n_rows // rows
    wspec = pl.BlockSpec((1, d, tn), lambda n, rb, be, used: (be[rb], 0, n))
    bspec = pl.BlockSpec((1, 1, tn), lambda n, rb, be, used: (be[rb], 0, n))
    return pl.pallas_call(
        _moe_up_kernel,
        out_shape=jax.ShapeDtypeStruct((n_rows, d_ff), BF16),
        grid_spec=pltpu.PrefetchScalarGridSpec(
            num_scalar_prefetch=2,
            grid=(d_ff // tn, n_blocks),
            in_specs=[pl.BlockSpec((rows * n_chunks, LANES), lambda n, rb, be, used: (rb, 0)),
                      wspec, wspec, bspec, bspec],
            out_specs=pl.BlockSpec((rows, tn), lambda n, rb, be, used: (rb, n)),
            scratch_shapes=[pltpu.VMEM((d, tn), BF16), pltpu.VMEM((d, tn), BF16)]),
        compiler_params=_cparams(("arbitrary", "arbitrary")), name="moe_gate_up",
    )(block_expert, n_used, xs, w_gate, w_up, b_gate.reshape(n_exp, 1, d_ff),
      b_up.reshape(n_exp, 1, d_ff))


def _moe_down_kernel(be_ref, used_ref, a_ref, w_ref, b_ref, o_ref, w16):
    rb = pl.program_id(1)

    @pl.when(_expert_changed(be_ref, rb))
    def _():
        w16[...] = w_ref[0].astype(BF16)

    @pl.when(rb < used_ref[0])
    def _():
        o_ref[...] = _dot(a_ref[...], w16[...]) + b_ref[0]

    @pl.when(rb >= used_ref[0])
    def _():
        o_ref[...] = jnp.zeros_like(o_ref)


def _moe_down(block_expert, n_used, act, w_down, b_down, tn):
    n_rows, d_ff = act.shape
    n_exp, _, d = w_down.shape
    rows = MOE_ROWS
    return pl.pallas_call(
        _moe_down_kernel,
        out_shape=jax.ShapeDtypeStruct((n_rows, d), F32),
        grid_spec=pltpu.PrefetchScalarGridSpec(
            num_scalar_prefetch=2,
            grid=(d // tn, n_rows // rows),
            in_specs=[pl.BlockSpec((rows, d_ff), lambda n, rb, be, used: (rb, 0)),
                      pl.BlockSpec((1, d_ff, tn), lambda n, rb, be, used: (be[rb], 0, n)),
                      pl.BlockSpec((1, 1, tn), lambda n, rb, be, used: (be[rb], 0, n))],
            out_specs=pl.BlockSpec((rows, tn), lambda n, rb, be, used: (rb, n)),
            scratch_shapes=[pltpu.VMEM((d_ff, tn), BF16)]),
        compiler_params=_cparams(("arbitrary", "arbitrary")), name="moe_down",
    )(block_expert, n_used, act, w_down, b_down.reshape(n_exp, 1, d))


def _pad_lanes(a, fill=0.0):
    return jnp.pad(a, ((0, 0), (0, LANES - a.shape[1])), constant_values=fill)


def _mixer_projections(x2d, pos, norm_g, w_in, b_forget, head_major):
    d = x2d.shape[1]
    dw = (w_in.shape[1] - 8) // 6
    q_scale = HEAD_DIM ** -0.5 * LOG2E
    xn = _rmsnorm_bf16(x2d, norm_g)
    w16 = w_in.astype(BF16)
    wcol = lambda j: w16[:, j * dw:(j + 1) * dw]
    tabs_q = _rope_tables(pos, q_scale)
    tabs_k = _rope_tables(pos, 1.0)
    hm = dict(head_major=head_major)
    (dq,) = _proj(xn, wcol(0), rope_tabs=tabs_q, head_w=2 * HEAD_DIM, **hm)
    dk32, dk = _proj(xn, wcol(1), rope_tabs=tabs_k, want_f32=True, head_w=2 * HEAD_DIM, **hm)
    dv32, dv = _proj(xn, wcol(2), want_f32=True, head_w=2 * HEAD_DIM, **hm)
    (fq,) = _proj(xn, wcol(3), scale=q_scale, **hm)
    fk32, fk = _proj(xn, wcol(4), want_f32=True, **hm)
    fv32, fv = _proj(xn, wcol(5), want_f32=True, **hm)
    logf = _logf(xn, _pad_lanes(w16[:, 6 * dw:]), _pad_lanes(b_forget.reshape(1, -1)))
    return (dq, dk, dv, fq, fk, fv), (dk32, dv32, fk32, fv32), logf


def _moe(hn_prompt, hn_sample, idx_all, w_gate, w_up, w_down, b_gate, b_up, b_down):
    n_tok = idx_all.shape[0]
    n_exp = w_gate.shape[0]
    rows = MOE_ROWS
    n_blocks = -(-(n_tok * TOP_K) // rows) + n_exp
    n_rows = n_blocks * rows

    rank, counts = _expert_ranks(idx_all)
    experts = idx_all[:, :TOP_K]
    counts = counts[0, :n_exp]
    padded = (counts + rows - 1) // rows * rows
    padded_end = jnp.cumsum(padded)
    padded_start = padded_end - padded
    dest = padded_start[experts] + rank[:, :TOP_K]
    pad_start = jnp.concatenate([padded_start + counts, padded_end[-1:]]).astype(jnp.int32)
    pad_count = jnp.concatenate([padded - counts, n_rows - padded_end[-1:]]).astype(jnp.int32)
    block_start = jnp.arange(n_blocks, dtype=jnp.int32) * rows
    block_expert = jnp.minimum(
        jnp.sum((padded_end[None, :] <= block_start[:, None]).astype(jnp.int32), axis=1), n_exp - 1)
    n_used = (padded_end[-1] // rows).astype(jnp.int32).reshape(1)

    xs = _dispatch(pad_start, pad_count, dest, hn_prompt, hn_sample, n_rows)
    act = _moe_up(block_expert, n_used, xs, w_gate, w_up, b_gate, b_up,
                  tn=min(1024, w_gate.shape[2]))
    y_rows = _moe_down(block_expert, n_used, act, w_down, b_down, tn=min(1024, w_down.shape[2]))
    return y_rows, dest


def _combine_tokens(dest, y_rows, h, gates, g):
    m = h.shape[0]
    tm = _tile(m, 256)
    dest_t = dest.reshape(m // tm, tm, TOP_K).transpose(0, 2, 1).reshape(m // tm, 1, TOP_K * tm)
    return _combine(dest_t, y_rows, h, gates, g.reshape(1, -1))


def kernel(x_prompt, x_sample, cache_diff_k, cache_diff_v, cache_fox_k, cache_fox_v, cache_fox_logf,
           attn_norm_g, w_in, b_forget, lambda_q1, lambda_k1, lambda_q2, lambda_k2,
           diff_out_norm_g, fox_out_norm_g, w_out, ffn_norm_g, w_router, b_router,
           w_gate, b_gate, w_up, b_up, w_down, b_down, final_norm_g):
    depth = w_in.shape[0]
    assert depth == 1, "single-layer stack"
    lam_init = 0.8 - 0.6 * math.exp(-0.3 * 0)
    drop = lambda a: a.reshape(a.shape[1:])
    (cache_diff_k, cache_diff_v, cache_fox_k, cache_fox_v, cache_fox_logf, attn_norm_g, w_in,
     b_forget, lambda_q1, lambda_k1, lambda_q2, lambda_k2, diff_out_norm_g, fox_out_norm_g, w_out,
     ffn_norm_g, w_router, b_router, w_gate, b_gate, w_up, b_up, w_down, b_down) = map(drop, (
         cache_diff_k, cache_diff_v, cache_fox_k, cache_fox_v, cache_fox_logf, attn_norm_g, w_in,
         b_forget, lambda_q1, lambda_k1, lambda_q2, lambda_k2, diff_out_norm_g, fox_out_norm_g,
         w_out, ffn_norm_g, w_router, b_router, w_gate, b_gate, w_up, b_up, w_down, b_down))
    bp, t, d = x_prompt.shape
    assert bp == 1
    bs, ts, _ = x_sample.shape
    past = cache_diff_k.shape[1]
    n_dh = cache_diff_k.shape[2]
    n_fh = cache_fox_k.shape[2]
    dw = n_dh * 2 * HEAD_DIM
    fw = n_fh * HEAD_DIM
    lams = [v.reshape(1, HEAD_DIM) for v in (lambda_q1, lambda_k1, lambda_q2, lambda_k2)]
    diff_g = diff_out_norm_g.reshape(1, -1)
    fox_g = fox_out_norm_g.reshape(1, -1)

    xp = x_prompt.reshape(t, d)
    pos_p = jnp.arange(t, dtype=jnp.int32)
    (dq, dk, dv, fq, fk, fv), cache_p, logf_p = _mixer_projections(
        xp, pos_p, attn_norm_g, w_in, b_forget, head_major=True)
    tq, _ = _sweep_tiles(t)
    c_p = _cumsum_rows(logf_p, _tile(t, 512))
    ck_p = c_p[:, :n_fh].T.reshape(n_fh, t // tq, 1, tq)
    mix_d_p = _diff_prompt_attention(dq, dk, dv, diff_g, lams, lam_init)
    mix_f_p = _fox_prompt_attention(fq, fk, fv, ck_p, fox_g)

    xs = x_sample.reshape(bs * ts, d)
    pos_s = past + jnp.arange(ts, dtype=jnp.int32)
    (sdq, sdk, sdv, sfq, sfk, sfv), cache_s, logf_s = _mixer_projections(
        xs, jnp.tile(pos_s, bs), attn_norm_g, w_in, b_forget, head_major=False)
    logf_all = jnp.concatenate([cache_fox_logf.astype(F32),
                                logf_s[:, :n_fh].reshape(bs, ts, n_fh)], axis=1)
    tt = past + ts
    lf_cols = logf_all.transpose(1, 0, 2).reshape(tt, bs * n_fh)
    pad_c = (-lf_cols.shape[1]) % LANES
    lf_cols = jnp.pad(lf_cols, ((0, 0), (0, pad_c)))
    tb = max(tb_ for tb_ in range(8, 513, 8) if tt % tb_ == 0)
    c_s = _cumsum_rows(lf_cols, tb)[:, :bs * n_fh].reshape(tt, bs, n_fh).transpose(1, 2, 0)
    ckp_s = c_s[:, :, None, :past]
    ckn_s = c_s[:, :, None, past:]
    r3 = lambda a: a.reshape(bs, ts, -1)
    mix_d_s = _diff_sample_attention(
        r3(sdq), r3(sdk), r3(sdv), cache_diff_k, cache_diff_v,
        diff_g, lams, lam_init).reshape(bs * ts, dw)
    mix_f_s = _fox_sample_attention(
        r3(sfq), r3(sfk), r3(sfv), cache_fox_k, cache_fox_v,
        ckp_s, ckn_s, fox_g).reshape(bs * ts, fw)

    w_out16 = w_out.astype(BF16)
    w_router_pad = _pad_lanes(w_router)
    b_router_pad = _pad_lanes(b_router.reshape(1, -1), fill=NEG_INF)
    ffn_g = ffn_norm_g.reshape(1, -1)
    h_p, hn_p, idx_p, gate_p = _outproj_router(mix_d_p, mix_f_p, w_out16[:dw], w_out16[dw:],
                                               xp, ffn_g, w_router_pad, b_router_pad)
    h_s, hn_s, idx_s, gate_s = _outproj_router(mix_d_s, mix_f_s, w_out16[:dw], w_out16[dw:],
                                               xs, ffn_g, w_router_pad, b_router_pad)

    idx_all = jnp.concatenate([idx_p, idx_s], axis=0)
    y_rows, dest = _moe(hn_p, hn_s, idx_all, w_gate, w_up, w_down, b_gate, b_up, b_down)
    y_p = _combine_tokens(dest[:t], y_rows, h_p, gate_p, final_norm_g)
    y_s = _combine_tokens(dest[t:], y_rows, h_s, gate_s, final_norm_g)

    dk32, dv32, fk32, fv32 = cache_p
    sdk32, sdv32, sfk32, sfv32 = cache_s
    return (y_p.reshape(1, t, d), y_s.reshape(bs, ts, d),
            dk32.reshape(1, 1, t, n_dh, 2 * HEAD_DIM), dv32.reshape(1, 1, t, n_dh, 2 * HEAD_DIM),
            fk32.reshape(1, 1, t, n_fh, HEAD_DIM), fv32.reshape(1, 1, t, n_fh, HEAD_DIM),
            logf_p[:, :n_fh].reshape(1, 1, t, n_fh),
            sdk32.reshape(1, bs, ts, n_dh, 2 * HEAD_DIM), sdv32.reshape(1, bs, ts, n_dh, 2 * HEAD_DIM),
            sfk32.reshape(1, bs, ts, n_fh, HEAD_DIM), sfv32.reshape(1, bs, ts, n_fh, HEAD_DIM),
            logf_s[:, :n_fh].reshape(1, bs, ts, n_fh))
```

```python
import functools
import math

import jax
import jax.numpy as jnp
from jax import lax
from jax.experimental import pallas as pl
from jax.experimental.pallas import tpu as pltpu

HEAD_DIM = 128
CHUNK_SHIFT = 6
ROT_DIM = HEAD_DIM // 4
ROPE_THETA = 500000.0
TOP_K = 4
SWIGLU_LIMIT = 7.0
SWIGLU_ALPHA = 1.702
RMS_EPS = 1e-6
NEG_INF = -1e30
LOG2E = 1.4426950408889634
LANES = 128
MOE_ROWS = 512
GATHER_UNROLL = 8
VMEM_LIMIT = 56 * 1024 * 1024

F32 = jnp.float32
BF16 = jnp.bfloat16


def _cparams(sem, vmem=VMEM_LIMIT):
    return pltpu.CompilerParams(dimension_semantics=sem, vmem_limit_bytes=vmem)


def _tile(n, pref):
    if n <= pref:
        return n
    t = max(c for c in range(8, pref + 1, 8) if n % c == 0)
    return t


def _dot_nt(a, b):
    return lax.dot_general(a, b, (((1,), (1,)), ((), ())), preferred_element_type=F32)


def _dot(a, b):
    return jnp.dot(a, b, preferred_element_type=F32)


def _rmsnorm_kernel(x_ref, g_ref, o_ref):
    x = x_ref[...]
    y = x * lax.rsqrt(jnp.mean(x * x, axis=-1, keepdims=True) + RMS_EPS)
    o_ref[...] = (y * g_ref[...]).astype(o_ref.dtype)


def _rmsnorm_bf16(x, g):
    m, d = x.shape
    tm = _tile(m, 512)
    return pl.pallas_call(
        _rmsnorm_kernel,
        out_shape=jax.ShapeDtypeStruct((m, d), BF16),
        grid=(m // tm,),
        in_specs=[pl.BlockSpec((tm, d), lambda i: (i, 0)),
                  pl.BlockSpec((1, d), lambda i: (0, 0))],
        out_specs=pl.BlockSpec((tm, d), lambda i: (i, 0)),
        compiler_params=_cparams(("parallel",)),
        name="rmsnorm_bf16",
    )(x, g.reshape(1, d))


def _proj_kernel(*refs, rope, scale, want_f32, want_bf16, head_major, head_w):
    it = iter(refs)
    x_ref = next(it)
    w_ref = next(it)
    if rope:
        c_ref, sa_ref, sb_ref = next(it), next(it), next(it)
    o32_ref = next(it) if want_f32 else None
    o16_ref = next(it) if want_bf16 else None

    n = w_ref.shape[1]
    y = _dot(x_ref[...], w_ref[...])
    for h in range(n // HEAD_DIM):
        sl = slice(h * HEAD_DIM, (h + 1) * HEAD_DIM)
        yh = y[:, sl]
        if rope:
            yh = (yh * c_ref[...]
                  + pltpu.roll(yh, HEAD_DIM - ROT_DIM // 2, 1) * sa_ref[...]
                  + pltpu.roll(yh, ROT_DIM // 2, 1) * sb_ref[...])
        elif scale != 1.0:
            yh = yh * scale
        per = head_w // HEAD_DIM
        if want_f32:
            o32_ref[:, h // per, (h % per) * HEAD_DIM:(h % per + 1) * HEAD_DIM] = yh
        if want_bf16:
            if head_major:
                o16_ref[h // per, :, (h % per) * HEAD_DIM:(h % per + 1) * HEAD_DIM] = yh.astype(BF16)
            else:
                o16_ref[:, sl] = yh.astype(BF16)


def _proj(xn, w, rope_tabs=None, scale=1.0, want_f32=False, want_bf16=True,
          head_major=False, head_w=HEAD_DIM):
    m, d = xn.shape
    n = w.shape[1]
    tm = _tile(m, 512)
    rope = rope_tabs is not None
    in_specs = [pl.BlockSpec((tm, d), lambda i: (i, 0)),
                pl.BlockSpec((d, n), lambda i: (0, 0))]
    args = [xn, w]
    if rope:
        in_specs += [pl.BlockSpec((tm, HEAD_DIM), lambda i: (i, 0))] * 3
        args += list(rope_tabs)
    out_shape, out_specs = [], []
    if want_f32:
        out_shape.append(jax.ShapeDtypeStruct((m, n // head_w, head_w), F32))
        out_specs.append(pl.BlockSpec((tm, n // head_w, head_w), lambda i: (i, 0, 0)))
    if want_bf16:
        if head_major:
            nh = n // head_w
            out_shape.append(jax.ShapeDtypeStruct((nh, m, head_w), BF16))
            out_specs.append(pl.BlockSpec((nh, tm, head_w), lambda i: (0, i, 0)))
        else:
            out_shape.append(jax.ShapeDtypeStruct((m, n), BF16))
            out_specs.append(pl.BlockSpec((tm, n), lambda i: (i, 0)))
    kern = functools.partial(_proj_kernel, rope=rope, scale=scale, want_f32=want_f32,
                             want_bf16=want_bf16, head_major=head_major, head_w=head_w)
    return pl.pallas_call(
        kern, out_shape=out_shape, grid=(m // tm,), in_specs=in_specs, out_specs=out_specs,
        compiler_params=_cparams(("parallel",)), name="in_proj",
    )(*args)


def _rope_tables(pos, scale):
    half = ROT_DIM // 2
    t = pos.shape[0]
    inv_freq = jnp.power(ROPE_THETA, -jnp.arange(half, dtype=F32) * (2.0 / ROT_DIM))
    ang = pos.astype(F32)[:, None] * inv_freq[None, :]
    cos, sin = jnp.cos(ang), jnp.sin(ang)
    zeros = lambda w: jnp.zeros((t, w), F32)
    c = jnp.concatenate([cos, cos, jnp.ones((t, HEAD_DIM - ROT_DIM), F32)], axis=1)
    sa = jnp.concatenate([-sin, zeros(HEAD_DIM - half)], axis=1)
    sb = jnp.concatenate([zeros(half), sin, zeros(HEAD_DIM - ROT_DIM)], axis=1)
    return c * scale, sa * scale, sb * scale


def _logf_kernel(x_ref, w_ref, b_ref, o_ref):
    z = _dot(x_ref[...], w_ref[...]) + b_ref[...]
    o_ref[...] = jnp.minimum(z, 0.0) - jnp.log(1.0 + jnp.exp(-jnp.abs(z)))


def _logf(xn, w_pad, b_pad):
    m, d = xn.shape
    tm = _tile(m, 512)
    return pl.pallas_call(
        _logf_kernel,
        out_shape=jax.ShapeDtypeStruct((m, LANES), F32),
        grid=(m // tm,),
        in_specs=[pl.BlockSpec((tm, d), lambda i: (i, 0)),
                  pl.BlockSpec((d, LANES), lambda i: (0, 0)),
                  pl.BlockSpec((1, LANES), lambda i: (0, 0))],
        out_specs=pl.BlockSpec((tm, LANES), lambda i: (i, 0)),
        compiler_params=_cparams(("parallel",)), name="forget_logits",
    )(xn, w_pad, b_pad)


def _cumsum_kernel(x_ref, o_ref, carry_ref):
    @pl.when(pl.program_id(0) == 0)
    def _():
        carry_ref[...] = jnp.zeros_like(carry_ref)

    tb = x_ref.shape[0]
    row = lax.broadcasted_iota(jnp.int32, (tb, tb), 0)
    col = lax.broadcasted_iota(jnp.int32, (tb, tb), 1)
    tri = jnp.where(col <= row, 1.0, 0.0).astype(F32)
    c = jnp.dot(tri, x_ref[...], preferred_element_type=F32,
                precision=lax.Precision.HIGHEST) + carry_ref[...]
    o_ref[...] = c
    carry_ref[...] = c[tb - 1:tb, :]


def _cumsum_rows(x, tb):
    t, c = x.shape
    assert t % tb == 0 and c % LANES == 0
    return pl.pallas_call(
        _cumsum_kernel,
        out_shape=jax.ShapeDtypeStruct((t, c), F32),
        grid=(t // tb,),
        in_specs=[pl.BlockSpec((tb, c), lambda i: (i, 0))],
        out_specs=pl.BlockSpec((tb, c), lambda i: (i, 0)),
        scratch_shapes=[pltpu.VMEM((1, c), F32)],
        compiler_params=_cparams(("arbitrary",)), name="cumsum_rows",
    )(x)


def _lambda_value(lq1, lk1, lq2, lk2, lam_init):
    a = jnp.sum(lq1[...] * lk1[...], axis=-1, keepdims=True)
    b = jnp.sum(lq2[...] * lk2[...], axis=-1, keepdims=True)
    return jnp.exp(a) - jnp.exp(b) + lam_init


def _head_rmsnorm(o, g):
    return o * lax.rsqrt(jnp.mean(o * o, axis=-1, keepdims=True) + RMS_EPS) * g


def _online_update(s, m, l, acc_ref, v):
    m_new = jnp.maximum(m, jnp.max(s, axis=-1, keepdims=True))
    alpha = jnp.exp2(m - m_new)
    p = jnp.exp2(s - m_new)
    l_new = alpha * l + jnp.sum(p, axis=-1, keepdims=True)
    acc_ref[...] = alpha * acc_ref[...] + _dot(p.astype(BF16), v)
    return m_new, l_new


def _causal_sweep(n_full, scores_fn, values_fn, mask_fn, s_a, s_b, m_ref, l_ref, acc_refs):
    n_maps = len(acc_refs)
    for n in range(n_maps):
        m_ref[n] = jnp.full(m_ref.shape[1:], NEG_INF, F32)
        l_ref[n] = jnp.zeros(l_ref.shape[1:], F32)
        acc_refs[n][...] = jnp.zeros_like(acc_refs[n])

    def produce(bufs, j):
        for ref, s in zip(bufs, scores_fn(j)):
            ref[...] = s

    def consume(tiles, j):
        v = values_fn(j)
        for n in range(n_maps):
            m_new, l_new = _online_update(tiles[n], m_ref[n], l_ref[n], acc_refs[n], v)
            m_ref[n] = m_new
            l_ref[n] = l_new

    def pair(p, carry):
        produce(s_b, 2 * p + 1)
        consume([r[...] for r in s_a], 2 * p)
        produce(s_a, 2 * p + 2)
        consume([r[...] for r in s_b], 2 * p + 1)
        return carry

    produce(s_a, 0)
    lax.fori_loop(0, n_full // 2, pair, 0)
    diag = [mask_fn(s) for s in scores_fn(n_full)]

    @pl.when(n_full % 2 == 1)
    def _():
        consume([r[...] for r in s_a], n_full - 1)

    consume(diag, n_full)


def _key_tile(ref, j, tk):
    return ref[0, pl.ds(pl.multiple_of(j * tk, tk), tk), :]


def _sweep_tiles(t):
    tq = _tile(t, 512)
    tk = 2 * tq if t % (2 * tq) == 0 else tq
    assert tq % (1 << CHUNK_SHIFT) == 0
    return tq, tk


def _diag_position(i, tq, tk):
    n_full = (i * tq) // tk
    return n_full, i * tq - n_full * tk


def _diff_prompt_kernel(q_ref, k_ref, v_ref, g_ref, lq1, lk1, lq2, lk2, o_ref,
                        acc1, acc2, a1, a2, b1, b2, m_ref, l_ref, *, tq, tk, lam_init):
    n_full, offset = _diag_position(pl.program_id(1), tq, tk)
    q = q_ref[0]
    q1, q2 = q[:, :HEAD_DIM], q[:, HEAD_DIM:]

    def scores(j):
        k = _key_tile(k_ref, j, tk)
        return _dot_nt(q1, k[:, :HEAD_DIM]), _dot_nt(q2, k[:, HEAD_DIM:])

    def mask(s):
        row = lax.broadcasted_iota(jnp.int32, (tq, tk), 0) + offset
        col = lax.broadcasted_iota(jnp.int32, (tq, tk), 1)
        return jnp.where((col >> CHUNK_SHIFT) <= (row >> CHUNK_SHIFT), s, NEG_INF)

    _causal_sweep(n_full, scores, lambda j: _key_tile(v_ref, j, tk), mask,
                  (a1, a2), (b1, b2), m_ref, l_ref, (acc1, acc2))
    lam = _lambda_value(lq1, lk1, lq2, lk2, lam_init)
    o = acc1[...] / l_ref[0] - lam * (acc2[...] / l_ref[1])
    o_ref[...] = (_head_rmsnorm(o, g_ref[...]) * (1.0 - lam_init)).astype(o_ref.dtype)


def _diff_prompt_attention(q, k, v, g, lams, lam_init):
    nh, t, w = q.shape
    tq, tk = _sweep_tiles(t)
    vec = pl.BlockSpec((1, HEAD_DIM), lambda h, i: (0, 0))
    kern = functools.partial(_diff_prompt_kernel, tq=tq, tk=tk, lam_init=lam_init)
    return pl.pallas_call(
        kern,
        out_shape=jax.ShapeDtypeStruct((t, nh * w), BF16),
        grid=(nh, t // tq),
        in_specs=[pl.BlockSpec((1, tq, w), lambda h, i: (h, i, 0)),
                  pl.BlockSpec((1, t, w), lambda h, i: (h, 0, 0)),
                  pl.BlockSpec((1, t, w), lambda h, i: (h, 0, 0)),
                  pl.BlockSpec((1, w), lambda h, i: (0, 0)),
                  vec, vec, vec, vec],
        out_specs=pl.BlockSpec((tq, w), lambda h, i: (i, h)),
        scratch_shapes=[pltpu.VMEM((tq, w), F32)] * 2 + [pltpu.VMEM((tq, tk), F32)] * 4
        + [pltpu.VMEM((2, tq, 1), F32)] * 2,
        compiler_params=_cparams(("parallel", "arbitrary")), name="diff_prompt_attention",
    )(q, k, v, g, *lams)


def _fox_prompt_kernel(q_ref, k_ref, v_ref, ck_ref, g_ref, o_ref, acc, s_a, s_b, m_ref, l_ref,
                       *, tq, tk):
    n_full, offset = _diag_position(pl.program_id(1), tq, tk)
    q = q_ref[0]

    def scores(j):
        return (_dot_nt(q, _key_tile(k_ref, j, tk)) - ck_ref[0, j] * LOG2E,)

    def mask(s):
        row = lax.broadcasted_iota(jnp.int32, (tq, tk), 0) + offset
        col = lax.broadcasted_iota(jnp.int32, (tq, tk), 1)
        return jnp.where(col <= row, s, NEG_INF)

    _causal_sweep(n_full, scores, lambda j: _key_tile(v_ref, j, tk), mask,
                  (s_a,), (s_b,), m_ref, l_ref, (acc,))
    o_ref[...] = _head_rmsnorm(acc[...] / l_ref[0], g_ref[...]).astype(o_ref.dtype)


def _fox_prompt_attention(q, k, v, ck, g):
    nh, t, w = q.shape
    tq, tk = _sweep_tiles(t)
    assert ck.shape == (nh, t // tk, 1, tk)
    kern = functools.partial(_fox_prompt_kernel, tq=tq, tk=tk)
    return pl.pallas_call(
        kern,
        out_shape=jax.ShapeDtypeStruct((t, nh * w), BF16),
        grid=(nh, t // tq),
        in_specs=[pl.BlockSpec((1, tq, w), lambda h, i: (h, i, 0)),
                  pl.BlockSpec((1, t, w), lambda h, i: (h, 0, 0)),
                  pl.BlockSpec((1, t, w), lambda h, i: (h, 0, 0)),
                  pl.BlockSpec((1, t // tk, 1, tk), lambda h, i: (h, 0, 0, 0)),
                  pl.BlockSpec((1, w), lambda h, i: (0, 0))],
        out_specs=pl.BlockSpec((tq, w), lambda h, i: (i, h)),
        scratch_shapes=[pltpu.VMEM((tq, w), F32)] + [pltpu.VMEM((tq, tk), F32)] * 2
        + [pltpu.VMEM((1, tq, 1), F32)] * 2,
        compiler_params=_cparams(("parallel", "arbitrary")), name="fox_prompt_attention",
    )(q, k, v, ck, g)


def _two_segment_softmax(sp, sn):
    m = jnp.maximum(jnp.max(sp, axis=-1, keepdims=True), jnp.max(sn, axis=-1, keepdims=True))
    pp = jnp.exp2(sp - m)
    pn = jnp.exp2(sn - m)
    l = jnp.sum(pp, axis=-1, keepdims=True) + jnp.sum(pn, axis=-1, keepdims=True)
    return pp.astype(BF16), pn.astype(BF16), l


def _cache_view(cache):
    b, t, nh, w = cache.shape
    nc = w // LANES
    return cache.reshape(b, t, nh, nc, LANES).transpose(0, 1, 3, 2, 4).reshape(b, t * nc * nh, LANES)


def _cache_rows(ref, head, lane_block, n_heads, n_lane_blocks, n_rows):
    stride = n_heads * n_lane_blocks
    return ref[0, pl.ds(lane_block * n_heads + head, n_rows, stride=stride), :].astype(BF16)


def _diff_sample_kernel(q_ref, kn_ref, vn_ref, kp_ref, vp_ref, g_ref, lq1, lk1, lq2, lk2,
                        o_ref, *, past_len, lam_init, n_heads):
    t = q_ref.shape[1]
    w = 2 * HEAD_DIM
    row = lax.broadcasted_iota(jnp.int32, (t, t), 0) + past_len
    col = lax.broadcasted_iota(jnp.int32, (t, t), 1) + past_len
    ok = (col >> CHUNK_SHIFT) <= (row >> CHUNK_SHIFT)
    lam = _lambda_value(lq1, lk1, lq2, lk2, lam_init)
    for h in range(n_heads):
        sl = slice(h * w, (h + 1) * w)
        q = q_ref[0, :, sl]
        kn = kn_ref[0, :, sl]
        vn = vn_ref[0, :, sl]
        vp = [_cache_rows(vp_ref, h, c, n_heads, 2, past_len) for c in range(2)]
        outs = []
        for half in range(2):
            hs = slice(half * HEAD_DIM, (half + 1) * HEAD_DIM)
            sp = _dot_nt(q[:, hs], _cache_rows(kp_ref, h, half, n_heads, 2, past_len))
            sn = jnp.where(ok, _dot_nt(q[:, hs], kn[:, hs]), NEG_INF)
            pp, pn, l = _two_segment_softmax(sp, sn)
            o_past = jnp.concatenate([_dot(pp, vp[0]), _dot(pp, vp[1])], axis=1)
            outs.append((o_past + _dot(pn, vn)) / l)
        o = outs[0] - lam * outs[1]
        o_ref[0, :, sl] = (_head_rmsnorm(o, g_ref[...]) * (1.0 - lam_init)).astype(o_ref.dtype)


def _diff_sample_attention(q, kn, vn, kp, vp, g, lams, lam_init):
    b, t, n = q.shape
    past = kp.shape[1]
    kp, vp = _cache_view(kp), _cache_view(vp)
    new = pl.BlockSpec((1, t, n), lambda i: (i, 0, 0))
    old = pl.BlockSpec((1,) + kp.shape[1:], lambda i: (i, 0, 0))
    vec = pl.BlockSpec((1, HEAD_DIM), lambda i: (0, 0))
    kern = functools.partial(_diff_sample_kernel, past_len=past, lam_init=lam_init,
                             n_heads=n // (2 * HEAD_DIM))
    return pl.pallas_call(
        kern,
        out_shape=jax.ShapeDtypeStruct((b, t, n), BF16),
        grid=(b,),
        in_specs=[new, new, new, old, old,
                  pl.BlockSpec((1, 2 * HEAD_DIM), lambda i: (0, 0)), vec, vec, vec, vec],
        out_specs=new,
        compiler_params=_cparams(("parallel",)), name="diff_sample_attention",
    )(q, kn, vn, kp, vp, g, *lams)


def _fox_sample_kernel(q_ref, kn_ref, vn_ref, kp_ref, vp_ref, ckp_ref, ckn_ref, g_ref, o_ref,
                       *, n_heads):
    t = q_ref.shape[1]
    row = lax.broadcasted_iota(jnp.int32, (t, t), 0)
    col = lax.broadcasted_iota(jnp.int32, (t, t), 1)
    ok = col <= row
    for h in range(n_heads):
        sl = slice(h * HEAD_DIM, (h + 1) * HEAD_DIM)
        q = q_ref[0, :, sl]
        past = ckp_ref.shape[-1]
        kp = _cache_rows(kp_ref, h, 0, n_heads, 1, past)
        vp = _cache_rows(vp_ref, h, 0, n_heads, 1, past)
        sp = _dot_nt(q, kp) - ckp_ref[0, h] * LOG2E
        sn = jnp.where(ok, _dot_nt(q, kn_ref[0, :, sl]) - ckn_ref[0, h] * LOG2E, NEG_INF)
        pp, pn, l = _two_segment_softmax(sp, sn)
        o = (_dot(pp, vp) + _dot(pn, vn_ref[0, :, sl])) / l
        o_ref[0, :, sl] = _head_rmsnorm(o, g_ref[...]).astype(o_ref.dtype)


def _fox_sample_attention(q, kn, vn, kp, vp, ckp, ckn, g):
    b, t, n = q.shape
    past = kp.shape[1]
    nh = n // HEAD_DIM
    kp, vp = _cache_view(kp), _cache_view(vp)
    new = pl.BlockSpec((1, t, n), lambda i: (i, 0, 0))
    old = pl.BlockSpec((1,) + kp.shape[1:], lambda i: (i, 0, 0))
    kern = functools.partial(_fox_sample_kernel, n_heads=nh)
    return pl.pallas_call(
        kern,
        out_shape=jax.ShapeDtypeStruct((b, t, n), BF16),
        grid=(b,),
        in_specs=[new, new, new, old, old,
                  pl.BlockSpec((1, nh, 1, past), lambda i: (i, 0, 0, 0)),
                  pl.BlockSpec((1, nh, 1, t), lambda i: (i, 0, 0, 0)),
                  pl.BlockSpec((1, HEAD_DIM), lambda i: (0, 0))],
        out_specs=new,
        compiler_params=_cparams(("parallel",)), name="fox_sample_attention",
    )(q, kn, vn, kp, vp, ckp, ckn, g)


def _split_bf16(a):
    hi = a.astype(BF16)
    return hi, (a - hi.astype(F32)).astype(BF16)


def _outproj_router_kernel(md_ref, mf_ref, wd_ref, wf_ref, x_ref, g_ref, wr_ref, br_ref,
                           h_ref, hn_ref, idx_ref, gate_ref):
    h = x_ref[...] + _dot(md_ref[...], wd_ref[...]) + _dot(mf_ref[...], wf_ref[...])
    h_ref[...] = h
    hn = h * lax.rsqrt(jnp.mean(h * h, axis=-1, keepdims=True) + RMS_EPS) * g_ref[...]
    n_chunks = hn.shape[1] // LANES
    for c in range(n_chunks):
        hn_ref[pl.ds(c, hn.shape[0], stride=n_chunks), :] = hn[:, c * LANES:(c + 1) * LANES]
    hn_hi, hn_lo = _split_bf16(hn)
    wr_hi, wr_lo = _split_bf16(wr_ref[...])
    logits = _dot(hn_hi, wr_hi) + (_dot(hn_lo, wr_hi) + _dot(hn_hi, wr_lo)) + br_ref[...]
    lane = lax.broadcasted_iota(jnp.int32, logits.shape, 1)
    work = logits
    idx_out = jnp.zeros(logits.shape, jnp.int32)
    val_out = jnp.zeros(logits.shape, F32)
    top = None
    denom = None
    for k in range(TOP_K):
        mx = jnp.max(work, axis=-1, keepdims=True)
        first = jnp.min(jnp.where(work == mx, lane, LANES), axis=-1, keepdims=True)
        work = jnp.where(lane == first, -jnp.inf, work)
        if k == 0:
            top = mx
        e = jnp.exp(mx - top)
        denom = e if k == 0 else denom + e
        idx_out = jnp.where(lane == k, first, idx_out)
        val_out = jnp.where(lane == k, e, val_out)
    idx_ref[...] = idx_out
    gate_ref[...] = val_out / denom


def _outproj_router(mix_d, mix_f, w_d, w_f, x, g, w_router_pad, b_router_pad):
    m, d = x.shape
    kd, kf = mix_d.shape[1], mix_f.shape[1]
    tm = _tile(m, 256)
    row = lambda w: pl.BlockSpec((tm, w), lambda i: (i, 0))
    full = lambda r, c: pl.BlockSpec((r, c), lambda i: (0, 0))
    n_chunks = d // LANES
    h, hn_slabs, idx, gates = pl.pallas_call(
        _outproj_router_kernel,
        out_shape=[jax.ShapeDtypeStruct((m, d), F32),
                   jax.ShapeDtypeStruct((m * n_chunks, LANES), F32),
                   jax.ShapeDtypeStruct((m, LANES), jnp.int32),
                   jax.ShapeDtypeStruct((m, LANES), F32)],
        grid=(m // tm,),
        in_specs=[row(kd), row(kf), full(kd, d), full(kf, d), row(d), full(1, d),
                  full(d, LANES), full(1, LANES)],
        out_specs=[row(d), pl.BlockSpec((tm * n_chunks, LANES), lambda i: (i, 0)),
                   row(LANES), row(LANES)],
        compiler_params=_cparams(("parallel",)), name="outproj_router",
    )(mix_d, mix_f, w_d, w_f, x, g, w_router_pad, b_router_pad)
    return h, hn_slabs.reshape(m, n_chunks, LANES), idx, gates


def _rank_kernel(idx_ref, rank_ref, count_ref, carry_ref):
    @pl.when(pl.program_id(0) == 0)
    def _():
        carry_ref[...] = jnp.zeros_like(carry_ref)

    tm = idx_ref.shape[0]
    idx = idx_ref[...]
    lane = lax.broadcasted_iota(jnp.int32, (tm, LANES), 1)
    picks = [lane == idx[:, k:k + 1] for k in range(TOP_K)]
    onehot = jnp.zeros((tm, LANES), F32)
    for p in picks:
        onehot = onehot + jnp.where(p, 1.0, 0.0)
    row = lax.broadcasted_iota(jnp.int32, (tm, tm), 0)
    col = lax.broadcasted_iota(jnp.int32, (tm, tm), 1)
    strict = jnp.where(col < row, 1.0, 0.0).astype(BF16)
    before = _dot(strict, onehot.astype(BF16)) + carry_ref[...]
    rank = jnp.zeros((tm, LANES), F32)
    for k, p in enumerate(picks):
        r = jnp.sum(jnp.where(p, before, 0.0), axis=-1, keepdims=True)
        rank = jnp.where(lane == k, r, rank)
    rank_ref[...] = rank.astype(jnp.int32)
    total = before[tm - 1:tm, :] + onehot[tm - 1:tm, :]
    carry_ref[...] = total
    count_ref[...] = total.astype(jnp.int32)


def _expert_ranks(idx):
    n_tok = idx.shape[0]
    tm = _tile(n_tok, 512)
    return pl.pallas_call(
        _rank_kernel,
        out_shape=[jax.ShapeDtypeStruct((n_tok, LANES), jnp.int32),
                   jax.ShapeDtypeStruct((1, LANES), jnp.int32)],
        grid=(n_tok // tm,),
        in_specs=[pl.BlockSpec((tm, LANES), lambda i: (i, 0))],
        out_specs=[pl.BlockSpec((tm, LANES), lambda i: (i, 0)),
                   pl.BlockSpec((1, LANES), lambda i: (0, 0))],
        scratch_shapes=[pltpu.VMEM((1, LANES), F32)],
        compiler_params=_cparams(("arbitrary",)), name="expert_ranks",
    )(idx)


def _row_copy(src_ref, dst_ref, sem, src_row, dst_row):
    return pltpu.make_async_copy(src_ref.at[pl.ds(src_row, 1)], dst_ref.at[pl.ds(dst_row, 1)], sem)


def _gather_rows(idx_ref, src_ref, dst_ref, sem, n_rows):
    assert n_rows % GATHER_UNROLL == 0

    def start(g, carry):
        base = pl.multiple_of(g * GATHER_UNROLL, GATHER_UNROLL)
        for u in range(GATHER_UNROLL):
            _row_copy(src_ref, dst_ref, sem, idx_ref[0, 0, base + u], base + u).start()
        return carry

    def wait(g, carry):
        base = pl.multiple_of(g * GATHER_UNROLL, GATHER_UNROLL)
        for u in range(GATHER_UNROLL):
            _row_copy(src_ref, dst_ref, sem, 0, base + u).wait()
        return carry

    lax.fori_loop(0, n_rows // GATHER_UNROLL, start, 0)
    lax.fori_loop(0, n_rows // GATHER_UNROLL, wait, 0)


def _dispatch_kernel(pad_start_ref, pad_count_ref, dest_ref, hp_ref, hs_ref, xs_ref, zero_buf, sem,
                     *, ts, n_prompt_steps, n_pad):
    i = pl.program_id(0)

    @pl.when(i == 0)
    def _():
        zero_buf[...] = jnp.zeros_like(zero_buf)

        def fill(e, carry):
            def start(r, c):
                pltpu.make_async_copy(zero_buf, xs_ref.at[pad_start_ref[e] + r], sem).start()
                return c

            def wait(r, c):
                pltpu.make_async_copy(zero_buf, xs_ref.at[0], sem).wait()
                return c

            lax.fori_loop(0, pad_count_ref[e], start, 0)
            lax.fori_loop(0, pad_count_ref[e], wait, 0)
            return carry

        lax.fori_loop(0, n_pad, fill, 0)

    def copy_tokens(src_ref):
        per_trip = GATHER_UNROLL // TOP_K

        def start(g, c):
            for u in range(per_trip):
                tok = g * per_trip + u
                for k in range(TOP_K):
                    pltpu.make_async_copy(src_ref.at[tok],
                                          xs_ref.at[dest_ref[0, 0, tok * TOP_K + k]], sem).start()
            return c

        def wait(g, c):
            for _ in range(GATHER_UNROLL):
                pltpu.make_async_copy(src_ref.at[0], xs_ref.at[0], sem).wait()
            return c

        lax.fori_loop(0, ts // per_trip, start, 0)
        lax.fori_loop(0, ts // per_trip, wait, 0)

    @pl.when(i < n_prompt_steps)
    def _():
        copy_tokens(hp_ref)

    @pl.when(i >= n_prompt_steps)
    def _():
        copy_tokens(hs_ref)


def _dispatch(pad_start, pad_count, dest, hn_prompt, hn_sample, n_rows):
    n_p, n_s = hn_prompt.shape[0], hn_sample.shape[0]
    ts = _tile(math.gcd(n_p, n_s), 512)
    assert ts % (GATHER_UNROLL // TOP_K) == 0
    n_steps = (n_p + n_s) // ts
    kern = functools.partial(_dispatch_kernel, ts=ts, n_prompt_steps=n_p // ts,
                             n_pad=pad_start.shape[0])
    slab = hn_prompt.shape[1:]
    last_p = n_p // ts - 1
    return pl.pallas_call(
        kern,
        out_shape=jax.ShapeDtypeStruct((n_rows,) + slab, hn_prompt.dtype),
        grid_spec=pltpu.PrefetchScalarGridSpec(
            num_scalar_prefetch=2,
            grid=(n_steps,),
            in_specs=[pl.BlockSpec((1, 1, ts * TOP_K), lambda i, ps, pc: (i, 0, 0),
                                   memory_space=pltpu.SMEM),
                      pl.BlockSpec((ts,) + slab, lambda i, ps, pc: (jnp.minimum(i, last_p), 0, 0)),
                      pl.BlockSpec((ts,) + slab,
                                   lambda i, ps, pc: (jnp.maximum(i - last_p - 1, 0), 0, 0))],
            out_specs=pl.BlockSpec(memory_space=pl.ANY),
            scratch_shapes=[pltpu.VMEM(hn_prompt.shape[1:], hn_prompt.dtype),
                            pltpu.SemaphoreType.DMA]),
        compiler_params=_cparams(("arbitrary",)), name="moe_dispatch",
    )(pad_start, pad_count, dest.reshape(n_steps, 1, ts * TOP_K), hn_prompt, hn_sample)


def _combine_kernel(dest_ref, y_ref, h_ref, gate_ref, g_ref, o_ref, buf, sem, *, tm):
    _gather_rows(dest_ref, y_ref, buf, sem, TOP_K * tm)
    gates = gate_ref[...]
    out = h_ref[...]
    for k in range(TOP_K):
        out = out + buf[k * tm:(k + 1) * tm, :] * gates[:, k:k + 1]
    out = out * lax.rsqrt(jnp.mean(out * out, axis=-1, keepdims=True) + RMS_EPS)
    o_ref[...] = out * g_ref[...]


def _combine(dest, y_rows, h, gates, g):
    n_tiles, _, per = dest.shape
    tm = per // TOP_K
    m, d = h.shape
    kern = functools.partial(_combine_kernel, tm=tm)
    return pl.pallas_call(
        kern,
        out_shape=jax.ShapeDtypeStruct((m, d), F32),
        grid=(n_tiles,),
        in_specs=[pl.BlockSpec((1, 1, per), lambda i: (i, 0, 0), memory_space=pltpu.SMEM),
                  pl.BlockSpec(memory_space=pl.ANY),
                  pl.BlockSpec((tm, d), lambda i: (i, 0)),
                  pl.BlockSpec((tm, LANES), lambda i: (i, 0)),
                  pl.BlockSpec((1, d), lambda i: (0, 0))],
        out_specs=pl.BlockSpec((tm, d), lambda i: (i, 0)),
        scratch_shapes=[pltpu.VMEM((per, d), y_rows.dtype), pltpu.SemaphoreType.DMA],
        compiler_params=_cparams(("arbitrary",)), name="moe_combine",
    )(dest, y_rows, h, gates, g)


def _expert_changed(be_ref, rb):
    prev = be_ref[jnp.maximum(rb - 1, 0)]
    return jnp.logical_or(rb == 0, be_ref[rb] != prev)


def _moe_up_kernel(be_ref, used_ref, x_ref, wg_ref, wu_ref, bg_ref, bu_ref, o_ref, wg16, wu16):
    rb = pl.program_id(1)

    @pl.when(_expert_changed(be_ref, rb))
    def _():
        wg16[...] = wg_ref[0].astype(BF16)
        wu16[...] = wu_ref[0].astype(BF16)

    @pl.when(rb < used_ref[0])
    def _():
        rows, n_chunks = o_ref.shape[0], wg16.shape[0] // LANES
        x = jnp.concatenate([x_ref[pl.ds(c, rows, stride=n_chunks), :].astype(BF16)
                             for c in range(n_chunks)], axis=1)
        gate = jnp.minimum(_dot(x, wg16[...]) + bg_ref[0], SWIGLU_LIMIT)
        up = jnp.clip(_dot(x, wu16[...]) + bu_ref[0], -SWIGLU_LIMIT, SWIGLU_LIMIT)
        act = (up + 1.0) * (gate * jax.nn.sigmoid(SWIGLU_ALPHA * gate))
        o_ref[...] = act.astype(o_ref.dtype)

    @pl.when(rb >= used_ref[0])
    def _():
        o_ref[...] = jnp.zeros_like(o_ref)


def _moe_up(block_expert, n_used, xs_slabs, w_gate, w_up, b_gate, b_up, tn):
    n_rows, n_chunks, _ = xs_slabs.shape
    xs = xs_slabs.reshape(n_rows * n_chunks, LANES)
    n_exp, d, d_ff = w_gate.shape
    rows = MOE_ROWS
    n_blocks = n_rows // rows
    wspec = pl.BlockSpec((1, d, tn), lambda n, rb, be, used: (be[rb], 0, n))
    bspec = pl.BlockSpec((1, 1, tn), lambda n, rb, be, used: (be[rb], 0, n))
    return pl.pallas_call(
        _moe_up_kernel,
        out_shape=jax.ShapeDtypeStruct((n_rows, d_ff), BF16),
        grid_spec=pltpu.PrefetchScalarGridSpec(
            num_scalar_prefetch=2,
            grid=(d_ff // tn, n_blocks),
            in_specs=[pl.BlockSpec((rows * n_chunks, LANES), lambda n, rb, be, used: (rb, 0)),
                      wspec, wspec, bspec, bspec],
            out_specs=pl.BlockSpec((rows, tn), lambda n, rb, be, used: (rb, n)),
            scratch_shapes=[pltpu.VMEM((d, tn), BF16), pltpu.VMEM((d, tn), BF16)]),
        compiler_params=_cparams(("arbitrary", "arbitrary")), name="moe_gate_up",
    )(block_expert, n_used, xs, w_gate, w_up, b_gate.reshape(n_exp, 1, d_ff),
      b_up.reshape(n_exp, 1, d_ff))


def _moe_down_kernel(be_ref, used_ref, a_ref, w_ref, b_ref, o_ref, w16):
    rb = pl.program_id(1)

    @pl.when(_expert_changed(be_ref, rb))
    def _():
        w16[...] = w_ref[0].astype(BF16)

    @pl.when(rb < used_ref[0])
    def _():
        o_ref[...] = _dot(a_ref[...], w16[...]) + b_ref[0]

    @pl.when(rb >= used_ref[0])
    def _():
        o_ref[...] = jnp.zeros_like(o_ref)


def _moe_down(block_expert, n_used, act, w_down, b_down, tn):
    n_rows, d_ff = act.shape
    n_exp, _, d = w_down.shape
    rows = MOE_ROWS
    return pl.pallas_call(
        _moe_down_kernel,
        out_shape=jax.ShapeDtypeStruct((n_rows, d), F32),
        grid_spec=pltpu.PrefetchScalarGridSpec(
            num_scalar_prefetch=2,
            grid=(d // tn, n_rows // rows),
            in_specs=[pl.BlockSpec((rows, d_ff), lambda n, rb, be, used: (rb, 0)),
                      pl.BlockSpec((1, d_ff, tn), lambda n, rb, be, used: (be[rb], 0, n)),
                      pl.BlockSpec((1, 1, tn), lambda n, rb, be, used: (be[rb], 0, n))],
            out_specs=pl.BlockSpec((rows, tn), lambda n, rb, be, used: (rb, n)),
            scratch_shapes=[pltpu.VMEM((d_ff, tn), BF16)]),
        compiler_params=_cparams(("arbitrary", "arbitrary")), name="moe_down",
    )(block_expert, n_used, act, w_down, b_down.reshape(n_exp, 1, d))


def _pad_lanes(a, fill=0.0):
    return jnp.pad(a, ((0, 0), (0, LANES - a.shape[1])), constant_values=fill)


def _mixer_projections(x2d, pos, norm_g, w_in, b_forget, head_major):
    d = x2d.shape[1]
    dw = (w_in.shape[1] - 8) // 6
    q_scale = HEAD_DIM ** -0.5 * LOG2E
    xn = _rmsnorm_bf16(x2d, norm_g)
    w16 = w_in.astype(BF16)
    wcol = lambda j: w16[:, j * dw:(j + 1) * dw]
    tabs_q = _rope_tables(pos, q_scale)
    tabs_k = _rope_tables(pos, 1.0)
    hm = dict(head_major=head_major)
    (dq,) = _proj(xn, wcol(0), rope_tabs=tabs_q, head_w=2 * HEAD_DIM, **hm)
    dk32, dk = _proj(xn, wcol(1), rope_tabs=tabs_k, want_f32=True, head_w=2 * HEAD_DIM, **hm)
    dv32, dv = _proj(xn, wcol(2), want_f32=True, head_w=2 * HEAD_DIM, **hm)
    (fq,) = _proj(xn, wcol(3), scale=q_scale, **hm)
    fk32, fk = _proj(xn, wcol(4), want_f32=True, **hm)
    fv32, fv = _proj(xn, wcol(5), want_f32=True, **hm)
    logf = _logf(xn, _pad_lanes(w16[:, 6 * dw:]), _pad_lanes(b_forget.reshape(1, -1)))
    return (dq, dk, dv, fq, fk, fv), (dk32, dv32, fk32, fv32), logf


def _moe(hn_prompt, hn_sample, idx_all, w_gate, w_up, w_down, b_gate, b_up, b_down):
    n_tok = idx_all.shape[0]
    n_exp = w_gate.shape[0]
    rows = MOE_ROWS
    n_blocks = -(-(n_tok * TOP_K) // rows) + n_exp
    n_rows = n_blocks * rows

    rank, counts = _expert_ranks(idx_all)
    experts = idx_all[:, :TOP_K]
    counts = counts[0, :n_exp]
    padded = (counts + rows - 1) // rows * rows
    padded_end = jnp.cumsum(padded)
    padded_start = padded_end - padded
    dest = padded_start[experts] + rank[:, :TOP_K]
    pad_start = jnp.concatenate([padded_start + counts, padded_end[-1:]]).astype(jnp.int32)
    pad_count = jnp.concatenate([padded - counts, n_rows - padded_end[-1:]]).astype(jnp.int32)
    block_start = jnp.arange(n_blocks, dtype=jnp.int32) * rows
    block_expert = jnp.minimum(
        jnp.sum((padded_end[None, :] <= block_start[:, None]).astype(jnp.int32), axis=1), n_exp - 1)
    n_used = (padded_end[-1] // rows).astype(jnp.int32).reshape(1)

    xs = _dispatch(pad_start, pad_count, dest, hn_prompt, hn_sample, n_rows)
    act = _moe_up(block_expert, n_used, xs, w_gate, w_up, b_gate, b_up,
                  tn=min(1024, w_gate.shape[2]))
    y_rows = _moe_down(block_expert, n_used, act, w_down, b_down, tn=min(1024, w_down.shape[2]))
    return y_rows, dest


def _combine_tokens(dest, y_rows, h, gates, g):
    m = h.shape[0]
    tm = _tile(m, 512)
    dest_t = dest.reshape(m // tm, tm, TOP_K).transpose(0, 2, 1).reshape(m // tm, 1, TOP_K * tm)
    return _combine(dest_t, y_rows, h, gates, g.reshape(1, -1))


def kernel(x_prompt, x_sample, cache_diff_k, cache_diff_v, cache_fox_k, cache_fox_v, cache_fox_logf,
           attn_norm_g, w_in, b_forget, lambda_q1, lambda_k1, lambda_q2, lambda_k2,
           diff_out_norm_g, fox_out_norm_g, w_out, ffn_norm_g, w_router, b_router,
           w_gate, b_gate, w_up, b_up, w_down, b_down, final_norm_g):
    depth = w_in.shape[0]
    assert depth == 1, "single-layer stack"
    lam_init = 0.8 - 0.6 * math.exp(-0.3 * 0)
    drop = lambda a: a.reshape(a.shape[1:])
    (cache_diff_k, cache_diff_v, cache_fox_k, cache_fox_v, cache_fox_logf, attn_norm_g, w_in,
     b_forget, lambda_q1, lambda_k1, lambda_q2, lambda_k2, diff_out_norm_g, fox_out_norm_g, w_out,
     ffn_norm_g, w_router, b_router, w_gate, b_gate, w_up, b_up, w_down, b_down) = map(drop, (
         cache_diff_k, cache_diff_v, cache_fox_k, cache_fox_v, cache_fox_logf, attn_norm_g, w_in,
         b_forget, lambda_q1, lambda_k1, lambda_q2, lambda_k2, diff_out_norm_g, fox_out_norm_g,
         w_out, ffn_norm_g, w_router, b_router, w_gate, b_gate, w_up, b_up, w_down, b_down))
    bp, t, d = x_prompt.shape
    assert bp == 1
    bs, ts, _ = x_sample.shape
    past = cache_diff_k.shape[1]
    n_dh = cache_diff_k.shape[2]
    n_fh = cache_fox_k.shape[2]
    dw = n_dh * 2 * HEAD_DIM
    fw = n_fh * HEAD_DIM
    lams = [v.reshape(1, HEAD_DIM) for v in (lambda_q1, lambda_k1, lambda_q2, lambda_k2)]
    diff_g = diff_out_norm_g.reshape(1, -1)
    fox_g = fox_out_norm_g.reshape(1, -1)

    xp = x_prompt.reshape(t, d)
    pos_p = jnp.arange(t, dtype=jnp.int32)
    (dq, dk, dv, fq, fk, fv), cache_p, logf_p = _mixer_projections(
        xp, pos_p, attn_norm_g, w_in, b_forget, head_major=True)
    _, tk = _sweep_tiles(t)
    c_p = _cumsum_rows(logf_p, _tile(t, 512))
    ck_p = c_p[:, :n_fh].T.reshape(n_fh, t // tk, 1, tk)
    mix_d_p = _diff_prompt_attention(dq, dk, dv, diff_g, lams, lam_init)
    mix_f_p = _fox_prompt_attention(fq, fk, fv, ck_p, fox_g)

    xs = x_sample.reshape(bs * ts, d)
    pos_s = past + jnp.arange(ts, dtype=jnp.int32)
    (sdq, sdk, sdv, sfq, sfk, sfv), cache_s, logf_s = _mixer_projections(
        xs, jnp.tile(pos_s, bs), attn_norm_g, w_in, b_forget, head_major=False)
    logf_all = jnp.concatenate([cache_fox_logf.astype(F32),
                                logf_s[:, :n_fh].reshape(bs, ts, n_fh)], axis=1)
    tt = past + ts
    lf_cols = logf_all.transpose(1, 0, 2).reshape(tt, bs * n_fh)
    pad_c = (-lf_cols.shape[1]) % LANES
    lf_cols = jnp.pad(lf_cols, ((0, 0), (0, pad_c)))
    tb = max(tb_ for tb_ in range(8, 513, 8) if tt % tb_ == 0)
    c_s = _cumsum_rows(lf_cols, tb)[:, :bs * n_fh].reshape(tt, bs, n_fh).transpose(1, 2, 0)
    ckp_s = c_s[:, :, None, :past]
    ckn_s = c_s[:, :, None, past:]
    r3 = lambda a: a.reshape(bs, ts, -1)
    mix_d_s = _diff_sample_attention(
        r3(sdq), r3(sdk), r3(sdv), cache_diff_k, cache_diff_v,
        diff_g, lams, lam_init).reshape(bs * ts, dw)
    mix_f_s = _fox_sample_attention(
        r3(sfq), r3(sfk), r3(sfv), cache_fox_k, cache_fox_v,
        ckp_s, ckn_s, fox_g).reshape(bs * ts, fw)

    w_out16 = w_out.astype(BF16)
    w_router_pad = _pad_lanes(w_router)
    b_router_pad = _pad_lanes(b_router.reshape(1, -1), fill=NEG_INF)
    ffn_g = ffn_norm_g.reshape(1, -1)
    h_p, hn_p, idx_p, gate_p = _outproj_router(mix_d_p, mix_f_p, w_out16[:dw], w_out16[dw:],
                                               xp, ffn_g, w_router_pad, b_router_pad)
    h_s, hn_s, idx_s, gate_s = _outproj_router(mix_d_s, mix_f_s, w_out16[:dw], w_out16[dw:],
                                               xs, ffn_g, w_router_pad, b_router_pad)

    idx_all = jnp.concatenate([idx_p, idx_s], axis=0)
    y_rows, dest = _moe(hn_p, hn_s, idx_all, w_gate, w_up, w_down, b_gate, b_up, b_down)
    y_p = _combine_tokens(dest[:t], y_rows, h_p, gate_p, final_norm_g)
    y_s = _combine_tokens(dest[t:], y_rows, h_s, gate_s, final_norm_g)

    dk32, dv32, fk32, fv32 = cache_p
    sdk32, sdv32, sfk32, sfv32 = cache_s
    return (y_p.reshape(1, t, d), y_s.reshape(bs, ts, d),
            dk32.reshape(1, 1, t, n_dh, 2 * HEAD_DIM), dv32.reshape(1, 1, t, n_dh, 2 * HEAD_DIM),
            fk32.reshape(1, 1, t, n_fh, HEAD_DIM), fv32.reshape(1, 1, t, n_fh, HEAD_DIM),
            logf_p[:, :n_fh].reshape(1, 1, t, n_fh),
            sdk32.reshape(1, bs, ts, n_dh, 2 * HEAD_DIM), sdv32.reshape(1, bs, ts, n_dh, 2 * HEAD_DIM),
            sfk32.reshape(1, bs, ts, n_fh, HEAD_DIM), sfv32.reshape(1, bs, ts, n_fh, HEAD_DIM),
            logf_s[:, :n_fh].reshape(1, bs, ts, n_fh))
```
